```python
import jax
import jax.numpy as jnp
from jax import lax
import numpy as np

D_MODEL = 1024
BATCH = 4
SEQ = 8192
DEPTH = 4

GRID_W = 64
CTX_LEN = 256
N_MIXERS = 4
BLOCK = 128
WINDOW = 128
WIN_HEADS = 16
WIN_KV = 4
WIN_HD = 64
AX_HEADS = 8
AX_KV = 2
AX_HD = 128
ML_HEADS = 4
ML_DK = 128
ML_DV = 256
ML_CHUNK = 128
D_FF = 2816
ROPE_THETA = 10000.0
EPS = 1e-6
NEG = -1e30

kernel_name = 'hybrid_interleaved_diffusion_block'


def rms_norm(x, g):
    xf = x.astype(jnp.float32)
    y = xf * lax.rsqrt(jnp.mean(xf * xf, axis=-1, keepdims=True) + EPS)
    return (y * g.astype(jnp.float32)).astype(x.dtype)


def dwconv3(x, w):
    xp = jnp.pad(x, ((0, 0), (1, 1), (0, 0)))
    return xp[:, :-2] * w[0] + xp[:, 1:-1] * w[1] + xp[:, 2:] * w[2]


def adaln(cvec, w, b):
    m = (jax.nn.silu(cvec) @ w + b)[..., None, :]
    return jnp.split(m, 6, axis=-1)


def modulate(x, shift, scale):
    return x * (1 + scale) + shift


def axial_rope(n_tok, hd):
    rows = n_tok // GRID_W
    row = jnp.repeat(jnp.arange(rows, dtype=jnp.float32), GRID_W)
    col = jnp.tile(jnp.arange(GRID_W, dtype=jnp.float32), rows)
    n_freq = hd // 4
    inv = jnp.power(ROPE_THETA, -jnp.arange(n_freq, dtype=jnp.float32) / n_freq)
    ang = jnp.concatenate([row[:, None] * inv, col[:, None] * inv], axis=-1)
    return jnp.cos(ang), jnp.sin(ang)


def apply_rope(t, cos, sin):
    t1, t2 = jnp.split(t, 2, axis=-1)
    cos = cos.astype(t.dtype)
    sin = sin.astype(t.dtype)
    return jnp.concatenate([t1 * cos - t2 * sin, t2 * cos + t1 * sin], axis=-1)


def split_heads(t, n_heads, hd):
    b, s, _ = t.shape
    return t.reshape(b, s, n_heads, hd).transpose(0, 2, 1, 3)


def merge_heads(o):
    b, _, _, s, _ = o.shape
    return o.transpose(0, 3, 1, 2, 4).reshape(b, s, -1)


def qkv_heads(a, w_qkv, q_norm, k_norm, n_heads, n_kv, hd):
    b, s, _ = a.shape
    q, k, v = jnp.split(a @ w_qkv, [n_heads * hd, (n_heads + n_kv) * hd], axis=-1)
    q = rms_norm(split_heads(q, n_heads, hd), q_norm).reshape(b, n_kv, n_heads // n_kv, s, hd)
    k = rms_norm(split_heads(k, n_kv, hd), k_norm)
    v = split_heads(v, n_kv, hd)
    return q, k, v


def gqa_softmax(q, k, v, mask, sink):
    s = jnp.einsum('bkgqd,bkjd->bkgqj', q, k, preferred_element_type=jnp.float32) * (q.shape[-1] ** -0.5)
    if mask is not None:
        s = jnp.where(mask, s, NEG)
    if sink is None:
        p = jax.nn.softmax(s, axis=-1)
    else:
        sk = sink.astype(jnp.float32)[None, :, :, None, None]
        m = jnp.maximum(s.max(axis=-1, keepdims=True), sk)
        e = jnp.exp(s - m)
        p = e / (e.sum(axis=-1, keepdims=True) + jnp.exp(sk - m))
    return jnp.einsum('bkgqj,bkjd->bkgqd', p.astype(v.dtype), v)


def window_attention_mixer(a_lat, a_ctx, w_qkv, q_norm, k_norm, sink, w_o, ctx_out):
    b, s, _ = a_lat.shape
    nb = s // BLOCK
    g = WIN_HEADS // WIN_KV
    span = BLOCK + 2 * WINDOW
    q, k, v = qkv_heads(a_lat, w_qkv, q_norm, k_norm, WIN_HEADS, WIN_KV, WIN_HD)
    qc, kc, vc = qkv_heads(a_ctx, w_qkv, q_norm, k_norm, WIN_HEADS, WIN_KV, WIN_HD)
    cos, sin = axial_rope(s, WIN_HD)
    q = apply_rope(q, cos, sin)
    k = apply_rope(k, cos, sin)
    sink = sink.reshape(WIN_KV, g)
    kp = jnp.pad(k, ((0, 0), (0, 0), (WINDOW, WINDOW), (0, 0)))
    vp = jnp.pad(v, ((0, 0), (0, 0), (WINDOW, WINDOW), (0, 0)))
    qb = jnp.moveaxis(q.reshape(b, WIN_KV, g, nb, BLOCK, WIN_HD), 3, 0)
    offs = jnp.arange(span) - WINDOW
    ctx_cols = jnp.ones((BLOCK, kc.shape[2]), bool)

    def block(args):
        bi, qblk = args
        start = bi * BLOCK
        kw = jnp.concatenate([lax.dynamic_slice_in_dim(kp, start, span, axis=2), kc], axis=2)
        vw = jnp.concatenate([lax.dynamic_slice_in_dim(vp, start, span, axis=2), vc], axis=2)
        qpos = start + jnp.arange(BLOCK)
        kpos = start + offs
        band = (jnp.abs(qpos[:, None] - kpos[None, :]) <= WINDOW) & (kpos >= 0)[None, :] & (kpos < s)[None, :]
        return gqa_softmax(qblk, kw, vw, jnp.concatenate([band, ctx_cols], axis=1), sink)

    o = lax.map(block, (jnp.arange(nb), qb))
    o = jnp.moveaxis(o, 0, 3).reshape(b, WIN_KV, g, s, WIN_HD)
    y_lat = merge_heads(o) @ w_o
    y_ctx = merge_heads(gqa_softmax(qc, kc, vc, None, sink)) @ w_o if ctx_out else None
    return y_lat, y_ctx


def short_conv_mixer(a, w_in, conv_w, w_out):
    gate_b, gate_c, u = jnp.split(a @ w_in, 3, axis=-1)
    return (gate_b * dwconv3(gate_c * u, conv_w)) @ w_out


def axial_attention_mixer(a_lat, a_ctx, w_qkv, q_norm, k_norm, w_o, ctx_out):
    b, s, _ = a_lat.shape
    nb = s // BLOCK
    g = AX_HEADS // AX_KV
    q, k, v = qkv_heads(a_lat, w_qkv, q_norm, k_norm, AX_HEADS, AX_KV, AX_HD)
    qc, kc, vc = qkv_heads(a_ctx, w_qkv, q_norm, k_norm, AX_HEADS, AX_KV, AX_HD)
    cos, sin = axial_rope(s, AX_HD)
    q = apply_rope(q, cos, sin)
    k = apply_rope(k, cos, sin)
    k_all = jnp.concatenate([k, kc], axis=2)
    v_all = jnp.concatenate([v, vc], axis=2)
    qb = jnp.moveaxis(q.reshape(b, AX_KV, g, nb, BLOCK, AX_HD), 3, 0)

    def block(qblk):
        return gqa_softmax(qblk, k_all, v_all, None, None)

    o = lax.map(block, qb)
    o = jnp.moveaxis(o, 0, 3).reshape(b, AX_KV, g, s, AX_HD)
    y_lat = merge_heads(o) @ w_o
    y_ctx = merge_heads(gqa_softmax(qc, kc, vc, None, None)) @ w_o if ctx_out else None
    return y_lat, y_ctx


def mlstm_chunk_summary(k, v, log_i, log_f):
    cum = jnp.cumsum(log_f, axis=-1)
    a = cum[..., -1:] - cum + log_i
    m = a.max(axis=-1)
    w = jnp.exp(a - m[..., None])
    c_bar = jnp.einsum('...lv,...lk->...vk', v * w[..., None], k)
    n_bar = jnp.einsum('...l,...lk->...k', w, k)
    return c_bar, n_bar, m, cum


def mlstm_final_state(k, v, log_i, log_f):
    c_bar, n_bar, m, _ = mlstm_chunk_summary(k, v, log_i, log_f)
    return (c_bar, n_bar, m)


def mlstm_chunkwise(q, k, v, log_i, log_f, state):
    b, h, s, dk = q.shape
    dv = v.shape[-1]
    nc = s // ML_CHUNK
    qc = q.reshape(b, h, nc, ML_CHUNK, dk)
    kc = k.reshape(b, h, nc, ML_CHUNK, dk)
    vc = v.reshape(b, h, nc, ML_CHUNK, dv)
    lic = log_i.reshape(b, h, nc, ML_CHUNK)
    lfc = log_f.reshape(b, h, nc, ML_CHUNK)
    c_bar, n_bar, m_bar, cum = mlstm_chunk_summary(kc, vc, lic, lfc)

    def step(carry, xs):
        c_prev, n_prev, m_prev = carry
        cb, nb_, mb, gb = xs
        m_new = jnp.maximum(gb + m_prev, mb)
        decay = jnp.exp(gb + m_prev - m_new)
        inj = jnp.exp(mb - m_new)
        c_new = decay[..., None, None] * c_prev + inj[..., None, None] * cb
        n_new = decay[..., None] * n_prev + inj[..., None] * nb_
        return (c_new, n_new, m_new), (c_prev, n_prev, m_prev)

    xs = (jnp.moveaxis(c_bar, 2, 0), jnp.moveaxis(n_bar, 2, 0), jnp.moveaxis(m_bar, 2, 0), jnp.moveaxis(cum[..., -1], 2, 0))
    final, (c_start, n_start, m_start) = lax.scan(step, state, xs)
    c_start = jnp.moveaxis(c_start, 0, 2)
    n_start = jnp.moveaxis(n_start, 0, 2)
    m_start = jnp.moveaxis(m_start, 0, 2)
    tri = jnp.tril(jnp.ones((ML_CHUNK, ML_CHUNK), bool))
    d = jnp.where(tri, cum[..., :, None] - cum[..., None, :] + lic[..., None, :], NEG)
    inter = cum + m_start[..., None]
    m_t = jnp.maximum(inter, d.max(axis=-1))
    w_intra = jnp.exp(d - m_t[..., None])
    w_inter = jnp.exp(inter - m_t)
    qk = jnp.einsum('bhcld,bhcsd->bhcls', qc, kc) * w_intra
    num = jnp.einsum('bhcls,bhcsv->bhclv', qk, vc) + w_inter[..., None] * jnp.einsum('bhcld,bhcvd->bhclv', qc, c_start)
    den = qk.sum(axis=-1) + w_inter * jnp.einsum('bhcld,bhcd->bhcl', qc, n_start)
    out = num / jnp.maximum(jnp.abs(den), jnp.exp(-m_t))[..., None]
    return out.reshape(b, h, s, dv), final


def mlstm_mixer(a_lat, a_ctx, w_in, b_gate, h_norm, w_out, ctx_out):
    qk_w = ML_HEADS * ML_DK
    v_w = ML_HEADS * ML_DV

    def proj(a):
        b, s, _ = a.shape
        q, k, v, o, gates = jnp.split(a @ w_in, [qk_w, 2 * qk_w, 2 * qk_w + v_w, 2 * qk_w + 2 * v_w], axis=-1)
        q = split_heads(q, ML_HEADS, ML_DK).astype(jnp.float32) * (ML_DK ** -0.5)
        k = split_heads(k, ML_HEADS, ML_DK).astype(jnp.float32)
        v = split_heads(v, ML_HEADS, ML_DV).astype(jnp.float32)
        gates = (gates + b_gate).astype(jnp.float32).reshape(b, s, 4, ML_HEADS).transpose(2, 0, 3, 1)
        fwd = (gates[0], jax.nn.log_sigmoid(gates[1]))
        bwd = (gates[2], jax.nn.log_sigmoid(gates[3]))
        return q, k, v, o, fwd, bwd

    def flip(t):
        return jnp.flip(t, axis=2)

    def readout(h_f, h_b, o):
        hs = rms_norm(h_f + flip(h_b), h_norm)
        b, h, s, dv = hs.shape
        hs = hs.transpose(0, 2, 1, 3).reshape(b, s, h * dv).astype(o.dtype)
        return (jax.nn.sigmoid(o) * hs) @ w_out

    qc, kc, vc, oc, gfc, gbc = proj(a_ctx)
    bsz = a_ctx.shape[0]
    zero = (jnp.zeros((bsz, ML_HEADS, ML_DV, ML_DK), jnp.float32),
            jnp.zeros((bsz, ML_HEADS, ML_DK), jnp.float32),
            jnp.full((bsz, ML_HEADS), NEG, jnp.float32))
    if ctx_out:
        hc_f, st_f = mlstm_chunkwise(qc, kc, vc, gfc[0], gfc[1], zero)
        hc_b, st_b = mlstm_chunkwise(flip(qc), flip(kc), flip(vc), flip(gbc[0]), flip(gbc[1]), zero)
        y_ctx = readout(hc_f, hc_b, oc)
    else:
        st_f = mlstm_final_state(kc, vc, gfc[0], gfc[1])
        st_b = mlstm_final_state(flip(kc), flip(vc), flip(gbc[0]), flip(gbc[1]))
        y_ctx = None
    ql, kl, vl, ol, gfl, gbl = proj(a_lat)
    hl_f, _ = mlstm_chunkwise(ql, kl, vl, gfl[0], gfl[1], st_f)
    hl_b, _ = mlstm_chunkwise(flip(ql), flip(kl), flip(vl), flip(gbl[0]), flip(gbl[1]), st_b)
    return readout(hl_f, hl_b, ol), y_ctx


def conv_ffn(a, w_up, conv_w, w_down):
    g, u = jnp.split(dwconv3(a @ w_up, conv_w), 2, axis=-1)
    return (jax.nn.silu(g) * u) @ w_down


def setup_inputs(seed: int = 0) -> dict:
    d = D_MODEL
    n_win, n_sc, n_ax, n_ml = [len(range(kind, DEPTH, N_MIXERS)) for kind in range(N_MIXERS)]
    keys = iter(jax.random.split(jax.random.key(seed), 40))

    def normal(shape, std):
        return std * jax.random.normal(next(keys), shape, jnp.float32)

    def gain(shape):
        return 1.0 + normal(shape, 0.1)

    ml_in = 2 * ML_HEADS * ML_DK + 2 * ML_HEADS * ML_DV + 4 * ML_HEADS
    f_base = jnp.linspace(3.0, 6.0, ML_HEADS, dtype=jnp.float32)
    zeros_h = jnp.zeros((ML_HEADS,), jnp.float32)
    gate_base = jnp.concatenate([zeros_h, f_base, zeros_h, f_base])
    return {
        'x': normal((BATCH, SEQ, d), 1.0),
        'c': normal((BATCH, d), 1.0),
        'ctx': normal((BATCH, CTX_LEN, d), 1.0),
        'c_ctx': normal((d,), 1.0),
        'ada_w': normal((DEPTH, d, 6 * d), 0.5 * d ** -0.5),
        'ada_b': normal((DEPTH, 6 * d), 0.02),
        'norm_mix': gain((DEPTH, d)),
        'norm_ffn': gain((DEPTH, d)),
        'ffn_w_up': normal((DEPTH, d, 2 * D_FF), d ** -0.5),
        'ffn_conv': normal((DEPTH, 3, 2 * D_FF), 3 ** -0.5),
        'ffn_w_down': normal((DEPTH, D_FF, d), D_FF ** -0.5),
        'win_w_qkv': normal((n_win, d, (WIN_HEADS + 2 * WIN_KV) * WIN_HD), d ** -0.5),
        'win_q_norm': gain((n_win, WIN_HD)),
        'win_k_norm': gain((n_win, WIN_HD)),
        'win_sink': normal((n_win, WIN_HEADS), 0.5),
        'win_w_o': normal((n_win, WIN_HEADS * WIN_HD, d), (WIN_HEADS * WIN_HD) ** -0.5),
        'sc_w_in': normal((n_sc, d, 3 * d), d ** -0.5),
        'sc_conv': normal((n_sc, 3, d), 3 ** -0.5),
        'sc_w_out': normal((n_sc, d, d), d ** -0.5),
        'ax_w_qkv': normal((n_ax, d, (AX_HEADS + 2 * AX_KV) * AX_HD), d ** -0.5),
        'ax_q_norm': gain((n_ax, AX_HD)),
        'ax_k_norm': gain((n_ax, AX_HD)),
        'ax_w_o': normal((n_ax, AX_HEADS * AX_HD, d), (AX_HEADS * AX_HD) ** -0.5),
        'ml_w_in': normal((n_ml, d, ml_in), d ** -0.5),
        'ml_b_gate': gate_base + normal((n_ml, 4 * ML_HEADS), 0.1),
        'ml_h_norm': gain((n_ml, ML_DV)),
        'ml_w_out': normal((n_ml, ML_HEADS * ML_DV, d), (ML_HEADS * ML_DV) ** -0.5),
    }


def reference(x, c, ctx, c_ctx, ada_w, ada_b, norm_mix, norm_ffn, ffn_w_up, ffn_conv, ffn_w_down,
              win_w_qkv, win_q_norm, win_k_norm, win_sink, win_w_o,
              sc_w_in, sc_conv, sc_w_out,
              ax_w_qkv, ax_q_norm, ax_k_norm, ax_w_o,
              ml_w_in, ml_b_gate, ml_h_norm, ml_w_out):
    h, hc = x, ctx
    for i in range(DEPTH):
        kind, j = i % N_MIXERS, i // N_MIXERS
        ctx_out = i < DEPTH - 1
        sh1, sc1, g1, sh2, sc2, g2 = adaln(c, ada_w[i], ada_b[i])
        a_lat = modulate(rms_norm(h, norm_mix[i]), sh1, sc1)
        if ctx_out or kind != 1:
            csh1, csc1, cg1, csh2, csc2, cg2 = adaln(c_ctx, ada_w[i], ada_b[i])
            a_ctx = modulate(rms_norm(hc, norm_mix[i]), csh1, csc1)
        if kind == 0:
            y_lat, y_ctx = window_attention_mixer(a_lat, a_ctx, win_w_qkv[j], win_q_norm[j], win_k_norm[j],
                                                  win_sink[j], win_w_o[j], ctx_out)
        elif kind == 1:
            y_lat = short_conv_mixer(a_lat, sc_w_in[j], sc_conv[j], sc_w_out[j])
            y_ctx = short_conv_mixer(a_ctx, sc_w_in[j], sc_conv[j], sc_w_out[j]) if ctx_out else None
        elif kind == 2:
            y_lat, y_ctx = axial_attention_mixer(a_lat, a_ctx, ax_w_qkv[j], ax_q_norm[j], ax_k_norm[j],
                                                 ax_w_o[j], ctx_out)
        else:
            y_lat, y_ctx = mlstm_mixer(a_lat, a_ctx, ml_w_in[j], ml_b_gate[j], ml_h_norm[j], ml_w_out[j], ctx_out)
        h = h + g1 * y_lat
        h = h + g2 * conv_ffn(modulate(rms_norm(h, norm_ffn[i]), sh2, sc2), ffn_w_up[i], ffn_conv[i], ffn_w_down[i])
        if ctx_out:
            hc = hc + cg1 * y_ctx
            hc = hc + cg2 * conv_ffn(modulate(rms_norm(hc, norm_ffn[i]), csh2, csc2),
                                     ffn_w_up[i], ffn_conv[i], ffn_w_down[i])
    return h
```

```python
import functools

import numpy as np
import jax
import jax.numpy as jnp
from jax import lax
from jax.experimental import pallas as pl
from jax.experimental.pallas import tpu as pltpu

D_MODEL = 1024
DEPTH = 4
GRID_W = 64
N_MIXERS = 4
BLOCK = 128
WINDOW = 128
WIN_HEADS = 16
WIN_KV = 4
WIN_HD = 64
AX_HEADS = 8
AX_KV = 2
AX_HD = 128
ML_HEADS = 4
ML_DK = 128
ML_DV = 256
ML_CHUNK = 128
D_FF = 2816
ROPE_THETA = 10000.0
EPS = 1e-6
NEG = -1e30

F32 = jnp.float32
BF16 = jnp.bfloat16

LANES = 128
SUBLANES = 8
TM = 256
HALO = SUBLANES
CONV_CHUNK = 256
FLASH_TQ = 256
FLASH_CK = 512
VMEM_LIMIT = 56 * 1024 * 1024


def _cparams(sem):
    return pltpu.CompilerParams(dimension_semantics=sem, vmem_limit_bytes=VMEM_LIMIT)


def _resident(shape):
    nd = len(shape)
    return pl.BlockSpec(shape, lambda *_: (0,) * nd, pipeline_mode=pl.Buffered(1))


def _dot(a, b):
    return jnp.dot(a, b, preferred_element_type=F32)


def _dot_nt(a, b):
    return lax.dot_general(a, b, (((1,), (1,)), ((), ())), preferred_element_type=F32)


def _dot_tn(a, b):
    return lax.dot_general(a, b, (((0,), (0,)), ((), ())), preferred_element_type=F32)


def _sigmoid(x):
    return 1.0 / (1.0 + jnp.exp(-x))


def _rms_mod(x, g, shift, scale):
    y = x * lax.rsqrt(jnp.mean(x * x, axis=-1, keepdims=True) + EPS) * g
    return y * (1.0 + scale) + shift


def _ada_kernel(c_ref, w_ref, b_ref, o_ref):
    c = c_ref[...]
    o_ref[0] = _dot(c * _sigmoid(c), w_ref[0]) + b_ref[0]


def _ada_call(cvec, ada_w, ada_b):
    depth, d, n = ada_w.shape
    nb = n // 4
    return pl.pallas_call(
        _ada_kernel,
        grid=(depth, n // nb),
        in_specs=[pl.BlockSpec(cvec.shape, lambda l, j: (0, 0)),
                  pl.BlockSpec((1, d, nb), lambda l, j: (l, 0, j)),
                  pl.BlockSpec((1, 1, nb), lambda l, j: (l, 0, j))],
        out_specs=pl.BlockSpec((1, cvec.shape[0], nb), lambda l, j: (l, 0, j)),
        out_shape=jax.ShapeDtypeStruct((depth, cvec.shape[0], n), F32),
        compiler_params=_cparams(("arbitrary", "arbitrary")),
        name="ada",
    )(cvec, ada_w, ada_b.reshape(depth, 1, n))


def _inproj_kernel(kind, h_ref, mod_ref, g_ref, w_ref, *rest):
    m = mod_ref[0]
    a = _rms_mod(h_ref[0], g_ref[...], m[0:1], m[1:2]).astype(BF16)
    y = _dot(a, w_ref[...])
    tm = y.shape[0]
    if kind in ("win", "ax"):
        gain_ref, cos_ref, sin_ref, o_ref = rest
        cos = cos_ref[...]
        sin = sin_ref[...]
        lane = lax.broadcasted_iota(jnp.int32, (tm, LANES), 1)
        n_rot = gain_ref.shape[1] // LANES
        for t in range(n_rot):
            sl = slice(t * LANES, (t + 1) * LANES)
            xt = y[:, sl]
            x2 = xt * xt
            if kind == "win":
                lo = lane < WIN_HD
                s_lo = jnp.sum(jnp.where(lo, x2, 0.0), axis=-1, keepdims=True)
                s_hi = jnp.sum(jnp.where(lo, 0.0, x2), axis=-1, keepdims=True)
                ms = jnp.where(lo, s_lo, s_hi) * (1.0 / WIN_HD)
                xn = xt * lax.rsqrt(ms + EPS) * gain_ref[:, sl]
                first = (lane % WIN_HD) < (WIN_HD // 2)
                sw = jnp.where(first, pltpu.roll(xn, LANES - WIN_HD // 2, 1), pltpu.roll(xn, WIN_HD // 2, 1))
            else:
                ms = jnp.mean(x2, axis=-1, keepdims=True)
                xn = xt * lax.rsqrt(ms + EPS) * gain_ref[:, sl]
                sw = pltpu.roll(xn, AX_HD // 2, 1)
            o_ref[0, :, sl] = (xn * cos + sw * sin).astype(BF16)
        o_ref[0, :, n_rot * LANES:] = y[:, n_rot * LANES:].astype(BF16)
    else:
        bg_ref, qkv_ref, og_ref, gate_ref = rest
        nqk = 2 * ML_HEADS * ML_DK
        nv = ML_HEADS * ML_DV
        nq = ML_HEADS * ML_DK
        qkv_ref[0, :, :nq] = (y[:, :nq] * ML_DK ** -0.5).astype(BF16)
        qkv_ref[0, :, nq:nqk + nv] = y[:, nq:nqk + nv].astype(BF16)
        og_ref[0] = y[:, nqk + nv:nqk + 2 * nv]
        g = y[:, nqk + 2 * nv:] + bg_ref[...]
        lane = lax.broadcasted_iota(jnp.int32, g.shape, 1)
        is_forget = (lane % (2 * ML_HEADS)) >= ML_HEADS
        log_sig = jnp.minimum(g, 0.0) - jnp.log1p(jnp.exp(-jnp.abs(g)))
        gate_ref[0] = jnp.where(is_forget, log_sig, g)


def _mod_spec(n_lat_tiles, ctx_row):
    return pl.BlockSpec((1, 6, D_MODEL), lambda b, j: (jnp.where(j < n_lat_tiles, b, ctx_row), 0, 0))


def _inproj_call(kind, h, mods, gain, w, extras, n_lat_tiles):
    b, t, d = h.shape
    n = w.shape[1]
    ctx_row = mods.shape[0] - 1
    in_specs = [pl.BlockSpec((1, TM, d), lambda bi, j: (bi, j, 0)),
                _mod_spec(n_lat_tiles, ctx_row),
                _resident((1, d)),
                _resident(w.shape)]
    if kind in ("win", "ax"):
        hgain, cos, sin = extras
        in_specs += [_resident(hgain.shape),
                     pl.BlockSpec((TM, LANES), lambda bi, j: (j, 0)),
                     pl.BlockSpec((TM, LANES), lambda bi, j: (j, 0))]
        out_shape = jax.ShapeDtypeStruct((b, t, n), BF16)
        out_specs = pl.BlockSpec((1, TM, n), lambda bi, j: (bi, j, 0))
        args = (hgain, cos, sin)
    else:
        (bg,) = extras
        nqkv = 2 * ML_HEADS * ML_DK + ML_HEADS * ML_DV
        nv = ML_HEADS * ML_DV
        in_specs += [_resident(bg.shape)]
        out_shape = (jax.ShapeDtypeStruct((b, t, nqkv), BF16),
                     jax.ShapeDtypeStruct((b, t, nv), F32),
                     jax.ShapeDtypeStruct((b, t, LANES), F32))
        out_specs = (pl.BlockSpec((1, TM, nqkv), lambda bi, j: (bi, j, 0)),
                     pl.BlockSpec((1, TM, nv), lambda bi, j: (bi, j, 0)),
                     pl.BlockSpec((1, TM, LANES), lambda bi, j: (bi, j, 0)))
        args = (bg,)
    return pl.pallas_call(
        functools.partial(_inproj_kernel, kind),
        grid=(b, t // TM),
        in_specs=in_specs, out_specs=out_specs, out_shape=out_shape,
        compiler_params=_cparams(("parallel", "arbitrary")),
        name="inproj_" + kind,
    )(h, mods, gain, w, *args)


def _win_heads(q_ref, pieces, sink_ref, o_ref):
    tq = q_ref.shape[1]
    lane = lax.broadcasted_iota(jnp.int32, (tq, LANES), 1)
    lo = lane < WIN_HD
    group = WIN_HEADS // WIN_KV
    for tile in range(WIN_HEADS // 2):
        qt = q_ref[0, :, tile * LANES:(tile + 1) * LANES].astype(F32)
        outs = []
        for half in range(2):
            head = 2 * tile + half
            kv = head // group
            kt, khalf = kv // 2, kv % 2
            qh = jnp.where(lo, qt, 0.0) if half == 0 else jnp.where(lo, 0.0, qt)
            if half != khalf:
                qh = pltpu.roll(qh, WIN_HD, 1)
            qh = qh.astype(BF16)
            sink = sink_ref[head:head + 1, :][:, 0:1]
            ss = []
            mx = None
            for k, _, mask in pieces:
                s = _dot_nt(qh, k[:, kt * LANES:(kt + 1) * LANES])
                if mask is not None:
                    s = jnp.where(mask, s, NEG)
                ss.append(s)
                pm = jnp.max(s, axis=-1, keepdims=True)
                mx = pm if mx is None else jnp.maximum(mx, pm)
            mx = jnp.maximum(mx, sink)
            den = jnp.exp(sink - mx)
            acc = None
            for s, (_, v, _) in zip(ss, pieces):
                e = jnp.exp(s - mx)
                den = den + jnp.sum(e, axis=-1, keepdims=True)
                pv = _dot(e.astype(BF16), v[:, kt * LANES:(kt + 1) * LANES])
                acc = pv if acc is None else acc + pv
            o = acc / den
            if half != khalf:
                o = pltpu.roll(o, WIN_HD, 1)
            outs.append(o)
        o_ref[0, :, tile * LANES:(tile + 1) * LANES] = jnp.where(lo, outs[0], outs[1]).astype(BF16)


def _win_kernel(n_lat, q_ref, kp, kc, kn, kx, vp, vc, vn, vx, sink_ref, o_ref):
    i = pl.program_id(1)
    tq = q_ref.shape[1]

    @pl.when(i < n_lat)
    def _():
        r = lax.broadcasted_iota(jnp.int32, (tq, tq), 0)
        c = lax.broadcasted_iota(jnp.int32, (tq, tq), 1)
        mask_p = (c >= r) & (i > 0)
        mask_n = (c <= r) & (i < n_lat - 1)
        pieces = [(kp[0], vp[0], mask_p), (kc[0], vc[0], None), (kn[0], vn[0], mask_n), (kx[0], vx[0], None)]
        _win_heads(q_ref, pieces, sink_ref, o_ref)

    @pl.when(i >= n_lat)
    def _():
        _win_heads(q_ref, [(kx[0], vx[0], None)], sink_ref, o_ref)


def _win_call(qkv, sink_b, s_lat, n_ctx):
    b, t, _ = qkv.shape
    n_lat = s_lat // BLOCK
    nq = t // BLOCK
    dq = WIN_HEADS * WIN_HD
    kw = WIN_KV * WIN_HD
    kcol = dq // kw
    vcol = kcol + 1
    last = n_lat - 1
    xrow = s_lat // n_ctx

    def kv_specs(col):
        return [pl.BlockSpec((1, BLOCK, kw), lambda bi, i: (bi, jnp.clip(i - 1, 0, last), col)),
                pl.BlockSpec((1, BLOCK, kw), lambda bi, i: (bi, jnp.minimum(i, last), col)),
                pl.BlockSpec((1, BLOCK, kw), lambda bi, i: (bi, jnp.minimum(i + 1, last), col)),
                pl.BlockSpec((1, n_ctx, kw), lambda bi, i: (bi, xrow, col))]

    return pl.pallas_call(
        functools.partial(_win_kernel, n_lat),
        grid=(b, nq),
        in_specs=[pl.BlockSpec((1, BLOCK, dq), lambda bi, i: (bi, i, 0))] + kv_specs(kcol) + kv_specs(vcol)
                 + [_resident(sink_b.shape)],
        out_specs=pl.BlockSpec((1, BLOCK, dq), lambda bi, i: (bi, i, 0)),
        out_shape=jax.ShapeDtypeStruct((b, t, dq), BF16),
        compiler_params=_cparams(("parallel", "arbitrary")),
        name="win_attn",
    )(qkv, *([qkv] * 8), sink_b)


def _flash_kernel(n_lat_q, n_full, s_lat, n_ctx, q_ref, k_ref, v_ref, o_ref, m_sc, l_sc, acc_sc):
    qi = pl.program_id(2)
    tq = q_ref.shape[1]
    group = AX_HEADS // AX_KV
    q = jnp.concatenate([q_ref[0, :, g * AX_HD:(g + 1) * AX_HD] for g in range(group)], axis=0)

    m_sc[...] = jnp.full(m_sc.shape, NEG, F32)
    l_sc[...] = jnp.zeros(l_sc.shape, F32)
    acc_sc[...] = jnp.zeros(acc_sc.shape, F32)

    def chunk(start, size):
        k = k_ref[0, pl.ds(start, size), :]
        v = v_ref[0, pl.ds(start, size), :]
        s = _dot_nt(q, k)
        m_prev = m_sc[...]
        m_next = jnp.maximum(m_prev, jnp.max(s, axis=-1, keepdims=True))
        p = jnp.exp(s - jnp.concatenate([m_next] * (size // LANES), axis=1))
        alpha = jnp.exp(m_prev - m_next)
        l_sc[...] = alpha * l_sc[...] + jnp.sum(p, axis=-1, keepdims=True)
        acc_sc[...] = acc_sc[...] * alpha + _dot(p.astype(BF16), v)
        m_sc[...] = m_next

    @pl.when(qi < n_lat_q)
    def _():
        def body(c, carry):
            chunk(pl.multiple_of(c * FLASH_CK, FLASH_CK), FLASH_CK)
            return carry
        lax.fori_loop(0, n_full, body, 0)

    chunk(s_lat, n_ctx)
    o = acc_sc[...] / l_sc[...]
    for g in range(group):
        o_ref[0, :, g * AX_HD:(g + 1) * AX_HD] = o[g * tq:(g + 1) * tq].astype(BF16)


def _flash_call(qkv, s_lat, n_ctx):
    b, t, _ = qkv.shape
    tq = FLASH_TQ
    group = AX_HEADS // AX_KV
    gw = group * AX_HD
    dq = AX_HEADS * AX_HD
    kcol = dq // AX_HD
    vcol = kcol + AX_KV
    m_rows = group * tq
    return pl.pallas_call(
        functools.partial(_flash_kernel, s_lat // tq, s_lat // FLASH_CK, s_lat, n_ctx),
        grid=(b, AX_KV, t // tq),
        in_specs=[pl.BlockSpec((1, tq, gw), lambda bi, j, i: (bi, i, j)),
                  pl.BlockSpec((1, t, AX_HD), lambda bi, j, i: (bi, 0, kcol + j)),
                  pl.BlockSpec((1, t, AX_HD), lambda bi, j, i: (bi, 0, vcol + j))],
        out_specs=pl.BlockSpec((1, tq, gw), lambda bi, j, i: (bi, i, j)),
        out_shape=jax.ShapeDtypeStruct((b, t, dq), BF16),
        scratch_shapes=[pltpu.VMEM((m_rows, LANES), F32), pltpu.VMEM((m_rows, LANES), F32),
                        pltpu.VMEM((m_rows, AX_HD), F32)],
        compiler_params=_cparams(("parallel", "parallel", "arbitrary")),
        name="flash_attn",
    )(qkv, qkv, qkv)


def _mlstm_kernel(qf_ref, gf_ref, qb_ref, gb_ref, hf_ref, hb_ref, c_sc, n_sc, m_sc):
    step = pl.program_id(1)
    L = ML_CHUNK

    @pl.when(step == 0)
    def _():
        c_sc[...] = jnp.zeros(c_sc.shape, F32)
        n_sc[...] = jnp.zeros(n_sc.shape, F32)
        m_sc[...] = jnp.full(m_sc.shape, NEG, F32)

    r = lax.broadcasted_iota(jnp.int32, (L, L), 0)
    c = lax.broadcasted_iota(jnp.int32, (L, L), 1)
    lo = c <= r
    up = c >= r
    lo_f = lo.astype(F32)
    up_f = up.astype(F32)
    hi = lax.Precision.HIGHEST
    nqk = ML_HEADS * ML_DK

    for d, (x_ref, g_ref, out_ref) in enumerate(((qf_ref, gf_ref, hf_ref), (qb_ref, gb_ref, hb_ref))):
        gates = g_ref[0]
        gates_t = gates.T
        tri_col, tri_row, mask = (lo_f, up_f, lo) if d == 0 else (up_f, lo_f, up)
        cum_col = jnp.dot(tri_col, gates, precision=hi, preferred_element_type=F32)
        cum_row = jnp.dot(gates_t, tri_row, precision=hi, preferred_element_type=F32)
        total = jnp.sum(gates, axis=0, keepdims=True)
        for h in range(ML_HEADS):
            ci = 2 * ML_HEADS * d + h
            cf = ci + ML_HEADS
            q = x_ref[0, :, h * ML_DK:(h + 1) * ML_DK]
            k = x_ref[0, :, nqk + h * ML_DK:nqk + (h + 1) * ML_DK]
            v = x_ref[0, :, 2 * nqk + h * ML_DV:2 * nqk + (h + 1) * ML_DV]
            c_prev = c_sc[d, h]
            n_prev = n_sc[d, h][0:1, :]
            m_prev = m_sc[d, h][0:1, 0:1]
            f_col = cum_col[:, cf:cf + 1]
            f_row = cum_row[cf:cf + 1, :]
            i_col = gates[:, ci:ci + 1]
            i_row = gates_t[ci:ci + 1, :]
            f_tot = total[:, cf:cf + 1]

            dmat = jnp.where(mask, f_col - f_row + i_row, NEG)
            inter = f_col + m_prev
            m_t = jnp.maximum(inter, jnp.max(dmat, axis=-1, keepdims=True))
            w_intra = jnp.exp(dmat - m_t)
            w_inter = jnp.exp(inter - m_t)
            qk = _dot_nt(q, k) * w_intra
            qn = jnp.sum(q.astype(F32) * n_prev, axis=-1, keepdims=True)
            num = _dot(qk.astype(BF16), v) + w_inter * _dot_nt(q, c_prev.astype(BF16))
            den = jnp.sum(qk, axis=-1, keepdims=True) + w_inter * qn
            out_ref[0, :, h * ML_DV:(h + 1) * ML_DV] = num / jnp.maximum(jnp.abs(den), jnp.exp(-m_t))

            a_col = f_tot - f_col + i_col
            m_bar = jnp.max(a_col, axis=0, keepdims=True)
            w_col = jnp.exp(a_col - m_bar)
            c_bar = _dot_tn((v.astype(F32) * w_col).astype(BF16), k)
            n_bar = jnp.sum(k.astype(F32) * w_col, axis=0, keepdims=True)
            m_new = jnp.maximum(f_tot + m_prev, m_bar)
            decay = jnp.exp(f_tot + m_prev - m_new)
            inj = jnp.exp(m_bar - m_new)
            c_sc[d, h] = decay * c_prev + inj * c_bar
            n_sc[d, h] = jnp.broadcast_to(decay * n_prev + inj * n_bar, (SUBLANES, ML_DK))
            m_sc[d, h] = jnp.broadcast_to(m_new, (SUBLANES, LANES))


def _mlstm_call(qkv, gates, s_lat):
    b, t, nx = qkv.shape
    nc = t // ML_CHUNK
    nlc = s_lat // ML_CHUNK
    nv = ML_HEADS * ML_DV

    def fwd(bi, s):
        return (bi, (s + nlc) % nc, 0)

    def bwd(bi, s):
        return (bi, nc - 1 - s, 0)

    return pl.pallas_call(
        _mlstm_kernel,
        grid=(b, nc),
        in_specs=[pl.BlockSpec((1, ML_CHUNK, nx), fwd), pl.BlockSpec((1, ML_CHUNK, LANES), fwd),
                  pl.BlockSpec((1, ML_CHUNK, nx), bwd), pl.BlockSpec((1, ML_CHUNK, LANES), bwd)],
        out_specs=(pl.BlockSpec((1, ML_CHUNK, nv), fwd), pl.BlockSpec((1, ML_CHUNK, nv), bwd)),
        out_shape=(jax.ShapeDtypeStruct((b, t, nv), F32), jax.ShapeDtypeStruct((b, t, nv), F32)),
        scratch_shapes=[pltpu.VMEM((2, ML_HEADS, ML_DV, ML_DK), F32),
                        pltpu.VMEM((2, ML_HEADS, SUBLANES, ML_DK), F32),
                        pltpu.VMEM((2, ML_HEADS, SUBLANES, LANES), F32)],
        compiler_params=_cparams(("parallel", "arbitrary")),
        name="mlstm",
    )(qkv, gates, qkv, gates)


def _outproj_kernel(kind, *refs):
    if kind == "attn":
        y_ref, h_ref, mod_ref, w_ref, o_ref = refs
        y = y_ref[0]
    else:
        hf_ref, hb_ref, og_ref, hn_ref, h_ref, mod_ref, w_ref, o_ref = refs
        hs = hf_ref[0] + hb_ref[0]
        parts = []
        for h in range(ML_HEADS):
            sl = slice(h * ML_DV, (h + 1) * ML_DV)
            x = hs[:, sl]
            xn = x * lax.rsqrt(jnp.mean(x * x, axis=-1, keepdims=True) + EPS) * hn_ref[:, sl]
            parts.append((_sigmoid(og_ref[0, :, sl]) * xn).astype(BF16))
        y = jnp.concatenate(parts, axis=1)
    o_ref[0] = h_ref[0] + mod_ref[0][2:3] * _dot(y, w_ref[...])


def _outproj_call(kind, ins, h, mods, w, n_lat_tiles, n_tiles):
    b, t, d = h.shape
    ctx_row = mods.shape[0] - 1
    row = lambda bi, j: (bi, j, 0)
    if kind == "attn":
        (y,) = ins
        specs = [pl.BlockSpec((1, TM, y.shape[2]), row)]
    else:
        hf, hb, og, hn = ins
        specs = [pl.BlockSpec((1, TM, hf.shape[2]), row)] * 3 + [_resident(hn.shape)]
    return pl.pallas_call(
        functools.partial(_outproj_kernel, kind),
        grid=(b, n_tiles),
        in_specs=specs + [pl.BlockSpec((1, TM, d), row), _mod_spec(n_lat_tiles, ctx_row), _resident(w.shape)],
        out_specs=pl.BlockSpec((1, TM, d), row),
        out_shape=jax.ShapeDtypeStruct((b, n_tiles * TM, d), F32),
        compiler_params=_cparams(("parallel", "arbitrary")),
        name="outproj_" + kind,
    )(*ins, h, mods, w)


def _convmlp_kernel(kind, n_lat_tiles, n_all_tiles, hp_ref, h_ref, hn_ref, mod_ref, g_ref, w1_ref, cw_ref, w2_ref, o_ref):
    j = pl.program_id(1)
    m = mod_ref[0]
    sh, sc, gt = (m[0:1], m[1:2], m[2:3]) if kind == "sc" else (m[3:4], m[4:5], m[5:6])
    x = jnp.concatenate([hp_ref[0], h_ref[0], hn_ref[0]], axis=0)
    rows = x.shape[0]
    a = _rms_mod(x, g_ref[...], sh, sc)
    seg_first = (j == 0) | (j == n_lat_tiles)
    seg_last = (j == n_lat_tiles - 1) | (j == n_all_tiles - 1)
    r = lax.broadcasted_iota(jnp.int32, (rows, 1), 0)
    dead = ((r < HALO) & seg_first) | ((r >= rows - HALO) & seg_last)
    a = jnp.where(dead, 0.0, a).astype(BF16)

    def conv3(z, col):
        w = cw_ref[:, col:col + CONV_CHUNK]
        prev = pltpu.roll(z, 1, 0)[HALO:rows - HALO]
        nxt = pltpu.roll(z, rows - 1, 0)[HALO:rows - HALO]
        return prev * w[0:1] + z[HALO:rows - HALO] * w[1:2] + nxt * w[2:3]

    parts = 3 if kind == "sc" else 2
    width = parts * CONV_CHUNK
    acc = None
    for c in range(w2_ref.shape[0] // CONV_CHUNK):
        u = _dot(a, w1_ref[:, c * width:(c + 1) * width])
        if kind == "sc":
            gate_b = u[HALO:rows - HALO, :CONV_CHUNK]
            z = u[:, CONV_CHUNK:2 * CONV_CHUNK] * u[:, 2 * CONV_CHUNK:]
            hid = gate_b * conv3(z, c * CONV_CHUNK)
        else:
            gg = conv3(u[:, :CONV_CHUNK], c * width)
            uu = conv3(u[:, CONV_CHUNK:], c * width + CONV_CHUNK)
            hid = gg * _sigmoid(gg) * uu
        y = _dot(hid.astype(BF16), w2_ref[c * CONV_CHUNK:(c + 1) * CONV_CHUNK, :])
        acc = y if acc is None else acc + y
    o_ref[0] = h_ref[0] + gt * acc


def _convmlp_call(kind, h, mods, gain, w1, cw, w2, n_lat_tiles, n_all_tiles, n_tiles):
    b, t, d = h.shape
    ctx_row = mods.shape[0] - 1
    per = TM // HALO
    last_halo = t // HALO - 1
    row = lambda bi, j: (bi, j, 0)
    return pl.pallas_call(
        functools.partial(_convmlp_kernel, kind, n_lat_tiles, n_all_tiles),
        grid=(b, n_tiles),
        in_specs=[pl.BlockSpec((1, HALO, d), lambda bi, j: (bi, jnp.maximum(j * per - 1, 0), 0)),
                  pl.BlockSpec((1, TM, d), row),
                  pl.BlockSpec((1, HALO, d), lambda bi, j: (bi, jnp.minimum((j + 1) * per, last_halo), 0)),
                  _mod_spec(n_lat_tiles, ctx_row),
                  _resident((1, d)), _resident(w1.shape), _resident(cw.shape), _resident(w2.shape)],
        out_specs=pl.BlockSpec((1, TM, d), row),
        out_shape=jax.ShapeDtypeStruct((b, n_tiles * TM, d), F32),
        compiler_params=_cparams(("parallel", "arbitrary")),
        name="convmlp_" + kind,
    )(h, h, h, mods, gain, w1, cw, w2)


def _chunk_interleave(parts):
    k = parts[0].shape[0]
    st = jnp.stack([p.reshape(k, -1, CONV_CHUNK) for p in parts], axis=2)
    return st.reshape(k, -1)


def _rope_tables(s_lat, n_ctx, hd, reps):
    rows = s_lat // GRID_W
    row = np.repeat(np.arange(rows, dtype=np.float32), GRID_W)
    col = np.tile(np.arange(GRID_W, dtype=np.float32), rows)
    n_freq = hd // 4
    inv = jnp.power(ROPE_THETA, -jnp.arange(n_freq, dtype=F32) / n_freq)
    ang = jnp.concatenate([jnp.asarray(row)[:, None] * inv, jnp.asarray(col)[:, None] * inv], axis=-1)
    cos, sin = jnp.cos(ang), jnp.sin(ang)
    cos = jnp.tile(jnp.concatenate([cos, cos], axis=1), (1, reps))
    sin = jnp.tile(jnp.concatenate([-sin, sin], axis=1), (1, reps))
    cos = jnp.concatenate([cos, jnp.ones((n_ctx, LANES), F32)], axis=0)
    sin = jnp.concatenate([sin, jnp.zeros((n_ctx, LANES), F32)], axis=0)
    return cos, sin


def kernel(x, c, ctx, c_ctx, ada_w, ada_b, norm_mix, norm_ffn, ffn_w_up, ffn_conv, ffn_w_down, win_w_qkv, win_q_norm, win_k_norm, win_sink, win_w_o, sc_w_in, sc_conv, sc_w_out, ax_w_qkv, ax_q_norm, ax_k_norm, ax_w_o, ml_w_in, ml_b_gate, ml_h_norm, ml_w_out):
    b, s_lat, d = x.shape
    n_ctx = ctx.shape[1]
    assert d == D_MODEL and s_lat % FLASH_CK == 0 and n_ctx == TM and s_lat % GRID_W == 0
    n_lat_tiles = s_lat // TM
    n_all_tiles = n_lat_tiles + n_ctx // TM
    depth = ada_w.shape[0]

    h = jnp.concatenate([x, ctx], axis=1)
    pad_rows = -(b + 1) % SUBLANES
    cvec = jnp.concatenate([c, c_ctx[None], jnp.zeros((pad_rows, d), F32)], axis=0)
    mods = _ada_call(cvec, ada_w, ada_b)[:, :b + 1].reshape(depth, b + 1, 6, d)

    for i in range(depth):
        kind, j = i % N_MIXERS, i // N_MIXERS
        last = i == depth - 1
        n_tiles = n_lat_tiles if last else n_all_tiles
        m = mods[i]
        g_mix = norm_mix[i][None]
        if kind == 0:
            gain = jnp.concatenate([jnp.tile(win_q_norm[j], WIN_HEADS) * WIN_HD ** -0.5,
                                    jnp.tile(win_k_norm[j], WIN_KV)])[None]
            cos, sin = _rope_tables(s_lat, n_ctx, WIN_HD, 2)
            qkv = _inproj_call("win", h, m, g_mix, win_w_qkv[j].astype(BF16), (gain, cos, sin), n_lat_tiles)
            sink_b = jnp.broadcast_to(win_sink[j][:, None], (WIN_HEADS, LANES))
            att = _win_call(qkv, sink_b, s_lat, n_ctx)
            h = _outproj_call("attn", (att,), h, m, win_w_o[j].astype(BF16), n_lat_tiles, n_tiles)
        elif kind == 1:
            w1 = _chunk_interleave(jnp.split(sc_w_in[j], 3, axis=1)).astype(BF16)
            h = _convmlp_call("sc", h, m, g_mix, w1, sc_conv[j], sc_w_out[j].astype(BF16),
                              n_lat_tiles, n_all_tiles, n_tiles)
        elif kind == 2:
            gain = jnp.concatenate([jnp.tile(ax_q_norm[j], AX_HEADS) * AX_HD ** -0.5,
                                    jnp.tile(ax_k_norm[j], AX_KV)])[None]
            cos, sin = _rope_tables(s_lat, n_ctx, AX_HD, 1)
            qkv = _inproj_call("ax", h, m, g_mix, ax_w_qkv[j].astype(BF16), (gain, cos, sin), n_lat_tiles)
            att = _flash_call(qkv, s_lat, n_ctx)
            h = _outproj_call("attn", (att,), h, m, ax_w_o[j].astype(BF16), n_lat_tiles, n_tiles)
        else:
            w = jnp.concatenate([ml_w_in[j], jnp.zeros((d, LANES - 4 * ML_HEADS), F32)], axis=1).astype(BF16)
            bg = jnp.concatenate([ml_b_gate[j], jnp.zeros((LANES - 4 * ML_HEADS,), F32)])[None]
            qkv, og, gates = _inproj_call("ml", h, m, g_mix, w, (bg,), n_lat_tiles)
            hf, hb = _mlstm_call(qkv, gates, s_lat)
            hn = jnp.tile(ml_h_norm[j], ML_HEADS)[None]
            h = _outproj_call("ml", (hf, hb, og, hn), h, m, ml_w_out[j].astype(BF16), n_lat_tiles, n_tiles)
        w1 = _chunk_interleave(jnp.split(ffn_w_up[i], 2, axis=1)).astype(BF16)
        cw = _chunk_interleave(jnp.split(ffn_conv[i], 2, axis=1))
        h = _convmlp_call("ffn", h, m, norm_ffn[i][None], w1, cw, ffn_w_down[i].astype(BF16),
                          n_lat_tiles, n_all_tiles, n_tiles)
    return h[:, :s_lat] if h.shape[1] != s_lat else h
```

```python
import functools

import numpy as np
import jax
import jax.numpy as jnp
from jax import lax
from jax.experimental import pallas as pl
from jax.experimental.pallas import tpu as pltpu

D_MODEL = 1024
DEPTH = 4
GRID_W = 64
N_MIXERS = 4
BLOCK = 128
WINDOW = 128
WIN_HEADS = 16
WIN_KV = 4
WIN_HD = 64
AX_HEADS = 8
AX_KV = 2
AX_HD = 128
ML_HEADS = 4
ML_DK = 128
ML_DV = 256
ML_CHUNK = 128
D_FF = 2816
ROPE_THETA = 10000.0
EPS = 1e-6
NEG = -1e30

F32 = jnp.float32
BF16 = jnp.bfloat16

LANES = 128
SUBLANES = 8
TM = 256
HALO = SUBLANES
CONV_CHUNK = 256
FLASH_TQ = 256
FLASH_CK = 512
VMEM_LIMIT = 56 * 1024 * 1024


def _cparams(sem):
    return pltpu.CompilerParams(dimension_semantics=sem, vmem_limit_bytes=VMEM_LIMIT)


def _resident(shape):
    nd = len(shape)
    return pl.BlockSpec(shape, lambda *_: (0,) * nd, pipeline_mode=pl.Buffered(1))


def _dot(a, b):
    return jnp.dot(a, b, preferred_element_type=F32)


def _dot_nt(a, b):
    return lax.dot_general(a, b, (((1,), (1,)), ((), ())), preferred_element_type=F32)


def _dot_tn(a, b):
    return lax.dot_general(a, b, (((0,), (0,)), ((), ())), preferred_element_type=F32)


def _sigmoid(x):
    return 1.0 / (1.0 + jnp.exp(-x))


def _rms_mod(x, g, shift, scale):
    y = x * lax.rsqrt(jnp.mean(x * x, axis=-1, keepdims=True) + EPS) * g
    return y * (1.0 + scale) + shift


def _ada_kernel(c_ref, w_ref, b_ref, o_ref):
    c = c_ref[...]
    o_ref[0] = _dot(c * _sigmoid(c), w_ref[0]) + b_ref[0]


def _ada_call(cvec, ada_w, ada_b):
    depth, d, n = ada_w.shape
    nb = n // 4
    return pl.pallas_call(
        _ada_kernel,
        grid=(depth, n // nb),
        in_specs=[pl.BlockSpec(cvec.shape, lambda l, j: (0, 0)),
                  pl.BlockSpec((1, d, nb), lambda l, j: (l, 0, j)),
                  pl.BlockSpec((1, 1, nb), lambda l, j: (l, 0, j))],
        out_specs=pl.BlockSpec((1, cvec.shape[0], nb), lambda l, j: (l, 0, j)),
        out_shape=jax.ShapeDtypeStruct((depth, cvec.shape[0], n), F32),
        compiler_params=_cparams(("arbitrary", "arbitrary")),
        name="ada",
    )(cvec, ada_w, ada_b.reshape(depth, 1, n))


def _inproj_kernel(kind, h_ref, mod_ref, g_ref, w_ref, *rest):
    m = mod_ref[0]
    a = _rms_mod(h_ref[0], g_ref[...], m[0:1], m[1:2]).astype(BF16)
    y = _dot(a, w_ref[...])
    tm = y.shape[0]
    if kind in ("win", "ax"):
        gain_ref, cos_ref, sin_ref, o_ref = rest
        cos = cos_ref[...]
        sin = sin_ref[...]
        lane = lax.broadcasted_iota(jnp.int32, (tm, LANES), 1)
        n_rot = gain_ref.shape[1] // LANES
        for t in range(n_rot):
            sl = slice(t * LANES, (t + 1) * LANES)
            xt = y[:, sl]
            x2 = xt * xt
            if kind == "win":
                lo = lane < WIN_HD
                s_lo = jnp.sum(jnp.where(lo, x2, 0.0), axis=-1, keepdims=True)
                s_hi = jnp.sum(jnp.where(lo, 0.0, x2), axis=-1, keepdims=True)
                ms = jnp.where(lo, s_lo, s_hi) * (1.0 / WIN_HD)
                xn = xt * lax.rsqrt(ms + EPS) * gain_ref[:, sl]
                first = (lane % WIN_HD) < (WIN_HD // 2)
                sw = jnp.where(first, pltpu.roll(xn, LANES - WIN_HD // 2, 1), pltpu.roll(xn, WIN_HD // 2, 1))
            else:
                ms = jnp.mean(x2, axis=-1, keepdims=True)
                xn = xt * lax.rsqrt(ms + EPS) * gain_ref[:, sl]
                sw = pltpu.roll(xn, AX_HD // 2, 1)
            o_ref[0, :, sl] = (xn * cos + sw * sin).astype(BF16)
        if kind == "win":
            o_ref[0, :, n_rot * LANES:] = y[:, n_rot * LANES:].astype(BF16)
        else:
            for j in range(AX_KV):
                src = (n_rot + j) * LANES
                dst = (n_rot + 2 * j) * LANES
                o_ref[0, :, dst:dst + LANES] = y[:, src:src + LANES].astype(BF16)
                o_ref[0, :, dst + LANES:dst + 2 * LANES] = jnp.ones((tm, LANES), BF16)
    else:
        bg_ref, qkv_ref, og_ref, gate_ref = rest
        nqk = 2 * ML_HEADS * ML_DK
        nv = ML_HEADS * ML_DV
        nq = ML_HEADS * ML_DK
        qkv_ref[0, :, :nq] = (y[:, :nq] * ML_DK ** -0.5).astype(BF16)
        qkv_ref[0, :, nq:nqk + nv] = y[:, nq:nqk + nv].astype(BF16)
        og_ref[0] = y[:, nqk + nv:nqk + 2 * nv]
        g = y[:, nqk + 2 * nv:] + bg_ref[...]
        lane = lax.broadcasted_iota(jnp.int32, g.shape, 1)
        is_forget = (lane % (2 * ML_HEADS)) >= ML_HEADS
        log_sig = jnp.minimum(g, 0.0) - jnp.log1p(jnp.exp(-jnp.abs(g)))
        gate_ref[0] = jnp.where(is_forget, log_sig, g)


def _mod_spec(n_lat_tiles, ctx_row):
    return pl.BlockSpec((1, 6, D_MODEL), lambda b, j: (jnp.where(j < n_lat_tiles, b, ctx_row), 0, 0))


def _inproj_call(kind, h, mods, gain, w, extras, n_lat_tiles):
    b, t, d = h.shape
    n = w.shape[1]
    ctx_row = mods.shape[0] - 1
    in_specs = [pl.BlockSpec((1, TM, d), lambda bi, j: (bi, j, 0)),
                _mod_spec(n_lat_tiles, ctx_row),
                _resident((1, d)),
                _resident(w.shape)]
    if kind in ("win", "ax"):
        hgain, cos, sin = extras
        in_specs += [_resident(hgain.shape),
                     pl.BlockSpec((TM, LANES), lambda bi, j: (j, 0)),
                     pl.BlockSpec((TM, LANES), lambda bi, j: (j, 0))]
        n_out = n if kind == "win" else n + AX_KV * AX_HD
        out_shape = jax.ShapeDtypeStruct((b, t, n_out), BF16)
        out_specs = pl.BlockSpec((1, TM, n_out), lambda bi, j: (bi, j, 0))
        args = (hgain, cos, sin)
    else:
        (bg,) = extras
        nqkv = 2 * ML_HEADS * ML_DK + ML_HEADS * ML_DV
        nv = ML_HEADS * ML_DV
        in_specs += [_resident(bg.shape)]
        out_shape = (jax.ShapeDtypeStruct((b, t, nqkv), BF16),
                     jax.ShapeDtypeStruct((b, t, nv), F32),
                     jax.ShapeDtypeStruct((b, t, LANES), F32))
        out_specs = (pl.BlockSpec((1, TM, nqkv), lambda bi, j: (bi, j, 0)),
                     pl.BlockSpec((1, TM, nv), lambda bi, j: (bi, j, 0)),
                     pl.BlockSpec((1, TM, LANES), lambda bi, j: (bi, j, 0)))
        args = (bg,)
    return pl.pallas_call(
        functools.partial(_inproj_kernel, kind),
        grid=(b, t // TM),
        in_specs=in_specs, out_specs=out_specs, out_shape=out_shape,
        compiler_params=_cparams(("parallel", "arbitrary")),
        name="inproj_" + kind,
    )(h, mods, gain, w, *args)


def _win_heads(q_ref, k, v, bias, sink_ref, o_ref):
    tq = q_ref.shape[1]
    lane = lax.broadcasted_iota(jnp.int32, (tq, LANES), 1)
    lo = lane < WIN_HD
    group = WIN_HEADS // WIN_KV
    if bias is not None:
        bias = jnp.concatenate([bias] * group, axis=0)

    def scores(kv):
        kt, khalf = kv // 2, kv % 2
        qs = []
        for g in range(group):
            head = kv * group + g
            tile, half = head // 2, head % 2
            qt = q_ref[0, :, tile * LANES:(tile + 1) * LANES].astype(F32)
            qh = jnp.where(lo, qt, 0.0) if half == 0 else jnp.where(lo, 0.0, qt)
            if half != khalf:
                qh = pltpu.roll(qh, WIN_HD, 1)
            qs.append(qh.astype(BF16))
        return _dot_nt(jnp.concatenate(qs, axis=0), k[:, kt * LANES:(kt + 1) * LANES])

    s_next = scores(0)
    for kv in range(WIN_KV):
        kt, khalf = kv // 2, kv % 2
        s = s_next
        if kv + 1 < WIN_KV:
            s_next = scores(kv + 1)
        if bias is not None:
            s = s + bias
        sink = jnp.concatenate(
            [jnp.broadcast_to(sink_ref[kv * group + g:kv * group + g + 1, :][:, 0:1], (tq, 1)) for g in range(group)],
            axis=0)
        mx = jnp.maximum(jnp.max(s, axis=-1, keepdims=True), sink)
        e = jnp.exp(s - mx)
        den = jnp.sum(e, axis=-1, keepdims=True) + jnp.exp(sink - mx)
        o = _dot(e.astype(BF16), v[:, kt * LANES:(kt + 1) * LANES]) / den
        for pair in range(group // 2):
            halves = []
            for half in range(2):
                og = o[(2 * pair + half) * tq:(2 * pair + half + 1) * tq]
                halves.append(pltpu.roll(og, WIN_HD, 1) if half != khalf else og)
            tile = (kv * group) // 2 + pair
            o_ref[0, :, tile * LANES:(tile + 1) * LANES] = jnp.where(lo, halves[0], halves[1]).astype(BF16)


def _win_kernel(n_lat, q_ref, kp, kc, kn, kx, vp, vc, vn, vx, sink_ref, o_ref):
    i = pl.program_id(1)
    tq = q_ref.shape[1]
    n_ctx = kx.shape[1]

    @pl.when(i < n_lat)
    def _():
        nk = 3 * tq + n_ctx
        col = lax.broadcasted_iota(jnp.int32, (1, nk), 1)
        t = lax.broadcasted_iota(jnp.int32, (tq, nk), 1) - lax.broadcasted_iota(jnp.int32, (tq, nk), 0)
        big = 4 * nk
        t_min = jnp.where(col < tq, jnp.where(i > 0, 0, big), -big)
        t_max = jnp.where((col >= 2 * tq) & (col < 3 * tq), jnp.where(i < n_lat - 1, 2 * tq, -big), big)
        bias = jnp.where(t < t_min, NEG, 0.0) + jnp.where(t > t_max, NEG, 0.0)
        k = jnp.concatenate([kp[0], kc[0], kn[0], kx[0]], axis=0)
        v = jnp.concatenate([vp[0], vc[0], vn[0], vx[0]], axis=0)
        _win_heads(q_ref, k, v, bias, sink_ref, o_ref)

    @pl.when(i >= n_lat)
    def _():
        _win_heads(q_ref, kx[0], vx[0], None, sink_ref, o_ref)


def _win_call(qkv, sink_b, s_lat, n_ctx):
    b, t, _ = qkv.shape
    n_lat = s_lat // BLOCK
    nq = t // BLOCK
    dq = WIN_HEADS * WIN_HD
    kw = WIN_KV * WIN_HD
    kcol = dq // kw
    vcol = kcol + 1
    last = n_lat - 1
    xrow = s_lat // n_ctx

    def kv_specs(col):
        return [pl.BlockSpec((1, BLOCK, kw), lambda bi, i: (bi, jnp.clip(i - 1, 0, last), col)),
                pl.BlockSpec((1, BLOCK, kw), lambda bi, i: (bi, jnp.minimum(i, last), col)),
                pl.BlockSpec((1, BLOCK, kw), lambda bi, i: (bi, jnp.minimum(i + 1, last), col)),
                pl.BlockSpec((1, n_ctx, kw), lambda bi, i: (bi, xrow, col))]

    return pl.pallas_call(
        functools.partial(_win_kernel, n_lat),
        grid=(b, nq),
        in_specs=[pl.BlockSpec((1, BLOCK, dq), lambda bi, i: (bi, i, 0))] + kv_specs(kcol) + kv_specs(vcol)
                 + [_resident(sink_b.shape)],
        out_specs=pl.BlockSpec((1, BLOCK, dq), lambda bi, i: (bi, i, 0)),
        out_shape=jax.ShapeDtypeStruct((b, t, dq), BF16),
        compiler_params=_cparams(("parallel", "arbitrary")),
        name="win_attn",
    )(qkv, *([qkv] * 8), sink_b)


def _flash_kernel(n_lat_q, n_full, s_lat, n_ctx, q_ref, k_ref, v_ref, o_ref, m_sc, acc_sc, s_sc):
    qi = pl.program_id(2)
    tq = q_ref.shape[1]
    group = AX_HEADS // AX_KV
    q = jnp.concatenate([q_ref[0, :, g * AX_HD:(g + 1) * AX_HD] for g in range(group)], axis=0)

    m_sc[...] = jnp.full(m_sc.shape, NEG, F32)
    acc_sc[...] = jnp.zeros(acc_sc.shape, F32)

    def scores(slot, start, size):
        s_sc[slot, :, :size] = _dot_nt(q, k_ref[0, pl.ds(start, size), :])

    def update(slot, start, size):
        s = s_sc[slot, :, :size]
        m_prev = m_sc[...]
        m_next = jnp.maximum(m_prev, jnp.max(s, axis=-1, keepdims=True))
        p = jnp.exp2(s - jnp.concatenate([m_next] * (size // LANES), axis=1))
        alpha = jnp.exp2(m_prev - m_next)
        pv = _dot(p.astype(BF16), v_ref[0, pl.ds(start, size), :])
        acc_sc[...] = acc_sc[...] * jnp.concatenate([alpha, alpha], axis=1) + pv
        m_sc[...] = m_next

    @pl.when(qi < n_lat_q)
    def _():
        def at(c):
            return pl.multiple_of(c * FLASH_CK, FLASH_CK)

        scores(0, 0, FLASH_CK)

        def body(i, carry):
            c = 2 * i
            scores(1, at(c + 1), FLASH_CK)
            update(0, at(c), FLASH_CK)
            scores(0, at(c + 2), FLASH_CK)
            update(1, at(c + 1), FLASH_CK)
            return carry

        n_pairs = (n_full - 1) // 2
        lax.fori_loop(0, n_pairs, body, 0)
        tail = [(c * FLASH_CK, FLASH_CK) for c in range(2 * n_pairs, n_full)] + [(s_lat, n_ctx)]
        for idx, (start, size) in enumerate(tail):
            if idx + 1 < len(tail):
                scores((idx + 1) % 2, *tail[idx + 1])
            update(idx % 2, start, size)

    @pl.when(qi >= n_lat_q)
    def _():
        scores(0, s_lat, n_ctx)
        update(0, s_lat, n_ctx)

    acc = acc_sc[...]
    o = acc[:, :AX_HD] / acc[:, AX_HD:]
    for g in range(group):
        o_ref[0, :, g * AX_HD:(g + 1) * AX_HD] = o[g * tq:(g + 1) * tq].astype(BF16)


def _flash_call(qkv, s_lat, n_ctx):
    b, t, _ = qkv.shape
    tq = FLASH_TQ
    group = AX_HEADS // AX_KV
    gw = group * AX_HD
    dq = AX_HEADS * AX_HD
    kcol = dq // AX_HD
    vcol = (dq + AX_KV * AX_HD) // (2 * AX_HD)
    m_rows = group * tq
    return pl.pallas_call(
        functools.partial(_flash_kernel, s_lat // tq, s_lat // FLASH_CK, s_lat, n_ctx),
        grid=(b, AX_KV, t // tq),
        in_specs=[pl.BlockSpec((1, tq, gw), lambda bi, j, i: (bi, i, j)),
                  pl.BlockSpec((1, t, AX_HD), lambda bi, j, i: (bi, 0, kcol + j)),
                  pl.BlockSpec((1, t, 2 * AX_HD), lambda bi, j, i: (bi, 0, vcol + j))],
        out_specs=pl.BlockSpec((1, tq, gw), lambda bi, j, i: (bi, i, j)),
        out_shape=jax.ShapeDtypeStruct((b, t, dq), BF16),
        scratch_shapes=[pltpu.VMEM((m_rows, LANES), F32), pltpu.VMEM((m_rows, 2 * AX_HD), F32),
                        pltpu.VMEM((2, m_rows, FLASH_CK), F32)],
        compiler_params=_cparams(("parallel", "parallel", "arbitrary")),
        name="flash_attn",
    )(qkv, qkv, qkv)


def _mlstm_kernel(qf_ref, gf_ref, qb_ref, gb_ref, hf_ref, hb_ref, c_sc, n_sc, m_sc):
    step = pl.program_id(1)
    L = ML_CHUNK

    @pl.when(step == 0)
    def _():
        c_sc[...] = jnp.zeros(c_sc.shape, F32)
        n_sc[...] = jnp.zeros(n_sc.shape, F32)
        m_sc[...] = jnp.full(m_sc.shape, NEG, F32)

    r = lax.broadcasted_iota(jnp.int32, (L, L), 0)
    c = lax.broadcasted_iota(jnp.int32, (L, L), 1)
    lo = c <= r
    up = c >= r
    lo_f = lo.astype(F32)
    up_f = up.astype(F32)
    hi = lax.Precision.HIGHEST
    nqk = ML_HEADS * ML_DK

    for d, (x_ref, g_ref, out_ref) in enumerate(((qf_ref, gf_ref, hf_ref), (qb_ref, gb_ref, hb_ref))):
        gates = g_ref[0]
        gates_t = gates.T
        tri_col, tri_row, mask = (lo_f, up_f, lo) if d == 0 else (up_f, lo_f, up)
        cum_col = jnp.dot(tri_col, gates, precision=hi, preferred_element_type=F32)
        cum_row = jnp.dot(gates_t, tri_row, precision=hi, preferred_element_type=F32)
        total = jnp.sum(gates, axis=0, keepdims=True)
        for h in range(ML_HEADS):
            ci = 2 * ML_HEADS * d + h
            cf = ci + ML_HEADS
            q = x_ref[0, :, h * ML_DK:(h + 1) * ML_DK]
            k = x_ref[0, :, nqk + h * ML_DK:nqk + (h + 1) * ML_DK]
            v = x_ref[0, :, 2 * nqk + h * ML_DV:2 * nqk + (h + 1) * ML_DV]
            c_prev = c_sc[d, h]
            n_prev = n_sc[d, h][0:1, :]
            m_prev = m_sc[d, h][0:1, 0:1]
            f_col = cum_col[:, cf:cf + 1]
            f_row = cum_row[cf:cf + 1, :]
            i_col = gates[:, ci:ci + 1]
            i_row = gates_t[ci:ci + 1, :]
            f_tot = total[:, cf:cf + 1]

            dmat = jnp.where(mask, f_col - f_row + i_row, NEG)
            inter = f_col + m_prev
            m_t = jnp.maximum(inter, jnp.max(dmat, axis=-1, keepdims=True))
            w_intra = jnp.exp(dmat - m_t)
            w_inter = jnp.exp(inter - m_t)
            qk = _dot_nt(q, k) * w_intra
            qn = jnp.sum(q.astype(F32) * n_prev, axis=-1, keepdims=True)
            num = _dot(qk.astype(BF16), v) + w_inter * _dot_nt(q, c_prev.astype(BF16))
            den = jnp.sum(qk, axis=-1, keepdims=True) + w_inter * qn
            out_ref[0, :, h * ML_DV:(h + 1) * ML_DV] = num / jnp.maximum(jnp.abs(den), jnp.exp(-m_t))

            a_col = f_tot - f_col + i_col
            m_bar = jnp.max(a_col, axis=0, keepdims=True)
            w_col = jnp.exp(a_col - m_bar)
            c_bar = _dot_tn((v.astype(F32) * w_col).astype(BF16), k)
            n_bar = jnp.sum(k.astype(F32) * w_col, axis=0, keepdims=True)
            m_new = jnp.maximum(f_tot + m_prev, m_bar)
            decay = jnp.exp(f_tot + m_prev - m_new)
            inj = jnp.exp(m_bar - m_new)
            c_sc[d, h] = decay * c_prev + inj * c_bar
            n_sc[d, h] = jnp.broadcast_to(decay * n_prev + inj * n_bar, (SUBLANES, ML_DK))
            m_sc[d, h] = jnp.broadcast_to(m_new, (SUBLANES, LANES))


def _mlstm_call(qkv, gates, s_lat):
    b, t, nx = qkv.shape
    nc = t // ML_CHUNK
    nlc = s_lat // ML_CHUNK
    nv = ML_HEADS * ML_DV

    def fwd(bi, s):
        return (bi, (s + nlc) % nc, 0)

    def bwd(bi, s):
        return (bi, nc - 1 - s, 0)

    return pl.pallas_call(
        _mlstm_kernel,
        grid=(b, nc),
        in_specs=[pl.BlockSpec((1, ML_CHUNK, nx), fwd), pl.BlockSpec((1, ML_CHUNK, LANES), fwd),
                  pl.BlockSpec((1, ML_CHUNK, nx), bwd), pl.BlockSpec((1, ML_CHUNK, LANES), bwd)],
        out_specs=(pl.BlockSpec((1, ML_CHUNK, nv), fwd), pl.BlockSpec((1, ML_CHUNK, nv), bwd)),
        out_shape=(jax.ShapeDtypeStruct((b, t, nv), F32), jax.ShapeDtypeStruct((b, t, nv), F32)),
        scratch_shapes=[pltpu.VMEM((2, ML_HEADS, ML_DV, ML_DK), F32),
                        pltpu.VMEM((2, ML_HEADS, SUBLANES, ML_DK), F32),
                        pltpu.VMEM((2, ML_HEADS, SUBLANES, LANES), F32)],
        compiler_params=_cparams(("parallel", "arbitrary")),
        name="mlstm",
    )(qkv, gates, qkv, gates)


def _outproj_kernel(kind, *refs):
    if kind == "attn":
        y_ref, h_ref, mod_ref, w_ref, o_ref = refs
        y = y_ref[0]
    else:
        hf_ref, hb_ref, og_ref, hn_ref, h_ref, mod_ref, w_ref, o_ref = refs
        hs = hf_ref[0] + hb_ref[0]
        parts = []
        for h in range(ML_HEADS):
            sl = slice(h * ML_DV, (h + 1) * ML_DV)
            x = hs[:, sl]
            xn = x * lax.rsqrt(jnp.mean(x * x, axis=-1, keepdims=True) + EPS) * hn_ref[:, sl]
            parts.append((_sigmoid(og_ref[0, :, sl]) * xn).astype(BF16))
        y = jnp.concatenate(parts, axis=1)
    o_ref[0] = h_ref[0] + mod_ref[0][2:3] * _dot(y, w_ref[...])


def _outproj_call(kind, ins, h, mods, w, n_lat_tiles, n_tiles):
    b, t, d = h.shape
    ctx_row = mods.shape[0] - 1
    row = lambda bi, j: (bi, j, 0)
    if kind == "attn":
        (y,) = ins
        specs = [pl.BlockSpec((1, TM, y.shape[2]), row)]
    else:
        hf, hb, og, hn = ins
        specs = [pl.BlockSpec((1, TM, hf.shape[2]), row)] * 3 + [_resident(hn.shape)]
    return pl.pallas_call(
        functools.partial(_outproj_kernel, kind),
        grid=(b, n_tiles),
        in_specs=specs + [pl.BlockSpec((1, TM, d), row), _mod_spec(n_lat_tiles, ctx_row), _resident(w.shape)],
        out_specs=pl.BlockSpec((1, TM, d), row),
        out_shape=jax.ShapeDtypeStruct((b, n_tiles * TM, d), F32),
        compiler_params=_cparams(("parallel", "arbitrary")),
        name="outproj_" + kind,
    )(*ins, h, mods, w)


def _convmlp_kernel(kind, n_lat_tiles, n_all_tiles, hp_ref, h_ref, hn_ref, mod_ref, g_ref, w1_ref, cw_ref, w2_ref, o_ref):
    j = pl.program_id(1)
    m = mod_ref[0]
    sh, sc, gt = (m[0:1], m[1:2], m[2:3]) if kind == "sc" else (m[3:4], m[4:5], m[5:6])
    x = jnp.concatenate([hp_ref[0], h_ref[0], hn_ref[0]], axis=0)
    rows = x.shape[0]
    a = _rms_mod(x, g_ref[...], sh, sc)
    seg_first = (j == 0) | (j == n_lat_tiles)
    seg_last = (j == n_lat_tiles - 1) | (j == n_all_tiles - 1)
    r = lax.broadcasted_iota(jnp.int32, (rows, 1), 0)
    dead = ((r < HALO) & seg_first) | ((r >= rows - HALO) & seg_last)
    a = jnp.where(dead, 0.0, a).astype(BF16)

    def conv3(z, col):
        w = cw_ref[:, col:col + CONV_CHUNK]
        prev = pltpu.roll(z, 1, 0)[HALO:rows - HALO]
        nxt = pltpu.roll(z, rows - 1, 0)[HALO:rows - HALO]
        return prev * w[0:1] + z[HALO:rows - HALO] * w[1:2] + nxt * w[2:3]

    hidden = w2_ref.shape[0]
    parts = w1_ref.shape[1] // hidden

    def up(c):
        return [_dot(a, w1_ref[:, p * hidden + c * CONV_CHUNK:p * hidden + (c + 1) * CONV_CHUNK]) for p in range(parts)]

    n_chunks = hidden // CONV_CHUNK
    acc = None
    u_next = up(0)
    for c in range(n_chunks):
        u = u_next
        if c + 1 < n_chunks:
            u_next = up(c + 1)
        if kind == "sc":
            gate_b = u[0][HALO:rows - HALO]
            hid = gate_b * conv3(u[1] * u[2], c * CONV_CHUNK)
        else:
            gg = conv3(u[0], c * CONV_CHUNK)
            uu = conv3(u[1], hidden + c * CONV_CHUNK)
            hid = gg * _sigmoid(gg) * uu
        y = _dot(hid.astype(BF16), w2_ref[c * CONV_CHUNK:(c + 1) * CONV_CHUNK, :])
        acc = y if acc is None else acc + y
    o_ref[0] = h_ref[0] + gt * acc


def _convmlp_call(kind, h, mods, gain, w1, cw, w2, n_lat_tiles, n_all_tiles, n_tiles):
    b, t, d = h.shape
    ctx_row = mods.shape[0] - 1
    per = TM // HALO
    last_halo = t // HALO - 1
    row = lambda bi, j: (bi, j, 0)
    return pl.pallas_call(
        functools.partial(_convmlp_kernel, kind, n_lat_tiles, n_all_tiles),
        grid=(b, n_tiles),
        in_specs=[pl.BlockSpec((1, HALO, d), lambda bi, j: (bi, jnp.maximum(j * per - 1, 0), 0)),
                  pl.BlockSpec((1, TM, d), row),
                  pl.BlockSpec((1, HALO, d), lambda bi, j: (bi, jnp.minimum((j + 1) * per, last_halo), 0)),
                  _mod_spec(n_lat_tiles, ctx_row),
                  _resident((1, d)), _resident(w1.shape), _resident(cw.shape), _resident(w2.shape)],
        out_specs=pl.BlockSpec((1, TM, d), row),
        out_shape=jax.ShapeDtypeStruct((b, n_tiles * TM, d), F32),
        compiler_params=_cparams(("parallel", "arbitrary")),
        name="convmlp_" + kind,
    )(h, h, h, mods, gain, w1, cw, w2)


def _rope_tables(s_lat, n_ctx, hd, reps):
    rows = s_lat // GRID_W
    row = np.repeat(np.arange(rows, dtype=np.float32), GRID_W)
    col = np.tile(np.arange(GRID_W, dtype=np.float32), rows)
    n_freq = hd // 4
    inv = jnp.power(ROPE_THETA, -jnp.arange(n_freq, dtype=F32) / n_freq)
    ang = jnp.concatenate([jnp.asarray(row)[:, None] * inv, jnp.asarray(col)[:, None] * inv], axis=-1)
    cos, sin = jnp.cos(ang), jnp.sin(ang)
    cos = jnp.tile(jnp.concatenate([cos, cos], axis=1), (1, reps))
    sin = jnp.tile(jnp.concatenate([-sin, sin], axis=1), (1, reps))
    cos = jnp.concatenate([cos, jnp.ones((n_ctx, LANES), F32)], axis=0)
    sin = jnp.concatenate([sin, jnp.zeros((n_ctx, LANES), F32)], axis=0)
    return cos, sin


def kernel(x, c, ctx, c_ctx, ada_w, ada_b, norm_mix, norm_ffn, ffn_w_up, ffn_conv, ffn_w_down, win_w_qkv, win_q_norm, win_k_norm, win_sink, win_w_o, sc_w_in, sc_conv, sc_w_out, ax_w_qkv, ax_q_norm, ax_k_norm, ax_w_o, ml_w_in, ml_b_gate, ml_h_norm, ml_w_out):
    b, s_lat, d = x.shape
    n_ctx = ctx.shape[1]
    assert d == D_MODEL and s_lat % FLASH_CK == 0 and n_ctx == TM and s_lat % GRID_W == 0
    n_lat_tiles = s_lat // TM
    n_all_tiles = n_lat_tiles + n_ctx // TM
    depth = ada_w.shape[0]

    h = jnp.concatenate([x, ctx], axis=1)
    pad_rows = -(b + 1) % SUBLANES
    cvec = jnp.concatenate([c, c_ctx[None], jnp.zeros((pad_rows, d), F32)], axis=0)
    mods = _ada_call(cvec, ada_w, ada_b)[:, :b + 1].reshape(depth, b + 1, 6, d)

    for i in range(depth):
        kind, j = i % N_MIXERS, i // N_MIXERS
        last = i == depth - 1
        n_tiles = n_lat_tiles if last else n_all_tiles
        m = mods[i]
        g_mix = norm_mix[i][None]
        if kind == 0:
            gain = jnp.concatenate([jnp.tile(win_q_norm[j], WIN_HEADS) * WIN_HD ** -0.5,
                                    jnp.tile(win_k_norm[j], WIN_KV)])[None]
            cos, sin = _rope_tables(s_lat, n_ctx, WIN_HD, 2)
            qkv = _inproj_call("win", h, m, g_mix, win_w_qkv[j].astype(BF16), (gain, cos, sin), n_lat_tiles)
            sink_b = jnp.broadcast_to(win_sink[j][:, None], (WIN_HEADS, LANES))
            att = _win_call(qkv, sink_b, s_lat, n_ctx)
            h = _outproj_call("attn", (att,), h, m, win_w_o[j].astype(BF16), n_lat_tiles, n_tiles)
        elif kind == 1:
            h = _convmlp_call("sc", h, m, g_mix, sc_w_in[j].astype(BF16), sc_conv[j], sc_w_out[j].astype(BF16),
                              n_lat_tiles, n_all_tiles, n_tiles)
        elif kind == 2:
            gain = jnp.concatenate([jnp.tile(ax_q_norm[j], AX_HEADS) * (AX_HD ** -0.5 * np.log2(np.e)),
                                    jnp.tile(ax_k_norm[j], AX_KV)])[None]
            cos, sin = _rope_tables(s_lat, n_ctx, AX_HD, 1)
            qkv = _inproj_call("ax", h, m, g_mix, ax_w_qkv[j].astype(BF16), (gain, cos, sin), n_lat_tiles)
            att = _flash_call(qkv, s_lat, n_ctx)
            h = _outproj_call("attn", (att,), h, m, ax_w_o[j].astype(BF16), n_lat_tiles, n_tiles)
        else:
            w = jnp.concatenate([ml_w_in[j], jnp.zeros((d, LANES - 4 * ML_HEADS), F32)], axis=1).astype(BF16)
            bg = jnp.concatenate([ml_b_gate[j], jnp.zeros((LANES - 4 * ML_HEADS,), F32)])[None]
            qkv, og, gates = _inproj_call("ml", h, m, g_mix, w, (bg,), n_lat_tiles)
            hf, hb = _mlstm_call(qkv, gates, s_lat)
            hn = jnp.tile(ml_h_norm[j], ML_HEADS)[None]
            h = _outproj_call("ml", (hf, hb, og, hn), h, m, ml_w_out[j].astype(BF16), n_lat_tiles, n_tiles)
        h = _convmlp_call("ffn", h, m, norm_ffn[i][None], ffn_w_up[i].astype(BF16), ffn_conv[i], ffn_w_down[i].astype(BF16),
                          n_lat_tiles, n_all_tiles, n_tiles)
    return h[:, :s_lat] if h.shape[1] != s_lat else h
```

```python
import functools

import numpy as np
import jax
import jax.numpy as jnp
from jax import lax
from jax.experimental import pallas as pl
from jax.experimental.pallas import tpu as pltpu

D_MODEL = 1024
DEPTH = 4
GRID_W = 64
N_MIXERS = 4
BLOCK = 128
WINDOW = 128
WIN_HEADS = 16
WIN_KV = 4
WIN_HD = 64
AX_HEADS = 8
AX_KV = 2
AX_HD = 128
ML_HEADS = 4
ML_DK = 128
ML_DV = 256
ML_CHUNK = 128
D_FF = 2816
ROPE_THETA = 10000.0
EPS = 1e-6
NEG = -1e30

F32 = jnp.float32
BF16 = jnp.bfloat16

LANES = 128
SUBLANES = 8
TM = 256
HALO = SUBLANES
CONV_CHUNK = 256
FLASH_TQ = 256
FLASH_CK = 512
FLASH_UNROLL = 4
BF16_ROWS = 16
VT_ROWS = AX_HD + BF16_ROWS
VMEM_LIMIT = 56 * 1024 * 1024


def _cparams(sem):
    return pltpu.CompilerParams(dimension_semantics=sem, vmem_limit_bytes=VMEM_LIMIT)


def _resident(shape):
    nd = len(shape)
    return pl.BlockSpec(shape, lambda *_: (0,) * nd, pipeline_mode=pl.Buffered(1))


def _dot(a, b):
    return jnp.dot(a, b, preferred_element_type=F32)


def _dot_nt(a, b):
    return lax.dot_general(a, b, (((1,), (1,)), ((), ())), preferred_element_type=F32)


def _dot_tn(a, b):
    return lax.dot_general(a, b, (((0,), (0,)), ((), ())), preferred_element_type=F32)


def _sigmoid(x):
    return 1.0 / (1.0 + jnp.exp(-x))


def _rms_mod(x, g, shift, scale):
    y = x * lax.rsqrt(jnp.mean(x * x, axis=-1, keepdims=True) + EPS) * g
    return y * (1.0 + scale) + shift


def _ada_kernel(c_ref, w_ref, b_ref, o_ref):
    c = c_ref[...]
    o_ref[0] = _dot(c * _sigmoid(c), w_ref[0]) + b_ref[0]


def _ada_call(cvec, ada_w, ada_b):
    depth, d, n = ada_w.shape
    nb = n // 4
    return pl.pallas_call(
        _ada_kernel,
        grid=(depth, n // nb),
        in_specs=[pl.BlockSpec(cvec.shape, lambda l, j: (0, 0)),
                  pl.BlockSpec((1, d, nb), lambda l, j: (l, 0, j)),
                  pl.BlockSpec((1, 1, nb), lambda l, j: (l, 0, j))],
        out_specs=pl.BlockSpec((1, cvec.shape[0], nb), lambda l, j: (l, 0, j)),
        out_shape=jax.ShapeDtypeStruct((depth, cvec.shape[0], n), F32),
        compiler_params=_cparams(("arbitrary", "arbitrary")),
        name="ada",
    )(cvec, ada_w, ada_b.reshape(depth, 1, n))


def _inproj_kernel(kind, h_ref, mod_ref, g_ref, w_ref, *rest):
    m = mod_ref[0]
    a = _rms_mod(h_ref[0], g_ref[...], m[0:1], m[1:2]).astype(BF16)
    y = _dot(a, w_ref[...])
    tm = y.shape[0]
    if kind in ("win", "ax"):
        if kind == "win":
            gain_ref, cos_ref, sin_ref, o_ref = rest
        else:
            gain_ref, cos_ref, sin_ref, o_ref, vt_ref = rest
        cos = cos_ref[...]
        sin = sin_ref[...]
        lane = lax.broadcasted_iota(jnp.int32, (tm, LANES), 1)
        n_rot = gain_ref.shape[1] // LANES
        for t in range(n_rot):
            sl = slice(t * LANES, (t + 1) * LANES)
            xt = y[:, sl]
            x2 = xt * xt
            if kind == "win":
                lo = lane < WIN_HD
                s_lo = jnp.sum(jnp.where(lo, x2, 0.0), axis=-1, keepdims=True)
                s_hi = jnp.sum(jnp.where(lo, 0.0, x2), axis=-1, keepdims=True)
                ms = jnp.where(lo, s_lo, s_hi) * (1.0 / WIN_HD)
                xn = xt * lax.rsqrt(ms + EPS) * gain_ref[:, sl]
                first = (lane % WIN_HD) < (WIN_HD // 2)
                sw = jnp.where(first, pltpu.roll(xn, LANES - WIN_HD // 2, 1), pltpu.roll(xn, WIN_HD // 2, 1))
            else:
                ms = jnp.mean(x2, axis=-1, keepdims=True)
                xn = xt * lax.rsqrt(ms + EPS) * gain_ref[:, sl]
                sw = pltpu.roll(xn, AX_HD // 2, 1)
            o_ref[0, :, sl] = (xn * cos + sw * sin).astype(BF16)
        if kind == "win":
            o_ref[0, :, n_rot * LANES:] = y[:, n_rot * LANES:].astype(BF16)
        else:
            for j in range(AX_KV):
                src = (n_rot + j) * LANES
                vt_ref[0, j, 0, :AX_HD, :] = y[:, src:src + LANES].T.astype(BF16)
                vt_ref[0, j, 0, AX_HD:, :] = jnp.ones((VT_ROWS - AX_HD, tm), BF16)
    else:
        bg_ref, qkv_ref, og_ref, gate_ref = rest
        nqk = 2 * ML_HEADS * ML_DK
        nv = ML_HEADS * ML_DV
        nq = ML_HEADS * ML_DK
        qkv_ref[0, :, :nq] = (y[:, :nq] * ML_DK ** -0.5).astype(BF16)
        qkv_ref[0, :, nq:nqk + nv] = y[:, nq:nqk + nv].astype(BF16)
        og_ref[0] = y[:, nqk + nv:nqk + 2 * nv]
        g = y[:, nqk + 2 * nv:] + bg_ref[...]
        lane = lax.broadcasted_iota(jnp.int32, g.shape, 1)
        is_forget = (lane % (2 * ML_HEADS)) >= ML_HEADS
        log_sig = jnp.minimum(g, 0.0) - jnp.log1p(jnp.exp(-jnp.abs(g)))
        gate_ref[0] = jnp.where(is_forget, log_sig, g)


def _mod_spec(n_lat_tiles, ctx_row):
    return pl.BlockSpec((1, 6, D_MODEL), lambda b, j: (jnp.where(j < n_lat_tiles, b, ctx_row), 0, 0))


def _inproj_call(kind, h, mods, gain, w, extras, n_lat_tiles):
    b, t, d = h.shape
    n = w.shape[1]
    ctx_row = mods.shape[0] - 1
    in_specs = [pl.BlockSpec((1, TM, d), lambda bi, j: (bi, j, 0)),
                _mod_spec(n_lat_tiles, ctx_row),
                _resident((1, d)),
                _resident(w.shape)]
    if kind in ("win", "ax"):
        hgain, cos, sin = extras
        in_specs += [_resident(hgain.shape),
                     pl.BlockSpec((TM, LANES), lambda bi, j: (j, 0)),
                     pl.BlockSpec((TM, LANES), lambda bi, j: (j, 0))]
        if kind == "win":
            out_shape = jax.ShapeDtypeStruct((b, t, n), BF16)
            out_specs = pl.BlockSpec((1, TM, n), lambda bi, j: (bi, j, 0))
        else:
            n_qk = n - AX_KV * AX_HD
            out_shape = (jax.ShapeDtypeStruct((b, t, n_qk), BF16),
                         jax.ShapeDtypeStruct((b, AX_KV, t // TM, VT_ROWS, TM), BF16))
            out_specs = (pl.BlockSpec((1, TM, n_qk), lambda bi, j: (bi, j, 0)),
                         pl.BlockSpec((1, AX_KV, 1, VT_ROWS, TM), lambda bi, j: (bi, 0, j, 0, 0)))
        args = (hgain, cos, sin)
    else:
        (bg,) = extras
        nqkv = 2 * ML_HEADS * ML_DK + ML_HEADS * ML_DV
        nv = ML_HEADS * ML_DV
        in_specs += [_resident(bg.shape)]
        out_shape = (jax.ShapeDtypeStruct((b, t, nqkv), BF16),
                     jax.ShapeDtypeStruct((b, t, nv), F32),
                     jax.ShapeDtypeStruct((b, t, LANES), F32))
        out_specs = (pl.BlockSpec((1, TM, nqkv), lambda bi, j: (bi, j, 0)),
                     pl.BlockSpec((1, TM, nv), lambda bi, j: (bi, j, 0)),
                     pl.BlockSpec((1, TM, LANES), lambda bi, j: (bi, j, 0)))
        args = (bg,)
    return pl.pallas_call(
        functools.partial(_inproj_kernel, kind),
        grid=(b, t // TM),
        in_specs=in_specs, out_specs=out_specs, out_shape=out_shape,
        compiler_params=_cparams(("parallel", "arbitrary")),
        name="inproj_" + kind,
    )(h, mods, gain, w, *args)


def _win_heads(q_ref, k, v, bias, sink_ref, o_ref):
    tq = q_ref.shape[1]
    lane = lax.broadcasted_iota(jnp.int32, (tq, LANES), 1)
    lo = lane < WIN_HD
    group = WIN_HEADS // WIN_KV
    if bias is not None:
        bias = jnp.concatenate([bias] * group, axis=0)

    def scores(kv):
        kt, khalf = kv // 2, kv % 2
        qs = []
        for g in range(group):
            head = kv * group + g
            tile, half = head // 2, head % 2
            qt = q_ref[0, :, tile * LANES:(tile + 1) * LANES].astype(F32)
            qh = jnp.where(lo, qt, 0.0) if half == 0 else jnp.where(lo, 0.0, qt)
            if half != khalf:
                qh = pltpu.roll(qh, WIN_HD, 1)
            qs.append(qh.astype(BF16))
        return _dot_nt(jnp.concatenate(qs, axis=0), k[:, kt * LANES:(kt + 1) * LANES])

    s_next = scores(0)
    for kv in range(WIN_KV):
        kt, khalf = kv // 2, kv % 2
        s = s_next
        if kv + 1 < WIN_KV:
            s_next = scores(kv + 1)
        if bias is not None:
            s = s + bias
        sink = jnp.concatenate(
            [jnp.broadcast_to(sink_ref[kv * group + g:kv * group + g + 1, :][:, 0:1], (tq, 1)) for g in range(group)],
            axis=0)
        mx = jnp.maximum(jnp.max(s, axis=-1, keepdims=True), sink)
        e = jnp.exp(s - mx)
        den = jnp.sum(e, axis=-1, keepdims=True) + jnp.exp(sink - mx)
        o = _dot(e.astype(BF16), v[:, kt * LANES:(kt + 1) * LANES]) / den
        for pair in range(group // 2):
            halves = []
            for half in range(2):
                og = o[(2 * pair + half) * tq:(2 * pair + half + 1) * tq]
                halves.append(pltpu.roll(og, WIN_HD, 1) if half != khalf else og)
            tile = (kv * group) // 2 + pair
            o_ref[0, :, tile * LANES:(tile + 1) * LANES] = jnp.where(lo, halves[0], halves[1]).astype(BF16)


def _win_kernel(n_lat, q_ref, kp, kc, kn, kx, vp, vc, vn, vx, sink_ref, o_ref):
    i = pl.program_id(1)
    tq = q_ref.shape[1]
    n_ctx = kx.shape[1]

    @pl.when(i < n_lat)
    def _():
        nk = 3 * tq + n_ctx
        col = lax.broadcasted_iota(jnp.int32, (1, nk), 1)
        t = lax.broadcasted_iota(jnp.int32, (tq, nk), 1) - lax.broadcasted_iota(jnp.int32, (tq, nk), 0)
        big = 4 * nk
        t_min = jnp.where(col < tq, jnp.where(i > 0, 0, big), -big)
        t_max = jnp.where((col >= 2 * tq) & (col < 3 * tq), jnp.where(i < n_lat - 1, 2 * tq, -big), big)
        bias = jnp.where(t < t_min, NEG, 0.0) + jnp.where(t > t_max, NEG, 0.0)
        k = jnp.concatenate([kp[0], kc[0], kn[0], kx[0]], axis=0)
        v = jnp.concatenate([vp[0], vc[0], vn[0], vx[0]], axis=0)
        _win_heads(q_ref, k, v, bias, sink_ref, o_ref)

    @pl.when(i >= n_lat)
    def _():
        _win_heads(q_ref, kx[0], vx[0], None, sink_ref, o_ref)


def _win_call(qkv, sink_b, s_lat, n_ctx):
    b, t, _ = qkv.shape
    n_lat = s_lat // BLOCK
    nq = t // BLOCK
    dq = WIN_HEADS * WIN_HD
    kw = WIN_KV * WIN_HD
    kcol = dq // kw
    vcol = kcol + 1
    last = n_lat - 1
    xrow = s_lat // n_ctx

    def kv_specs(col):
        return [pl.BlockSpec((1, BLOCK, kw), lambda bi, i: (bi, jnp.clip(i - 1, 0, last), col)),
                pl.BlockSpec((1, BLOCK, kw), lambda bi, i: (bi, jnp.minimum(i, last), col)),
                pl.BlockSpec((1, BLOCK, kw), lambda bi, i: (bi, jnp.minimum(i + 1, last), col)),
                pl.BlockSpec((1, n_ctx, kw), lambda bi, i: (bi, xrow, col))]

    return pl.pallas_call(
        functools.partial(_win_kernel, n_lat),
        grid=(b, nq),
        in_specs=[pl.BlockSpec((1, BLOCK, dq), lambda bi, i: (bi, i, 0))] + kv_specs(kcol) + kv_specs(vcol)
                 + [_resident(sink_b.shape)],
        out_specs=pl.BlockSpec((1, BLOCK, dq), lambda bi, i: (bi, i, 0)),
        out_shape=jax.ShapeDtypeStruct((b, t, dq), BF16),
        compiler_params=_cparams(("parallel", "arbitrary")),
        name="win_attn",
    )(qkv, *([qkv] * 8), sink_b)


def _flash_kernel(n_lat_q, n_full, s_lat, n_ctx, q_ref, k_ref, vt_ref, o_ref, m_sc, acc_sc, s_sc):
    qi = pl.program_id(2)
    tq = q_ref.shape[1]
    group = AX_HEADS // AX_KV
    per = FLASH_CK // TM
    q = jnp.concatenate([q_ref[0, :, g * AX_HD:(g + 1) * AX_HD] for g in range(group)], axis=0)

    m_sc[...] = jnp.full(m_sc.shape, NEG, F32)
    acc_sc[...] = jnp.zeros(acc_sc.shape, F32)

    def scores(slot, start, size):
        s_sc[slot, :size, :] = _dot_nt(k_ref[0, pl.ds(start, size), :], q)

    def update(slot, block, size):
        s = s_sc[slot, :size, :]
        m_prev = m_sc[...]
        m_next = jnp.maximum(m_prev, jnp.max(s, axis=0, keepdims=True))
        p = jnp.exp2(s - m_next).astype(BF16)
        alpha = jnp.exp2(m_prev - m_next)
        pv = None
        for i in range(size // TM):
            part = _dot(vt_ref[0, 0, block + i], p[i * TM:(i + 1) * TM])
            pv = part if pv is None else pv + part
        acc_sc[...] = acc_sc[...] * alpha + pv
        m_sc[...] = m_next

    @pl.when(qi < n_lat_q)
    def _():
        def at(c):
            return pl.multiple_of(c * FLASH_CK, FLASH_CK)

        scores(0, 0, FLASH_CK)

        def body(i, carry):
            c = FLASH_UNROLL * i
            for u in range(FLASH_UNROLL):
                scores((u + 1) % 2, at(c + u + 1), FLASH_CK)
                update(u % 2, (c + u) * per, FLASH_CK)
            return carry

        n_iter = (n_full - 1) // FLASH_UNROLL
        lax.fori_loop(0, n_iter, body, 0)
        tail = [(c * FLASH_CK, c * per, FLASH_CK) for c in range(FLASH_UNROLL * n_iter, n_full)]
        tail.append((s_lat, s_lat // TM, n_ctx))
        for idx, (start, block, size) in enumerate(tail):
            if idx + 1 < len(tail):
                scores((idx + 1) % 2, tail[idx + 1][0], tail[idx + 1][2])
            update(idx % 2, block, size)

    @pl.when(qi >= n_lat_q)
    def _():
        scores(0, s_lat, n_ctx)
        update(0, s_lat // TM, n_ctx)

    acc = acc_sc[...]
    o = acc[:AX_HD] / acc[AX_HD:AX_HD + 1]
    for g in range(group):
        o_ref[0, :, g * AX_HD:(g + 1) * AX_HD] = o[:, g * tq:(g + 1) * tq].T.astype(BF16)


def _flash_call(qk, vt, s_lat, n_ctx):
    b, t, _ = qk.shape
    tq = FLASH_TQ
    group = AX_HEADS // AX_KV
    gw = group * AX_HD
    dq = AX_HEADS * AX_HD
    kcol = dq // AX_HD
    m_cols = group * tq
    return pl.pallas_call(
        functools.partial(_flash_kernel, s_lat // tq, s_lat // FLASH_CK, s_lat, n_ctx),
        grid=(b, AX_KV, t // tq),
        in_specs=[pl.BlockSpec((1, tq, gw), lambda bi, j, i: (bi, i, j)),
                  pl.BlockSpec((1, t, AX_HD), lambda bi, j, i: (bi, 0, kcol + j)),
                  pl.BlockSpec((1, 1) + vt.shape[2:], lambda bi, j, i: (bi, j, 0, 0, 0))],
        out_specs=pl.BlockSpec((1, tq, gw), lambda bi, j, i: (bi, i, j)),
        out_shape=jax.ShapeDtypeStruct((b, t, dq), BF16),
        scratch_shapes=[pltpu.VMEM((1, m_cols), F32), pltpu.VMEM((VT_ROWS, m_cols), F32),
                        pltpu.VMEM((2, FLASH_CK, m_cols), F32)],
        compiler_params=_cparams(("parallel", "parallel", "arbitrary")),
        name="flash_attn",
    )(qk, qk, vt)


def _mlstm_kernel(qf_ref, gf_ref, qb_ref, gb_ref, hf_ref, hb_ref, c_sc, n_sc, m_sc):
    step = pl.program_id(1)
    L = ML_CHUNK

    @pl.when(step == 0)
    def _():
        c_sc[...] = jnp.zeros(c_sc.shape, F32)
        n_sc[...] = jnp.zeros(n_sc.shape, F32)
        m_sc[...] = jnp.full(m_sc.shape, NEG, F32)

    r = lax.broadcasted_iota(jnp.int32, (L, L), 0)
    c = lax.broadcasted_iota(jnp.int32, (L, L), 1)
    lo = c <= r
    up = c >= r
    lo_f = lo.astype(F32)
    up_f = up.astype(F32)
    hi = lax.Precision.HIGHEST
    nqk = ML_HEADS * ML_DK

    dirs = ((qf_ref, gf_ref, hf_ref), (qb_ref, gb_ref, hb_ref))
    units = [(d, h) for d in range(2) for h in range(ML_HEADS)]

    def qkv(d, h):
        x_ref = dirs[d][0]
        return (x_ref[0, :, h * ML_DK:(h + 1) * ML_DK],
                x_ref[0, :, nqk + h * ML_DK:nqk + (h + 1) * ML_DK],
                x_ref[0, :, 2 * nqk + h * ML_DV:2 * nqk + (h + 1) * ML_DV])

    stats = []
    for d in range(2):
        gates = dirs[d][1][0]
        gates_t = gates.T
        tri_col, tri_row = (lo_f, up_f) if d == 0 else (up_f, lo_f)
        cum_col = jnp.dot(tri_col, gates, precision=hi, preferred_element_type=F32)
        cum_row = jnp.dot(gates_t, tri_row, precision=hi, preferred_element_type=F32)
        total = jnp.sum(gates, axis=0, keepdims=True)
        stats.append((gates, gates_t, cum_col, cum_row, total))

    s_mat, q_c = {}, {}
    for d, h in units:
        q, k, _ = qkv(d, h)
        s_mat[d, h] = _dot_nt(q, k)
        q_c[d, h] = _dot_nt(q, c_sc[d, h].astype(BF16))

    c_bar, n_bar, m_bar = {}, {}, {}
    for d, h in units:
        gates, _, cum_col, _, total = stats[d]
        ci = 2 * ML_HEADS * d + h
        cf = ci + ML_HEADS
        _, k, v = qkv(d, h)
        a_col = total[:, cf:cf + 1] - cum_col[:, cf:cf + 1] + gates[:, ci:ci + 1]
        m_bar[d, h] = jnp.max(a_col, axis=0, keepdims=True)
        w_col = jnp.exp(a_col - m_bar[d, h])
        c_bar[d, h] = _dot_tn((v.astype(F32) * w_col).astype(BF16), k)
        n_bar[d, h] = jnp.sum(k.astype(F32) * w_col, axis=0, keepdims=True)

    num, den, m_ts = {}, {}, {}
    for d, h in units:
        _, gates_t, cum_col, cum_row, _ = stats[d]
        mask = lo if d == 0 else up
        ci = 2 * ML_HEADS * d + h
        cf = ci + ML_HEADS
        q, _, v = qkv(d, h)
        n_prev = n_sc[d, h][0:1, :]
        m_prev = m_sc[d, h][0:1, 0:1]
        f_col = cum_col[:, cf:cf + 1]
        dmat = jnp.where(mask, f_col - cum_row[cf:cf + 1, :] + gates_t[ci:ci + 1, :], NEG)
        inter = f_col + m_prev
        m_t = jnp.maximum(inter, jnp.max(dmat, axis=-1, keepdims=True))
        w_inter = jnp.exp(inter - m_t)
        qk = s_mat[d, h] * jnp.exp(dmat - m_t)
        qn = jnp.sum(q.astype(F32) * n_prev, axis=-1, keepdims=True)
        num[d, h] = _dot(qk.astype(BF16), v) + w_inter * q_c[d, h]
        den[d, h] = jnp.sum(qk, axis=-1, keepdims=True) + w_inter * qn
        m_ts[d, h] = m_t

    for d, h in units:
        dirs[d][2][0, :, h * ML_DV:(h + 1) * ML_DV] = num[d, h] / jnp.maximum(jnp.abs(den[d, h]), jnp.exp(-m_ts[d, h]))

    for d, h in units:
        total = stats[d][4]
        cf = 2 * ML_HEADS * d + h + ML_HEADS
        f_tot = total[:, cf:cf + 1]
        n_prev = n_sc[d, h][0:1, :]
        m_prev = m_sc[d, h][0:1, 0:1]
        m_new = jnp.maximum(f_tot + m_prev, m_bar[d, h])
        decay = jnp.exp(f_tot + m_prev - m_new)
        inj = jnp.exp(m_bar[d, h] - m_new)
        c_sc[d, h] = decay * c_sc[d, h] + inj * c_bar[d, h]
        n_sc[d, h] = jnp.broadcast_to(decay * n_prev + inj * n_bar[d, h], (SUBLANES, ML_DK))
        m_sc[d, h] = jnp.broadcast_to(m_new, (SUBLANES, LANES))


def _mlstm_call(qkv, gates, s_lat):
    b, t, nx = qkv.shape
    nc = t // ML_CHUNK
    nlc = s_lat // ML_CHUNK
    nv = ML_HEADS * ML_DV

    def fwd(bi, s):
        return (bi, (s + nlc) % nc, 0)

    def bwd(bi, s):
        return (bi, nc - 1 - s, 0)

    return pl.pallas_call(
        _mlstm_kernel,
        grid=(b, nc),
        in_specs=[pl.BlockSpec((1, ML_CHUNK, nx), fwd), pl.BlockSpec((1, ML_CHUNK, LANES), fwd),
                  pl.BlockSpec((1, ML_CHUNK, nx), bwd), pl.BlockSpec((1, ML_CHUNK, LANES), bwd)],
        out_specs=(pl.BlockSpec((1, ML_CHUNK, nv), fwd), pl.BlockSpec((1, ML_CHUNK, nv), bwd)),
        out_shape=(jax.ShapeDtypeStruct((b, t, nv), F32), jax.ShapeDtypeStruct((b, t, nv), F32)),
        scratch_shapes=[pltpu.VMEM((2, ML_HEADS, ML_DV, ML_DK), F32),
                        pltpu.VMEM((2, ML_HEADS, SUBLANES, ML_DK), F32),
                        pltpu.VMEM((2, ML_HEADS, SUBLANES, LANES), F32)],
        compiler_params=_cparams(("parallel", "arbitrary")),
        name="mlstm",
    )(qkv, gates, qkv, gates)


def _outproj_kernel(kind, *refs):
    if kind == "attn":
        y_ref, h_ref, mod_ref, w_ref, o_ref = refs
        y = y_ref[0]
    else:
        hf_ref, hb_ref, og_ref, hn_ref, h_ref, mod_ref, w_ref, o_ref = refs
        hs = hf_ref[0] + hb_ref[0]
        parts = []
        for h in range(ML_HEADS):
            sl = slice(h * ML_DV, (h + 1) * ML_DV)
            x = hs[:, sl]
            xn = x * lax.rsqrt(jnp.mean(x * x, axis=-1, keepdims=True) + EPS) * hn_ref[:, sl]
            parts.append((_sigmoid(og_ref[0, :, sl]) * xn).astype(BF16))
        y = jnp.concatenate(parts, axis=1)
    o_ref[0] = h_ref[0] + mod_ref[0][2:3] * _dot(y, w_ref[...])


def _outproj_call(kind, ins, h, mods, w, n_lat_tiles, n_tiles):
    b, t, d = h.shape
    ctx_row = mods.shape[0] - 1
    row = lambda bi, j: (bi, j, 0)
    if kind == "attn":
        (y,) = ins
        specs = [pl.BlockSpec((1, TM, y.shape[2]), row)]
    else:
        hf, hb, og, hn = ins
        specs = [pl.BlockSpec((1, TM, hf.shape[2]), row)] * 3 + [_resident(hn.shape)]
    return pl.pallas_call(
        functools.partial(_outproj_kernel, kind),
        grid=(b, n_tiles),
        in_specs=specs + [pl.BlockSpec((1, TM, d), row), _mod_spec(n_lat_tiles, ctx_row), _resident(w.shape)],
        out_specs=pl.BlockSpec((1, TM, d), row),
        out_shape=jax.ShapeDtypeStruct((b, n_tiles * TM, d), F32),
        compiler_params=_cparams(("parallel", "arbitrary")),
        name="outproj_" + kind,
    )(*ins, h, mods, w)


def _convmlp_kernel(kind, n_lat_tiles, n_all_tiles, hp_ref, h_ref, hn_ref, mod_ref, g_ref, w1_ref, cw_ref, w2_ref, o_ref,
                    perm_sc):
    j = pl.program_id(1)
    m = mod_ref[0]
    sh, sc, gt = (m[0:1], m[1:2], m[2:3]) if kind == "sc" else (m[3:4], m[4:5], m[5:6])
    x = jnp.concatenate([hp_ref[0], h_ref[0], hn_ref[0]], axis=0)
    rows = x.shape[0]
    a = _rms_mod(x, g_ref[...], sh, sc)
    seg_first = (j == 0) | (j == n_lat_tiles)
    seg_last = (j == n_lat_tiles - 1) | (j == n_all_tiles - 1)
    r = lax.broadcasted_iota(jnp.int32, (rows, 1), 0)
    dead = ((r < HALO) & seg_first) | ((r >= rows - HALO) & seg_last)
    a = jnp.where(dead, 0.0, a)

    pitch = rows // SUBLANES
    n_slab = a.shape[1] // LANES
    for k in range(n_slab):
        perm_sc[k] = a[:, k * LANES:(k + 1) * LANES]
    a = jnp.concatenate(
        [jnp.concatenate([perm_sc[k, pl.ds(r, SUBLANES, stride=pitch), :] for k in range(n_slab)], axis=1)
         for r in range(pitch)], axis=0).astype(BF16)

    def conv3(z, col):
        w = cw_ref[:, col:col + CONV_CHUNK]
        prev = jnp.concatenate([pltpu.roll(z[rows - SUBLANES:], 1, 0), z[:rows - SUBLANES]], axis=0)
        nxt = jnp.concatenate([z[SUBLANES:], pltpu.roll(z[:SUBLANES], SUBLANES - 1, 0)], axis=0)
        return prev * w[0:1] + z * w[1:2] + nxt * w[2:3]

    hidden = w2_ref.shape[0]
    parts = w1_ref.shape[1] // hidden

    def up(c):
        return [_dot(a, w1_ref[:, p * hidden + c * CONV_CHUNK:p * hidden + (c + 1) * CONV_CHUNK]) for p in range(parts)]

    n_chunks = hidden // CONV_CHUNK
    acc = None
    u_next = up(0)
    for c in range(n_chunks):
        u = u_next
        if c + 1 < n_chunks:
            u_next = up(c + 1)
        if kind == "sc":
            hid = u[0] * conv3(u[1] * u[2], c * CONV_CHUNK)
        else:
            gg = conv3(u[0], c * CONV_CHUNK)
            uu = conv3(u[1], hidden + c * CONV_CHUNK)
            hid = gg * _sigmoid(gg) * uu
        y = _dot(hid.astype(BF16), w2_ref[c * CONV_CHUNK:(c + 1) * CONV_CHUNK, :])
        acc = y if acc is None else acc + y

    y = gt * acc
    for r in range(pitch):
        for k in range(n_slab):
            perm_sc[k, pl.ds(r, SUBLANES, stride=pitch), :] = y[r * SUBLANES:(r + 1) * SUBLANES, k * LANES:(k + 1) * LANES]
    o_ref[0] = h_ref[0] + jnp.concatenate([perm_sc[k, HALO:rows - HALO, :] for k in range(n_slab)], axis=1)


def _convmlp_call(kind, h, mods, gain, w1, cw, w2, n_lat_tiles, n_all_tiles, n_tiles):
    b, t, d = h.shape
    ctx_row = mods.shape[0] - 1
    per = TM // HALO
    last_halo = t // HALO - 1
    row = lambda bi, j: (bi, j, 0)
    return pl.pallas_call(
        functools.partial(_convmlp_kernel, kind, n_lat_tiles, n_all_tiles),
        grid=(b, n_tiles),
        in_specs=[pl.BlockSpec((1, HALO, d), lambda bi, j: (bi, jnp.maximum(j * per - 1, 0), 0)),
                  pl.BlockSpec((1, TM, d), row),
                  pl.BlockSpec((1, HALO, d), lambda bi, j: (bi, jnp.minimum((j + 1) * per, last_halo), 0)),
                  _mod_spec(n_lat_tiles, ctx_row),
                  _resident((1, d)), _resident(w1.shape), _resident(cw.shape), _resident(w2.shape)],
        out_specs=pl.BlockSpec((1, TM, d), row),
        out_shape=jax.ShapeDtypeStruct((b, n_tiles * TM, d), F32),
        scratch_shapes=[pltpu.VMEM((d // LANES, TM + 2 * HALO, LANES), F32)],
        compiler_params=_cparams(("parallel", "arbitrary")),
        name="convmlp_" + kind,
    )(h, h, h, mods, gain, w1, cw, w2)


def _rope_tables(s_lat, n_ctx, hd, reps):
    rows = s_lat // GRID_W
    row = np.repeat(np.arange(rows, dtype=np.float32), GRID_W)
    col = np.tile(np.arange(GRID_W, dtype=np.float32), rows)
    n_freq = hd // 4
    inv = jnp.power(ROPE_THETA, -jnp.arange(n_freq, dtype=F32) / n_freq)
    ang = jnp.concatenate([jnp.asarray(row)[:, None] * inv, jnp.asarray(col)[:, None] * inv], axis=-1)
    cos, sin = jnp.cos(ang), jnp.sin(ang)
    cos = jnp.tile(jnp.concatenate([cos, cos], axis=1), (1, reps))
    sin = jnp.tile(jnp.concatenate([-sin, sin], axis=1), (1, reps))
    cos = jnp.concatenate([cos, jnp.ones((n_ctx, LANES), F32)], axis=0)
    sin = jnp.concatenate([sin, jnp.zeros((n_ctx, LANES), F32)], axis=0)
    return cos, sin


def kernel(x, c, ctx, c_ctx, ada_w, ada_b, norm_mix, norm_ffn, ffn_w_up, ffn_conv, ffn_w_down, win_w_qkv, win_q_norm, win_k_norm, win_sink, win_w_o, sc_w_in, sc_conv, sc_w_out, ax_w_qkv, ax_q_norm, ax_k_norm, ax_w_o, ml_w_in, ml_b_gate, ml_h_norm, ml_w_out):
    b, s_lat, d = x.shape
    n_ctx = ctx.shape[1]
    assert d == D_MODEL and s_lat % FLASH_CK == 0 and n_ctx == TM and s_lat % GRID_W == 0
    n_lat_tiles = s_lat // TM
    n_all_tiles = n_lat_tiles + n_ctx // TM
    depth = ada_w.shape[0]

    h = jnp.concatenate([x, ctx], axis=1)
    pad_rows = -(b + 1) % SUBLANES
    cvec = jnp.concatenate([c, c_ctx[None], jnp.zeros((pad_rows, d), F32)], axis=0)
    mods = _ada_call(cvec, ada_w, ada_b)[:, :b + 1].reshape(depth, b + 1, 6, d)

    for i in range(depth):
        kind, j = i % N_MIXERS, i // N_MIXERS
        last = i == depth - 1
        n_tiles = n_lat_tiles if last else n_all_tiles
        m = mods[i]
        g_mix = norm_mix[i][None]
        if kind == 0:
            gain = jnp.concatenate([jnp.tile(win_q_norm[j], WIN_HEADS) * WIN_HD ** -0.5,
                                    jnp.tile(win_k_norm[j], WIN_KV)])[None]
            cos, sin = _rope_tables(s_lat, n_ctx, WIN_HD, 2)
            qkv = _inproj_call("win", h, m, g_mix, win_w_qkv[j].astype(BF16), (gain, cos, sin), n_lat_tiles)
            sink_b = jnp.broadcast_to(win_sink[j][:, None], (WIN_HEADS, LANES))
            att = _win_call(qkv, sink_b, s_lat, n_ctx)
            h = _outproj_call("attn", (att,), h, m, win_w_o[j].astype(BF16), n_lat_tiles, n_tiles)
        elif kind == 1:
            h = _convmlp_call("sc", h, m, g_mix, sc_w_in[j].astype(BF16), sc_conv[j], sc_w_out[j].astype(BF16),
                              n_lat_tiles, n_all_tiles, n_tiles)
        elif kind == 2:
            gain = jnp.concatenate([jnp.tile(ax_q_norm[j], AX_HEADS) * (AX_HD ** -0.5 * np.log2(np.e)),
                                    jnp.tile(ax_k_norm[j], AX_KV)])[None]
            cos, sin = _rope_tables(s_lat, n_ctx, AX_HD, 1)
            qk, vt = _inproj_call("ax", h, m, g_mix, ax_w_qkv[j].astype(BF16), (gain, cos, sin), n_lat_tiles)
            att = _flash_call(qk, vt, s_lat, n_ctx)
            h = _outproj_call("attn", (att,), h, m, ax_w_o[j].astype(BF16), n_lat_tiles, n_tiles)
        else:
            w = jnp.concatenate([ml_w_in[j], jnp.zeros((d, LANES - 4 * ML_HEADS), F32)], axis=1).astype(BF16)
            bg = jnp.concatenate([ml_b_gate[j], jnp.zeros((LANES - 4 * ML_HEADS,), F32)])[None]
            qkv, og, gates = _inproj_call("ml", h, m, g_mix, w, (bg,), n_lat_tiles)
            hf, hb = _mlstm_call(qkv, gates, s_lat)
            hn = jnp.tile(ml_h_norm[j], ML_HEADS)[None]
            h = _outproj_call("ml", (hf, hb, og, hn), h, m, ml_w_out[j].astype(BF16), n_lat_tiles, n_tiles)
        h = _convmlp_call("ffn", h, m, norm_ffn[i][None], ffn_w_up[i].astype(BF16), ffn_conv[i], ffn_w_down[i].astype(BF16),
                          n_lat_tiles, n_all_tiles, n_tiles)
    return h[:, :s_lat] if h.shape[1] != s_lat else h
```

```python
import functools

import numpy as np
import jax
import jax.numpy as jnp
from jax import lax
from jax.experimental import pallas as pl
from jax.experimental.pallas import tpu as pltpu

D_MODEL = 1024
DEPTH = 4
GRID_W = 64
N_MIXERS = 4
BLOCK = 128
WINDOW = 128
WIN_HEADS = 16
WIN_KV = 4
WIN_HD = 64
AX_HEADS = 8
AX_KV = 2
AX_HD = 128
ML_HEADS = 4
ML_DK = 128
ML_DV = 256
ML_CHUNK = 128
D_FF = 2816
ROPE_THETA = 10000.0
EPS = 1e-6
NEG = -1e30

F32 = jnp.float32
BF16 = jnp.bfloat16

LANES = 128
SUBLANES = 8
TM = 256
HALO = SUBLANES
CONV_CHUNK = 256
FLASH_TQ = 256
FLASH_CK = 512
FLASH_UNROLL = 4
ML_BATCH = 1
BF16_ROWS = 16
VT_ROWS = AX_HD + BF16_ROWS
VMEM_LIMIT = 56 * 1024 * 1024


def _cparams(sem):
    return pltpu.CompilerParams(dimension_semantics=sem, vmem_limit_bytes=VMEM_LIMIT)


def _resident(shape):
    nd = len(shape)
    return pl.BlockSpec(shape, lambda *_: (0,) * nd, pipeline_mode=pl.Buffered(1))


def _dot(a, b):
    return jnp.dot(a, b, preferred_element_type=F32)


def _dot_nt(a, b):
    return lax.dot_general(a, b, (((1,), (1,)), ((), ())), preferred_element_type=F32)


def _dot_tn(a, b):
    return lax.dot_general(a, b, (((0,), (0,)), ((), ())), preferred_element_type=F32)


def _sigmoid(x):
    return 1.0 / (1.0 + jnp.exp(-x))


def _rms_mod(x, g, shift, scale):
    y = x * lax.rsqrt(jnp.mean(x * x, axis=-1, keepdims=True) + EPS) * g
    return y * (1.0 + scale) + shift


def _ada_kernel(c_ref, w_ref, b_ref, o_ref):
    c = c_ref[...]
    o_ref[0] = _dot(c * _sigmoid(c), w_ref[0]) + b_ref[0]


def _ada_call(cvec, ada_w, ada_b):
    depth, d, n = ada_w.shape
    nb = n // 4
    return pl.pallas_call(
        _ada_kernel,
        grid=(depth, n // nb),
        in_specs=[pl.BlockSpec(cvec.shape, lambda l, j: (0, 0)),
                  pl.BlockSpec((1, d, nb), lambda l, j: (l, 0, j)),
                  pl.BlockSpec((1, 1, nb), lambda l, j: (l, 0, j))],
        out_specs=pl.BlockSpec((1, cvec.shape[0], nb), lambda l, j: (l, 0, j)),
        out_shape=jax.ShapeDtypeStruct((depth, cvec.shape[0], n), F32),
        compiler_params=_cparams(("arbitrary", "arbitrary")),
        name="ada",
    )(cvec, ada_w, ada_b.reshape(depth, 1, n))


def _inproj_kernel(kind, h_ref, mod_ref, g_ref, w_ref, *rest):
    m = mod_ref[0]
    a = _rms_mod(h_ref[0], g_ref[...], m[0:1], m[1:2]).astype(BF16)
    y = _dot(a, w_ref[...])
    tm = y.shape[0]
    if kind in ("win", "ax"):
        if kind == "win":
            gain_ref, cos_ref, sin_ref, o_ref = rest
        else:
            gain_ref, cos_ref, sin_ref, o_ref, vt_ref = rest
        cos = cos_ref[...]
        sin = sin_ref[...]
        lane = lax.broadcasted_iota(jnp.int32, (tm, LANES), 1)
        n_rot = gain_ref.shape[1] // LANES
        for t in range(n_rot):
            sl = slice(t * LANES, (t + 1) * LANES)
            xt = y[:, sl]
            x2 = xt * xt
            if kind == "win":
                lo = lane < WIN_HD
                s_lo = jnp.sum(jnp.where(lo, x2, 0.0), axis=-1, keepdims=True)
                s_hi = jnp.sum(jnp.where(lo, 0.0, x2), axis=-1, keepdims=True)
                ms = jnp.where(lo, s_lo, s_hi) * (1.0 / WIN_HD)
                xn = xt * lax.rsqrt(ms + EPS) * gain_ref[:, sl]
                first = (lane % WIN_HD) < (WIN_HD // 2)
                sw = jnp.where(first, pltpu.roll(xn, LANES - WIN_HD // 2, 1), pltpu.roll(xn, WIN_HD // 2, 1))
            else:
                ms = jnp.mean(x2, axis=-1, keepdims=True)
                xn = xt * lax.rsqrt(ms + EPS) * gain_ref[:, sl]
                sw = pltpu.roll(xn, AX_HD // 2, 1)
            o_ref[0, :, sl] = (xn * cos + sw * sin).astype(BF16)
        if kind == "win":
            o_ref[0, :, n_rot * LANES:] = y[:, n_rot * LANES:].astype(BF16)
        else:
            for j in range(AX_KV):
                src = (n_rot + j) * LANES
                vt_ref[0, j, 0, :AX_HD, :] = y[:, src:src + LANES].T.astype(BF16)
                vt_ref[0, j, 0, AX_HD:, :] = jnp.ones((VT_ROWS - AX_HD, tm), BF16)
    else:
        bg_ref, qkv_ref, og_ref, gate_ref = rest
        nqk = 2 * ML_HEADS * ML_DK
        nv = ML_HEADS * ML_DV
        nq = ML_HEADS * ML_DK
        qkv_ref[0, :, :nq] = (y[:, :nq] * ML_DK ** -0.5).astype(BF16)
        qkv_ref[0, :, nq:nqk + nv] = y[:, nq:nqk + nv].astype(BF16)
        og_ref[0] = y[:, nqk + nv:nqk + 2 * nv]
        g = y[:, nqk + 2 * nv:] + bg_ref[...]
        lane = lax.broadcasted_iota(jnp.int32, g.shape, 1)
        is_forget = (lane % (2 * ML_HEADS)) >= ML_HEADS
        log_sig = jnp.minimum(g, 0.0) - jnp.log1p(jnp.exp(-jnp.abs(g)))
        gate_ref[0] = jnp.where(is_forget, log_sig, g)


def _mod_spec(n_lat_tiles, ctx_row):
    return pl.BlockSpec((1, 6, D_MODEL), lambda b, j: (jnp.where(j < n_lat_tiles, b, ctx_row), 0, 0))


def _inproj_call(kind, h, mods, gain, w, extras, n_lat_tiles):
    b, t, d = h.shape
    n = w.shape[1]
    ctx_row = mods.shape[0] - 1
    in_specs = [pl.BlockSpec((1, TM, d), lambda bi, j: (bi, j, 0)),
                _mod_spec(n_lat_tiles, ctx_row),
                _resident((1, d)),
                _resident(w.shape)]
    if kind in ("win", "ax"):
        hgain, cos, sin = extras
        in_specs += [_resident(hgain.shape),
                     pl.BlockSpec((TM, LANES), lambda bi, j: (j, 0)),
                     pl.BlockSpec((TM, LANES), lambda bi, j: (j, 0))]
        if kind == "win":
            out_shape = jax.ShapeDtypeStruct((b, t, n), BF16)
            out_specs = pl.BlockSpec((1, TM, n), lambda bi, j: (bi, j, 0))
        else:
            n_qk = n - AX_KV * AX_HD
            out_shape = (jax.ShapeDtypeStruct((b, t, n_qk), BF16),
                         jax.ShapeDtypeStruct((b, AX_KV, t // TM, VT_ROWS, TM), BF16))
            out_specs = (pl.BlockSpec((1, TM, n_qk), lambda bi, j: (bi, j, 0)),
                         pl.BlockSpec((1, AX_KV, 1, VT_ROWS, TM), lambda bi, j: (bi, 0, j, 0, 0)))
        args = (hgain, cos, sin)
    else:
        (bg,) = extras
        nqkv = 2 * ML_HEADS * ML_DK + ML_HEADS * ML_DV
        nv = ML_HEADS * ML_DV
        in_specs += [_resident(bg.shape)]
        out_shape = (jax.ShapeDtypeStruct((b, t, nqkv), BF16),
                     jax.ShapeDtypeStruct((b, t, nv), F32),
                     jax.ShapeDtypeStruct((b, t, LANES), F32))
        out_specs = (pl.BlockSpec((1, TM, nqkv), lambda bi, j: (bi, j, 0)),
                     pl.BlockSpec((1, TM, nv), lambda bi, j: (bi, j, 0)),
                     pl.BlockSpec((1, TM, LANES), lambda bi, j: (bi, j, 0)))
        args = (bg,)
    return pl.pallas_call(
        functools.partial(_inproj_kernel, kind),
        grid=(b, t // TM),
        in_specs=in_specs, out_specs=out_specs, out_shape=out_shape,
        compiler_params=_cparams(("parallel", "arbitrary")),
        name="inproj_" + kind,
    )(h, mods, gain, w, *args)


def _win_heads(q_ref, k, v, bias, sink_ref, o_ref):
    tq = q_ref.shape[1]
    lane = lax.broadcasted_iota(jnp.int32, (tq, LANES), 1)
    lo = lane < WIN_HD
    group = WIN_HEADS // WIN_KV
    if bias is not None:
        bias = jnp.concatenate([bias] * group, axis=0)

    def scores(kv):
        kt, khalf = kv // 2, kv % 2
        qs = []
        for g in range(group):
            head = kv * group + g
            tile, half = head // 2, head % 2
            qt = q_ref[0, :, tile * LANES:(tile + 1) * LANES].astype(F32)
            qh = jnp.where(lo, qt, 0.0) if half == 0 else jnp.where(lo, 0.0, qt)
            if half != khalf:
                qh = pltpu.roll(qh, WIN_HD, 1)
            qs.append(qh.astype(BF16))
        return _dot_nt(jnp.concatenate(qs, axis=0), k[:, kt * LANES:(kt + 1) * LANES])

    s_next = scores(0)
    for kv in range(WIN_KV):
        kt, khalf = kv // 2, kv % 2
        s = s_next
        if kv + 1 < WIN_KV:
            s_next = scores(kv + 1)
        if bias is not None:
            s = s + bias
        sink = jnp.concatenate(
            [jnp.broadcast_to(sink_ref[kv * group + g:kv * group + g + 1, :][:, 0:1], (tq, 1)) for g in range(group)],
            axis=0)
        mx = jnp.maximum(jnp.max(s, axis=-1, keepdims=True), sink)
        e = jnp.exp(s - mx)
        den = jnp.sum(e, axis=-1, keepdims=True) + jnp.exp(sink - mx)
        o = _dot(e.astype(BF16), v[:, kt * LANES:(kt + 1) * LANES]) / den
        for pair in range(group // 2):
            halves = []
            for half in range(2):
                og = o[(2 * pair + half) * tq:(2 * pair + half + 1) * tq]
                halves.append(pltpu.roll(og, WIN_HD, 1) if half != khalf else og)
            tile = (kv * group) // 2 + pair
            o_ref[0, :, tile * LANES:(tile + 1) * LANES] = jnp.where(lo, halves[0], halves[1]).astype(BF16)


def _win_kernel(n_lat, q_ref, kp, kc, kn, kx, vp, vc, vn, vx, sink_ref, o_ref):
    i = pl.program_id(1)
    tq = q_ref.shape[1]
    n_ctx = kx.shape[1]

    @pl.when(i < n_lat)
    def _():
        nk = 3 * tq + n_ctx
        col = lax.broadcasted_iota(jnp.int32, (1, nk), 1)
        t = lax.broadcasted_iota(jnp.int32, (tq, nk), 1) - lax.broadcasted_iota(jnp.int32, (tq, nk), 0)
        big = 4 * nk
        t_min = jnp.where(col < tq, jnp.where(i > 0, 0, big), -big)
        t_max = jnp.where((col >= 2 * tq) & (col < 3 * tq), jnp.where(i < n_lat - 1, 2 * tq, -big), big)
        bias = jnp.where(t < t_min, NEG, 0.0) + jnp.where(t > t_max, NEG, 0.0)
        k = jnp.concatenate([kp[0], kc[0], kn[0], kx[0]], axis=0)
        v = jnp.concatenate([vp[0], vc[0], vn[0], vx[0]], axis=0)
        _win_heads(q_ref, k, v, bias, sink_ref, o_ref)

    @pl.when(i >= n_lat)
    def _():
        _win_heads(q_ref, kx[0], vx[0], None, sink_ref, o_ref)


def _win_call(qkv, sink_b, s_lat, n_ctx):
    b, t, _ = qkv.shape
    n_lat = s_lat // BLOCK
    nq = t // BLOCK
    dq = WIN_HEADS * WIN_HD
    kw = WIN_KV * WIN_HD
    kcol = dq // kw
    vcol = kcol + 1
    last = n_lat - 1
    xrow = s_lat // n_ctx

    def kv_specs(col):
        return [pl.BlockSpec((1, BLOCK, kw), lambda bi, i: (bi, jnp.clip(i - 1, 0, last), col)),
                pl.BlockSpec((1, BLOCK, kw), lambda bi, i: (bi, jnp.minimum(i, last), col)),
                pl.BlockSpec((1, BLOCK, kw), lambda bi, i: (bi, jnp.minimum(i + 1, last), col)),
                pl.BlockSpec((1, n_ctx, kw), lambda bi, i: (bi, xrow, col))]

    return pl.pallas_call(
        functools.partial(_win_kernel, n_lat),
        grid=(b, nq),
        in_specs=[pl.BlockSpec((1, BLOCK, dq), lambda bi, i: (bi, i, 0))] + kv_specs(kcol) + kv_specs(vcol)
                 + [_resident(sink_b.shape)],
        out_specs=pl.BlockSpec((1, BLOCK, dq), lambda bi, i: (bi, i, 0)),
        out_shape=jax.ShapeDtypeStruct((b, t, dq), BF16),
        compiler_params=_cparams(("parallel", "arbitrary")),
        name="win_attn",
    )(qkv, *([qkv] * 8), sink_b)


def _flash_kernel(n_lat_q, n_full, s_lat, n_ctx, q_ref, k_ref, vt_ref, o_ref, m_sc, acc_sc, s_sc):
    qi = pl.program_id(2)
    tq = q_ref.shape[1]
    group = AX_HEADS // AX_KV
    per = FLASH_CK // TM
    q = jnp.concatenate([q_ref[0, :, g * AX_HD:(g + 1) * AX_HD] for g in range(group)], axis=0)

    m_sc[...] = jnp.full(m_sc.shape, NEG, F32)
    acc_sc[...] = jnp.zeros(acc_sc.shape, F32)

    def scores(slot, start, size):
        s_sc[slot, :size, :] = _dot_nt(k_ref[0, pl.ds(start, size), :], q)

    def update(slot, block, size):
        s = s_sc[slot, :size, :]
        m_prev = m_sc[...]
        m_next = jnp.maximum(m_prev, jnp.max(s, axis=0, keepdims=True))
        p = jnp.exp2(s - m_next).astype(BF16)
        alpha = jnp.exp2(m_prev - m_next)
        pv = None
        for i in range(size // TM):
            part = _dot(vt_ref[0, 0, block + i], p[i * TM:(i + 1) * TM])
            pv = part if pv is None else pv + part
        acc_sc[...] = acc_sc[...] * alpha + pv
        m_sc[...] = m_next

    @pl.when(qi < n_lat_q)
    def _():
        def at(c):
            return pl.multiple_of(c * FLASH_CK, FLASH_CK)

        scores(0, 0, FLASH_CK)

        def body(i, carry):
            c = FLASH_UNROLL * i
            for u in range(FLASH_UNROLL):
                scores((u + 1) % 2, at(c + u + 1), FLASH_CK)
                update(u % 2, (c + u) * per, FLASH_CK)
            return carry

        n_iter = (n_full - 1) // FLASH_UNROLL
        lax.fori_loop(0, n_iter, body, 0)
        tail = [(c * FLASH_CK, c * per, FLASH_CK) for c in range(FLASH_UNROLL * n_iter, n_full)]
        tail.append((s_lat, s_lat // TM, n_ctx))
        for idx, (start, block, size) in enumerate(tail):
            if idx + 1 < len(tail):
                scores((idx + 1) % 2, tail[idx + 1][0], tail[idx + 1][2])
            update(idx % 2, block, size)

    @pl.when(qi >= n_lat_q)
    def _():
        scores(0, s_lat, n_ctx)
        update(0, s_lat // TM, n_ctx)

    acc = acc_sc[...]
    o = acc[:AX_HD] / acc[AX_HD:AX_HD + 1]
    for g in range(group):
        o_ref[0, :, g * AX_HD:(g + 1) * AX_HD] = o[:, g * tq:(g + 1) * tq].T.astype(BF16)


def _flash_call(qk, vt, s_lat, n_ctx):
    b, t, _ = qk.shape
    tq = FLASH_TQ
    group = AX_HEADS // AX_KV
    gw = group * AX_HD
    dq = AX_HEADS * AX_HD
    kcol = dq // AX_HD
    m_cols = group * tq
    return pl.pallas_call(
        functools.partial(_flash_kernel, s_lat // tq, s_lat // FLASH_CK, s_lat, n_ctx),
        grid=(b, AX_KV, t // tq),
        in_specs=[pl.BlockSpec((1, tq, gw), lambda bi, j, i: (bi, i, j)),
                  pl.BlockSpec((1, t, AX_HD), lambda bi, j, i: (bi, 0, kcol + j)),
                  pl.BlockSpec((1, 1) + vt.shape[2:], lambda bi, j, i: (bi, j, 0, 0, 0))],
        out_specs=pl.BlockSpec((1, tq, gw), lambda bi, j, i: (bi, i, j)),
        out_shape=jax.ShapeDtypeStruct((b, t, dq), BF16),
        scratch_shapes=[pltpu.VMEM((1, m_cols), F32), pltpu.VMEM((VT_ROWS, m_cols), F32),
                        pltpu.VMEM((2, FLASH_CK, m_cols), F32)],
        compiler_params=_cparams(("parallel", "parallel", "arbitrary")),
        name="flash_attn",
    )(qk, qk, vt)


def _mlstm_kernel(qf_ref, gf_ref, qb_ref, gb_ref, hf_ref, hb_ref, c_sc, n_sc, m_sc):
    step = pl.program_id(1)
    L = ML_CHUNK

    @pl.when(step == 0)
    def _():
        c_sc[...] = jnp.zeros(c_sc.shape, F32)
        n_sc[...] = jnp.zeros(n_sc.shape, F32)
        m_sc[...] = jnp.full(m_sc.shape, NEG, F32)

    r = lax.broadcasted_iota(jnp.int32, (L, L), 0)
    c = lax.broadcasted_iota(jnp.int32, (L, L), 1)
    lo = c <= r
    up = c >= r
    lo_f = lo.astype(F32)
    up_f = up.astype(F32)
    hi = lax.Precision.HIGHEST
    nqk = ML_HEADS * ML_DK

    dirs = ((qf_ref, gf_ref, hf_ref), (qb_ref, gb_ref, hb_ref))
    n_streams = 2 * qf_ref.shape[0]
    units = [(s, h) for s in range(n_streams) for h in range(ML_HEADS)]

    def qkv(s, h):
        x_ref, bb = dirs[s % 2][0], s // 2
        return (x_ref[bb, :, h * ML_DK:(h + 1) * ML_DK],
                x_ref[bb, :, nqk + h * ML_DK:nqk + (h + 1) * ML_DK],
                x_ref[bb, :, 2 * nqk + h * ML_DV:2 * nqk + (h + 1) * ML_DV])

    stats = []
    for s in range(n_streams):
        d = s % 2
        gates = dirs[d][1][s // 2]
        gates_t = gates.T
        tri_col, tri_row = (lo_f, up_f) if d == 0 else (up_f, lo_f)
        cum_col = jnp.dot(tri_col, gates, precision=hi, preferred_element_type=F32)
        cum_row = jnp.dot(gates_t, tri_row, precision=hi, preferred_element_type=F32)
        total = jnp.sum(gates, axis=0, keepdims=True)
        stats.append((gates, gates_t, cum_col, cum_row, total))

    s_mat, q_c = {}, {}
    for s, h in units:
        q, k, _ = qkv(s, h)
        s_mat[s, h] = _dot_nt(q, k)
        q_c[s, h] = _dot_nt(q, c_sc[s, h].astype(BF16))

    c_bar, n_bar, m_bar = {}, {}, {}
    for s, h in units:
        gates, _, cum_col, _, total = stats[s]
        ci = 2 * ML_HEADS * (s % 2) + h
        cf = ci + ML_HEADS
        _, k, v = qkv(s, h)
        a_col = total[:, cf:cf + 1] - cum_col[:, cf:cf + 1] + gates[:, ci:ci + 1]
        m_bar[s, h] = jnp.max(a_col, axis=0, keepdims=True)
        w_col = jnp.exp(a_col - m_bar[s, h])
        c_bar[s, h] = _dot_tn((v.astype(F32) * w_col).astype(BF16), k)
        n_bar[s, h] = jnp.sum(k.astype(F32) * w_col, axis=0, keepdims=True)

    num, den, m_ts = {}, {}, {}
    for s, h in units:
        _, gates_t, cum_col, cum_row, _ = stats[s]
        mask = lo if s % 2 == 0 else up
        ci = 2 * ML_HEADS * (s % 2) + h
        cf = ci + ML_HEADS
        q, _, v = qkv(s, h)
        n_prev = n_sc[s, h][0:1, :]
        m_prev = m_sc[s, h][0:1, 0:1]
        f_col = cum_col[:, cf:cf + 1]
        dmat = jnp.where(mask, f_col - cum_row[cf:cf + 1, :] + gates_t[ci:ci + 1, :], NEG)
        inter = f_col + m_prev
        m_t = jnp.maximum(inter, jnp.max(dmat, axis=-1, keepdims=True))
        w_inter = jnp.exp(inter - m_t)
        qk = s_mat[s, h] * jnp.exp(dmat - m_t)
        qn = jnp.sum(q.astype(F32) * n_prev, axis=-1, keepdims=True)
        num[s, h] = _dot(qk.astype(BF16), v) + w_inter * q_c[s, h]
        den[s, h] = jnp.sum(qk, axis=-1, keepdims=True) + w_inter * qn
        m_ts[s, h] = m_t

    for s, h in units:
        dirs[s % 2][2][s // 2, :, h * ML_DV:(h + 1) * ML_DV] = (
            num[s, h] / jnp.maximum(jnp.abs(den[s, h]), jnp.exp(-m_ts[s, h])))

    for s, h in units:
        total = stats[s][4]
        cf = 2 * ML_HEADS * (s % 2) + h + ML_HEADS
        f_tot = total[:, cf:cf + 1]
        n_prev = n_sc[s, h][0:1, :]
        m_prev = m_sc[s, h][0:1, 0:1]
        m_new = jnp.maximum(f_tot + m_prev, m_bar[s, h])
        decay = jnp.exp(f_tot + m_prev - m_new)
        inj = jnp.exp(m_bar[s, h] - m_new)
        c_sc[s, h] = decay * c_sc[s, h] + inj * c_bar[s, h]
        n_sc[s, h] = jnp.broadcast_to(decay * n_prev + inj * n_bar[s, h], (SUBLANES, ML_DK))
        m_sc[s, h] = jnp.broadcast_to(m_new, (SUBLANES, LANES))


def _mlstm_call(qkv, gates, s_lat):
    b, t, nx = qkv.shape
    nc = t // ML_CHUNK
    nlc = s_lat // ML_CHUNK
    nv = ML_HEADS * ML_DV

    def fwd(bi, s):
        return (bi, (s + nlc) % nc, 0)

    def bwd(bi, s):
        return (bi, nc - 1 - s, 0)

    nb = ML_BATCH if b % ML_BATCH == 0 else 1
    return pl.pallas_call(
        _mlstm_kernel,
        grid=(b // nb, nc),
        in_specs=[pl.BlockSpec((nb, ML_CHUNK, nx), fwd), pl.BlockSpec((nb, ML_CHUNK, LANES), fwd),
                  pl.BlockSpec((nb, ML_CHUNK, nx), bwd), pl.BlockSpec((nb, ML_CHUNK, LANES), bwd)],
        out_specs=(pl.BlockSpec((nb, ML_CHUNK, nv), fwd), pl.BlockSpec((nb, ML_CHUNK, nv), bwd)),
        out_shape=(jax.ShapeDtypeStruct((b, t, nv), F32), jax.ShapeDtypeStruct((b, t, nv), F32)),
        scratch_shapes=[pltpu.VMEM((2 * nb, ML_HEADS, ML_DV, ML_DK), F32),
                        pltpu.VMEM((2 * nb, ML_HEADS, SUBLANES, ML_DK), F32),
                        pltpu.VMEM((2 * nb, ML_HEADS, SUBLANES, LANES), F32)],
        compiler_params=_cparams(("parallel", "arbitrary")),
        name="mlstm",
    )(qkv, gates, qkv, gates)


def _ml_readout(hs, og, hn_ref):
    parts = []
    for h in range(ML_HEADS):
        sl = slice(h * ML_DV, (h + 1) * ML_DV)
        x = hs[:, sl]
        xn = x * lax.rsqrt(jnp.mean(x * x, axis=-1, keepdims=True) + EPS) * hn_ref[:, sl]
        parts.append((_sigmoid(og[:, sl]) * xn).astype(BF16))
    return jnp.concatenate(parts, axis=1)


def _convmlp_kernel(kind, pre, n_lat_tiles, n_all_tiles, *refs):
    n_pre = {None: 0, "attn": 4, "ml": 11}[pre]
    pre_refs = refs[:n_pre]
    hp_ref, h_ref, hn_ref, mod_ref, g_ref, w1_ref, cw_ref, w2_ref, o_ref, perm_sc = refs[n_pre:]
    j = pl.program_id(1)
    m = mod_ref[0]
    sh, sc, gt = (m[0:1], m[1:2], m[2:3]) if kind == "sc" else (m[3:4], m[4:5], m[5:6])
    x = jnp.concatenate([hp_ref[0], h_ref[0], hn_ref[0]], axis=0)
    rows = x.shape[0]
    if pre == "attn":
        yp_ref, y_ref, yn_ref, wo_ref = pre_refs
        y = jnp.concatenate([yp_ref[0], y_ref[0], yn_ref[0]], axis=0)
        skip = BF16_ROWS - HALO
        x = x + m[2:3] * _dot(y, wo_ref[...])[skip:skip + rows]
    elif pre == "ml":
        ext = [jnp.concatenate([pre_refs[3 * i][0], pre_refs[3 * i + 1][0], pre_refs[3 * i + 2][0]], axis=0)
               for i in range(3)]
        gain_ref, wo_ref = pre_refs[9:]
        x = x + m[2:3] * _dot(_ml_readout(ext[0] + ext[1], ext[2], gain_ref), wo_ref[...])
    a = _rms_mod(x, g_ref[...], sh, sc)
    seg_first = (j == 0) | (j == n_lat_tiles)
    seg_last = (j == n_lat_tiles - 1) | (j == n_all_tiles - 1)
    r = lax.broadcasted_iota(jnp.int32, (rows, 1), 0)
    dead = ((r < HALO) & seg_first) | ((r >= rows - HALO) & seg_last)
    a = jnp.where(dead, 0.0, a)

    pitch = rows // SUBLANES
    n_slab = a.shape[1] // LANES
    for k in range(n_slab):
        perm_sc[k] = a[:, k * LANES:(k + 1) * LANES]
    a = jnp.concatenate(
        [jnp.concatenate([perm_sc[k, pl.ds(r, SUBLANES, stride=pitch), :] for k in range(n_slab)], axis=1)
         for r in range(pitch)], axis=0).astype(BF16)

    def conv3(z, col):
        w = cw_ref[:, col:col + CONV_CHUNK]
        prev = jnp.concatenate([pltpu.roll(z[rows - SUBLANES:], 1, 0), z[:rows - SUBLANES]], axis=0)
        nxt = jnp.concatenate([z[SUBLANES:], pltpu.roll(z[:SUBLANES], SUBLANES - 1, 0)], axis=0)
        return prev * w[0:1] + z * w[1:2] + nxt * w[2:3]

    hidden = w2_ref.shape[0]
    parts = w1_ref.shape[1] // hidden

    def up(c):
        return [_dot(a, w1_ref[:, p * hidden + c * CONV_CHUNK:p * hidden + (c + 1) * CONV_CHUNK]) for p in range(parts)]

    n_chunks = hidden // CONV_CHUNK
    acc = None
    u_next = up(0)
    for c in range(n_chunks):
        u = u_next
        if c + 1 < n_chunks:
            u_next = up(c + 1)
        if kind == "sc":
            hid = u[0] * conv3(u[1] * u[2], c * CONV_CHUNK)
        else:
            gg = conv3(u[0], c * CONV_CHUNK)
            uu = conv3(u[1], hidden + c * CONV_CHUNK)
            hid = gg * _sigmoid(gg) * uu
        y = _dot(hid.astype(BF16), w2_ref[c * CONV_CHUNK:(c + 1) * CONV_CHUNK, :])
        acc = y if acc is None else acc + y

    y = gt * acc
    for r in range(pitch):
        for k in range(n_slab):
            perm_sc[k, pl.ds(r, SUBLANES, stride=pitch), :] = y[r * SUBLANES:(r + 1) * SUBLANES, k * LANES:(k + 1) * LANES]
    o_ref[0] = x[HALO:rows - HALO] + jnp.concatenate([perm_sc[k, HALO:rows - HALO, :] for k in range(n_slab)], axis=1)


def _halo_specs(t, width, halo):
    per = TM // halo
    last = t // halo - 1
    return [pl.BlockSpec((1, halo, width), lambda bi, j: (bi, jnp.maximum(j * per - 1, 0), 0)),
            pl.BlockSpec((1, TM, width), lambda bi, j: (bi, j, 0)),
            pl.BlockSpec((1, halo, width), lambda bi, j: (bi, jnp.minimum((j + 1) * per, last), 0))]


def _convmlp_call(kind, h, mods, gain, w1, cw, w2, n_lat_tiles, n_all_tiles, n_tiles, pre=None, pre_ins=()):
    b, t, d = h.shape
    ctx_row = mods.shape[0] - 1
    if pre == "attn":
        y, w_o = pre_ins
        pre_specs = _halo_specs(t, y.shape[2], BF16_ROWS) + [_resident(w_o.shape)]
        pre_args = (y, y, y, w_o)
    elif pre == "ml":
        hf, hb, og, hn, w_o = pre_ins
        pre_specs = _halo_specs(t, hf.shape[2], HALO) * 3 + [_resident(hn.shape), _resident(w_o.shape)]
        pre_args = (hf, hf, hf, hb, hb, hb, og, og, og, hn, w_o)
    else:
        pre_specs, pre_args = [], ()
    return pl.pallas_call(
        functools.partial(_convmlp_kernel, kind, pre, n_lat_tiles, n_all_tiles),
        grid=(b, n_tiles),
        in_specs=pre_specs + _halo_specs(t, d, HALO) + [
            _mod_spec(n_lat_tiles, ctx_row),
            _resident((1, d)), _resident(w1.shape), _resident(cw.shape), _resident(w2.shape)],
        out_specs=pl.BlockSpec((1, TM, d), lambda bi, j: (bi, j, 0)),
        out_shape=jax.ShapeDtypeStruct((b, n_tiles * TM, d), F32),
        scratch_shapes=[pltpu.VMEM((d // LANES, TM + 2 * HALO, LANES), F32)],
        compiler_params=_cparams(("parallel", "arbitrary")),
        name="convmlp_" + kind + ("_" + pre if pre else ""),
    )(*pre_args, h, h, h, mods, gain, w1, cw, w2)


def _rope_tables(s_lat, n_ctx, hd, reps):
    rows = s_lat // GRID_W
    row = np.repeat(np.arange(rows, dtype=np.float32), GRID_W)
    col = np.tile(np.arange(GRID_W, dtype=np.float32), rows)
    n_freq = hd // 4
    inv = jnp.power(ROPE_THETA, -jnp.arange(n_freq, dtype=F32) / n_freq)
    ang = jnp.concatenate([jnp.asarray(row)[:, None] * inv, jnp.asarray(col)[:, None] * inv], axis=-1)
    cos, sin = jnp.cos(ang), jnp.sin(ang)
    cos = jnp.tile(jnp.concatenate([cos, cos], axis=1), (1, reps))
    sin = jnp.tile(jnp.concatenate([-sin, sin], axis=1), (1, reps))
    cos = jnp.concatenate([cos, jnp.ones((n_ctx, LANES), F32)], axis=0)
    sin = jnp.concatenate([sin, jnp.zeros((n_ctx, LANES), F32)], axis=0)
    return cos, sin


def kernel(x, c, ctx, c_ctx, ada_w, ada_b, norm_mix, norm_ffn, ffn_w_up, ffn_conv, ffn_w_down, win_w_qkv, win_q_norm, win_k_norm, win_sink, win_w_o, sc_w_in, sc_conv, sc_w_out, ax_w_qkv, ax_q_norm, ax_k_norm, ax_w_o, ml_w_in, ml_b_gate, ml_h_norm, ml_w_out):
    b, s_lat, d = x.shape
    n_ctx = ctx.shape[1]
    assert d == D_MODEL and s_lat % FLASH_CK == 0 and n_ctx == TM and s_lat % GRID_W == 0
    n_lat_tiles = s_lat // TM
    n_all_tiles = n_lat_tiles + n_ctx // TM
    depth = ada_w.shape[0]

    h = jnp.concatenate([x, ctx], axis=1)
    pad_rows = -(b + 1) % SUBLANES
    cvec = jnp.concatenate([c, c_ctx[None], jnp.zeros((pad_rows, d), F32)], axis=0)
    mods = _ada_call(cvec, ada_w, ada_b)[:, :b + 1].reshape(depth, b + 1, 6, d)

    for i in range(depth):
        kind, j = i % N_MIXERS, i // N_MIXERS
        last = i == depth - 1
        n_tiles = n_lat_tiles if last else n_all_tiles
        m = mods[i]
        g_mix = norm_mix[i][None]
        pre, pre_ins = None, ()
        if kind == 0:
            gain = jnp.concatenate([jnp.tile(win_q_norm[j], WIN_HEADS) * WIN_HD ** -0.5,
                                    jnp.tile(win_k_norm[j], WIN_KV)])[None]
            cos, sin = _rope_tables(s_lat, n_ctx, WIN_HD, 2)
            qkv = _inproj_call("win", h, m, g_mix, win_w_qkv[j].astype(BF16), (gain, cos, sin), n_lat_tiles)
            sink_b = jnp.broadcast_to(win_sink[j][:, None], (WIN_HEADS, LANES))
            pre, pre_ins = "attn", (_win_call(qkv, sink_b, s_lat, n_ctx), win_w_o[j].astype(BF16))
        elif kind == 1:
            h = _convmlp_call("sc", h, m, g_mix, sc_w_in[j].astype(BF16), sc_conv[j], sc_w_out[j].astype(BF16),
                              n_lat_tiles, n_all_tiles, n_tiles)
        elif kind == 2:
            gain = jnp.concatenate([jnp.tile(ax_q_norm[j], AX_HEADS) * (AX_HD ** -0.5 * np.log2(np.e)),
                                    jnp.tile(ax_k_norm[j], AX_KV)])[None]
            cos, sin = _rope_tables(s_lat, n_ctx, AX_HD, 1)
            qk, vt = _inproj_call("ax", h, m, g_mix, ax_w_qkv[j].astype(BF16), (gain, cos, sin), n_lat_tiles)
            pre, pre_ins = "attn", (_flash_call(qk, vt, s_lat, n_ctx), ax_w_o[j].astype(BF16))
        else:
            w = jnp.concatenate([ml_w_in[j], jnp.zeros((d, LANES - 4 * ML_HEADS), F32)], axis=1).astype(BF16)
            bg = jnp.concatenate([ml_b_gate[j], jnp.zeros((LANES - 4 * ML_HEADS,), F32)])[None]
            qkv, og, gates = _inproj_call("ml", h, m, g_mix, w, (bg,), n_lat_tiles)
            hf, hb = _mlstm_call(qkv, gates, s_lat)
            hn = jnp.tile(ml_h_norm[j], ML_HEADS)[None]
            pre, pre_ins = "ml", (hf, hb, og, hn, ml_w_out[j].astype(BF16))
        h = _convmlp_call("ffn", h, m, norm_ffn[i][None], ffn_w_up[i].astype(BF16), ffn_conv[i], ffn_w_down[i].astype(BF16),
                          n_lat_tiles, n_all_tiles, n_tiles, pre, pre_ins)
    return h[:, :s_lat] if h.shape[1] != s_lat else h
```

```python
import functools

import numpy as np
import jax
import jax.numpy as jnp
from jax import lax
from jax.experimental import pallas as pl
from jax.experimental.pallas import tpu as pltpu

D_MODEL = 1024
DEPTH = 4
GRID_W = 64
N_MIXERS = 4
BLOCK = 128
WINDOW = 128
WIN_HEADS = 16
WIN_KV = 4
WIN_HD = 64
AX_HEADS = 8
AX_KV = 2
AX_HD = 128
ML_HEADS = 4
ML_DK = 128
ML_DV = 256
ML_CHUNK = 128
D_FF = 2816
ROPE_THETA = 10000.0
EPS = 1e-6
NEG = -1e30

F32 = jnp.float32
BF16 = jnp.bfloat16

LANES = 128
SUBLANES = 8
TM = 256
HALO = SUBLANES
CONV_CHUNK = 256
PROJ_CHUNK = 256
FLASH_TQ = 256
FLASH_CK = 512
FLASH_UNROLL = 8
ML_BATCH = 1
BF16_ROWS = 16
VT_ROWS = AX_HD + BF16_ROWS
VMEM_LIMIT = 56 * 1024 * 1024


def _cparams(sem):
    return pltpu.CompilerParams(dimension_semantics=sem, vmem_limit_bytes=VMEM_LIMIT)


def _resident(shape):
    nd = len(shape)
    return pl.BlockSpec(shape, lambda *_: (0,) * nd, pipeline_mode=pl.Buffered(1))


def _dot(a, b):
    return jnp.dot(a, b, preferred_element_type=F32)


def _dot_nt(a, b):
    return lax.dot_general(a, b, (((1,), (1,)), ((), ())), preferred_element_type=F32)


def _dot_tn(a, b):
    return lax.dot_general(a, b, (((0,), (0,)), ((), ())), preferred_element_type=F32)


def _sigmoid(x):
    return 1.0 / (1.0 + jnp.exp(-x))


def _rms_mod(x, g, shift, scale):
    y = x * lax.rsqrt(jnp.mean(x * x, axis=-1, keepdims=True) + EPS) * g
    return y * (1.0 + scale) + shift


def _ada_kernel(c_ref, w_ref, b_ref, o_ref):
    c = c_ref[...]
    o_ref[0] = _dot(c * _sigmoid(c), w_ref[0]) + b_ref[0]


def _ada_call(cvec, ada_w, ada_b):
    depth, d, n = ada_w.shape
    nb = n // 4
    return pl.pallas_call(
        _ada_kernel,
        grid=(depth, n // nb),
        in_specs=[pl.BlockSpec(cvec.shape, lambda l, j: (0, 0)),
                  pl.BlockSpec((1, d, nb), lambda l, j: (l, 0, j)),
                  pl.BlockSpec((1, 1, nb), lambda l, j: (l, 0, j))],
        out_specs=pl.BlockSpec((1, cvec.shape[0], nb), lambda l, j: (l, 0, j)),
        out_shape=jax.ShapeDtypeStruct((depth, cvec.shape[0], n), F32),
        compiler_params=_cparams(("arbitrary", "arbitrary")),
        name="ada",
    )(cvec, ada_w, ada_b.reshape(depth, 1, n))


def _inproj_kernel(kind, h_ref, mod_ref, g_ref, w_ref, *rest):
    m = mod_ref[0]
    a = _rms_mod(h_ref[0], g_ref[...], m[0:1], m[1:2]).astype(BF16)
    tm = a.shape[0]
    lane = lax.broadcasted_iota(jnp.int32, (tm, LANES), 1)

    if kind in ("win", "ax"):
        if kind == "win":
            gain_ref, cos_ref, sin_ref, o_ref = rest
        else:
            gain_ref, cos_ref, sin_ref, o_ref, vt_ref = rest
        cos = cos_ref[...]
        sin = sin_ref[...]
        n_rot = gain_ref.shape[1] // LANES

        def emit(t, xt):
            sl = slice(t * LANES, (t + 1) * LANES)
            if t >= n_rot:
                if kind == "win":
                    o_ref[0, :, sl] = xt.astype(BF16)
                else:
                    vt_ref[0, t - n_rot, 0, :AX_HD, :] = xt.T.astype(BF16)
                    vt_ref[0, t - n_rot, 0, AX_HD:, :] = jnp.ones((VT_ROWS - AX_HD, tm), BF16)
                return
            x2 = xt * xt
            if kind == "win":
                lo = lane < WIN_HD
                s_lo = jnp.sum(jnp.where(lo, x2, 0.0), axis=-1, keepdims=True)
                s_hi = jnp.sum(jnp.where(lo, 0.0, x2), axis=-1, keepdims=True)
                ms = jnp.where(lo, s_lo, s_hi) * (1.0 / WIN_HD)
                xn = xt * lax.rsqrt(ms + EPS) * gain_ref[:, sl]
                first = (lane % WIN_HD) < (WIN_HD // 2)
                sw = jnp.where(first, pltpu.roll(xn, LANES - WIN_HD // 2, 1), pltpu.roll(xn, WIN_HD // 2, 1))
            else:
                ms = jnp.mean(x2, axis=-1, keepdims=True)
                xn = xt * lax.rsqrt(ms + EPS) * gain_ref[:, sl]
                sw = pltpu.roll(xn, AX_HD // 2, 1)
            o_ref[0, :, sl] = (xn * cos + sw * sin).astype(BF16)
    else:
        bg_ref, qkv_ref, og_ref, gate_ref = rest
        nq = ML_HEADS * ML_DK // LANES
        nqkv = nq * 2 + ML_HEADS * ML_DV // LANES
        nog = nqkv + ML_HEADS * ML_DV // LANES

        def emit(t, xt):
            sl = slice(t * LANES, (t + 1) * LANES)
            if t < nq:
                qkv_ref[0, :, sl] = (xt * ML_DK ** -0.5).astype(BF16)
            elif t < nqkv:
                qkv_ref[0, :, sl] = xt.astype(BF16)
            elif t < nog:
                og_ref[0, :, (t - nqkv) * LANES:(t - nqkv + 1) * LANES] = xt
            else:
                g = xt + bg_ref[...]
                is_forget = (lane % (2 * ML_HEADS)) >= ML_HEADS
                log_sig = jnp.minimum(g, 0.0) - jnp.log1p(jnp.exp(-jnp.abs(g)))
                gate_ref[0] = jnp.where(is_forget, log_sig, g)

    n_out = w_ref.shape[1]
    starts = list(range(0, n_out, PROJ_CHUNK))

    def proj(c):
        return _dot(a, w_ref[:, starts[c]:min(starts[c] + PROJ_CHUNK, n_out)])

    y_next = proj(0)
    for c in range(len(starts)):
        y = y_next
        if c + 1 < len(starts):
            y_next = proj(c + 1)
        for i in range(y.shape[1] // LANES):
            emit(starts[c] // LANES + i, y[:, i * LANES:(i + 1) * LANES])


def _mod_spec(n_lat_tiles, ctx_row):
    return pl.BlockSpec((1, 6, D_MODEL), lambda b, j: (jnp.where(j < n_lat_tiles, b, ctx_row), 0, 0))


def _inproj_call(kind, h, mods, gain, w, extras, n_lat_tiles):
    b, t, d = h.shape
    n = w.shape[1]
    ctx_row = mods.shape[0] - 1
    in_specs = [pl.BlockSpec((1, TM, d), lambda bi, j: (bi, j, 0)),
                _mod_spec(n_lat_tiles, ctx_row),
                _resident((1, d)),
                _resident(w.shape)]
    if kind in ("win", "ax"):
        hgain, cos, sin = extras
        in_specs += [_resident(hgain.shape),
                     pl.BlockSpec((TM, LANES), lambda bi, j: (j, 0)),
                     pl.BlockSpec((TM, LANES), lambda bi, j: (j, 0))]
        if kind == "win":
            out_shape = jax.ShapeDtypeStruct((b, t, n), BF16)
            out_specs = pl.BlockSpec((1, TM, n), lambda bi, j: (bi, j, 0))
        else:
            n_qk = n - AX_KV * AX_HD
            out_shape = (jax.ShapeDtypeStruct((b, t, n_qk), BF16),
                         jax.ShapeDtypeStruct((b, AX_KV, t // TM, VT_ROWS, TM), BF16))
            out_specs = (pl.BlockSpec((1, TM, n_qk), lambda bi, j: (bi, j, 0)),
                         pl.BlockSpec((1, AX_KV, 1, VT_ROWS, TM), lambda bi, j: (bi, 0, j, 0, 0)))
        args = (hgain, cos, sin)
    else:
        (bg,) = extras
        nqkv = 2 * ML_HEADS * ML_DK + ML_HEADS * ML_DV
        nv = ML_HEADS * ML_DV
        in_specs += [_resident(bg.shape)]
        out_shape = (jax.ShapeDtypeStruct((b, t, nqkv), BF16),
                     jax.ShapeDtypeStruct((b, t, nv), F32),
                     jax.ShapeDtypeStruct((b, t, LANES), F32))
        out_specs = (pl.BlockSpec((1, TM, nqkv), lambda bi, j: (bi, j, 0)),
                     pl.BlockSpec((1, TM, nv), lambda bi, j: (bi, j, 0)),
                     pl.BlockSpec((1, TM, LANES), lambda bi, j: (bi, j, 0)))
        args = (bg,)
    return pl.pallas_call(
        functools.partial(_inproj_kernel, kind),
        grid=(b, t // TM),
        in_specs=in_specs, out_specs=out_specs, out_shape=out_shape,
        compiler_params=_cparams(("parallel", "arbitrary")),
        name="inproj_" + kind,
    )(h, mods, gain, w, *args)


def _win_heads(q_ref, k, v, bias, sink_ref, o_ref):
    tq = q_ref.shape[1]
    lane = lax.broadcasted_iota(jnp.int32, (tq, LANES), 1)
    lo = lane < WIN_HD
    group = WIN_HEADS // WIN_KV
    if bias is not None:
        bias = jnp.concatenate([bias] * group, axis=0)

    def scores(kv):
        kt, khalf = kv // 2, kv % 2
        qs = []
        for g in range(group):
            head = kv * group + g
            tile, half = head // 2, head % 2
            qt = q_ref[0, :, tile * LANES:(tile + 1) * LANES].astype(F32)
            qh = jnp.where(lo, qt, 0.0) if half == 0 else jnp.where(lo, 0.0, qt)
            if half != khalf:
                qh = pltpu.roll(qh, WIN_HD, 1)
            qs.append(qh.astype(BF16))
        return _dot_nt(jnp.concatenate(qs, axis=0), k[:, kt * LANES:(kt + 1) * LANES])

    s_next = scores(0)
    for kv in range(WIN_KV):
        kt, khalf = kv // 2, kv % 2
        s = s_next
        if kv + 1 < WIN_KV:
            s_next = scores(kv + 1)
        if bias is not None:
            s = s + bias
        sink = jnp.concatenate(
            [jnp.broadcast_to(sink_ref[kv * group + g:kv * group + g + 1, :][:, 0:1], (tq, 1)) for g in range(group)],
            axis=0)
        mx = jnp.maximum(jnp.max(s, axis=-1, keepdims=True), sink)
        e = jnp.exp(s - mx)
        den = jnp.sum(e, axis=-1, keepdims=True) + jnp.exp(sink - mx)
        o = _dot(e.astype(BF16), v[:, kt * LANES:(kt + 1) * LANES]) / den
        for pair in range(group // 2):
            halves = []
            for half in range(2):
                og = o[(2 * pair + half) * tq:(2 * pair + half + 1) * tq]
                halves.append(pltpu.roll(og, WIN_HD, 1) if half != khalf else og)
            tile = (kv * group) // 2 + pair
            o_ref[0, :, tile * LANES:(tile + 1) * LANES] = jnp.where(lo, halves[0], halves[1]).astype(BF16)


def _win_kernel(n_lat, q_ref, kp, kc, kn, kx, vp, vc, vn, vx, sink_ref, o_ref):
    i = pl.program_id(1)
    tq = q_ref.shape[1]
    n_ctx = kx.shape[1]

    @pl.when(i < n_lat)
    def _():
        nk = 3 * tq + n_ctx
        col = lax.broadcasted_iota(jnp.int32, (1, nk), 1)
        t = lax.broadcasted_iota(jnp.int32, (tq, nk), 1) - lax.broadcasted_iota(jnp.int32, (tq, nk), 0)
        big = 4 * nk
        t_min = jnp.where(col < tq, jnp.where(i > 0, 0, big), -big)
        t_max = jnp.where((col >= 2 * tq) & (col < 3 * tq), jnp.where(i < n_lat - 1, 2 * tq, -big), big)
        bias = jnp.where(t < t_min, NEG, 0.0) + jnp.where(t > t_max, NEG, 0.0)
        k = jnp.concatenate([kp[0], kc[0], kn[0], kx[0]], axis=0)
        v = jnp.concatenate([vp[0], vc[0], vn[0], vx[0]], axis=0)
        _win_heads(q_ref, k, v, bias, sink_ref, o_ref)

    @pl.when(i >= n_lat)
    def _():
        _win_heads(q_ref, kx[0], vx[0], None, sink_ref, o_ref)


def _win_call(qkv, sink_b, s_lat, n_ctx):
    b, t, _ = qkv.shape
    n_lat = s_lat // BLOCK
    nq = t // BLOCK
    dq = WIN_HEADS * WIN_HD
    kw = WIN_KV * WIN_HD
    kcol = dq // kw
    vcol = kcol + 1
    last = n_lat - 1
    xrow = s_lat // n_ctx

    def kv_specs(col):
        return [pl.BlockSpec((1, BLOCK, kw), lambda bi, i: (bi, jnp.clip(i - 1, 0, last), col)),
                pl.BlockSpec((1, BLOCK, kw), lambda bi, i: (bi, jnp.minimum(i, last), col)),
                pl.BlockSpec((1, BLOCK, kw), lambda bi, i: (bi, jnp.minimum(i + 1, last), col)),
                pl.BlockSpec((1, n_ctx, kw), lambda bi, i: (bi, xrow, col))]

    return pl.pallas_call(
        functools.partial(_win_kernel, n_lat),
        grid=(b, nq),
        in_specs=[pl.BlockSpec((1, BLOCK, dq), lambda bi, i: (bi, i, 0))] + kv_specs(kcol) + kv_specs(vcol)
                 + [_resident(sink_b.shape)],
        out_specs=pl.BlockSpec((1, BLOCK, dq), lambda bi, i: (bi, i, 0)),
        out_shape=jax.ShapeDtypeStruct((b, t, dq), BF16),
        compiler_params=_cparams(("parallel", "arbitrary")),
        name="win_attn",
    )(qkv, *([qkv] * 8), sink_b)


def _flash_kernel(n_lat_q, n_full, s_lat, n_ctx, q_ref, k_ref, vt_ref, o_ref, m_sc, acc_sc, s_sc):
    qi = pl.program_id(2)
    tq = q_ref.shape[1]
    group = AX_HEADS // AX_KV
    per = FLASH_CK // TM
    q = jnp.concatenate([q_ref[0, :, g * AX_HD:(g + 1) * AX_HD] for g in range(group)], axis=0)

    m_sc[...] = jnp.full(m_sc.shape, NEG, F32)
    acc_sc[...] = jnp.zeros(acc_sc.shape, F32)

    def scores(slot, start, size):
        s_sc[slot, :size, :] = _dot_nt(k_ref[0, pl.ds(start, size), :], q)

    def update(slot, block, size):
        s = s_sc[slot, :size, :]
        m_prev = m_sc[...]
        m_next = jnp.maximum(m_prev, jnp.max(s, axis=0, keepdims=True))
        p = jnp.exp2(s - m_next).astype(BF16)
        alpha = jnp.exp2(m_prev - m_next)
        pv = None
        for i in range(size // TM):
            part = _dot(vt_ref[0, 0, block + i], p[i * TM:(i + 1) * TM])
            pv = part if pv is None else pv + part
        acc_sc[...] = acc_sc[...] * alpha + pv
        m_sc[...] = m_next

    @pl.when(qi < n_lat_q)
    def _():
        def at(c):
            return pl.multiple_of(c * FLASH_CK, FLASH_CK)

        scores(0, 0, FLASH_CK)

        def body(i, carry):
            c = FLASH_UNROLL * i
            for u in range(FLASH_UNROLL):
                scores((u + 1) % 2, at(c + u + 1), FLASH_CK)
                update(u % 2, (c + u) * per, FLASH_CK)
            return carry

        n_iter = (n_full - 1) // FLASH_UNROLL
        lax.fori_loop(0, n_iter, body, 0)
        tail = [(c * FLASH_CK, c * per, FLASH_CK) for c in range(FLASH_UNROLL * n_iter, n_full)]
        tail.append((s_lat, s_lat // TM, n_ctx))
        for idx, (start, block, size) in enumerate(tail):
            if idx + 1 < len(tail):
                scores((idx + 1) % 2, tail[idx + 1][0], tail[idx + 1][2])
            update(idx % 2, block, size)

    @pl.when(qi >= n_lat_q)
    def _():
        scores(0, s_lat, n_ctx)
        update(0, s_lat // TM, n_ctx)

    acc = acc_sc[...]
    o = acc[:AX_HD] / acc[AX_HD:AX_HD + 1]
    for g in range(group):
        o_ref[0, :, g * AX_HD:(g + 1) * AX_HD] = o[:, g * tq:(g + 1) * tq].T.astype(BF16)


def _flash_call(qk, vt, s_lat, n_ctx):
    b, t, _ = qk.shape
    tq = FLASH_TQ
    group = AX_HEADS // AX_KV
    gw = group * AX_HD
    dq = AX_HEADS * AX_HD
    kcol = dq // AX_HD
    m_cols = group * tq
    return pl.pallas_call(
        functools.partial(_flash_kernel, s_lat // tq, s_lat // FLASH_CK, s_lat, n_ctx),
        grid=(b, AX_KV, t // tq),
        in_specs=[pl.BlockSpec((1, tq, gw), lambda bi, j, i: (bi, i, j)),
                  pl.BlockSpec((1, t, AX_HD), lambda bi, j, i: (bi, 0, kcol + j)),
                  pl.BlockSpec((1, 1) + vt.shape[2:], lambda bi, j, i: (bi, j, 0, 0, 0))],
        out_specs=pl.BlockSpec((1, tq, gw), lambda bi, j, i: (bi, i, j)),
        out_shape=jax.ShapeDtypeStruct((b, t, dq), BF16),
        scratch_shapes=[pltpu.VMEM((1, m_cols), F32), pltpu.VMEM((VT_ROWS, m_cols), F32),
                        pltpu.VMEM((2, FLASH_CK, m_cols), F32)],
        compiler_params=_cparams(("parallel", "parallel", "arbitrary")),
        name="flash_attn",
    )(qk, qk, vt)


def _mlstm_kernel(qf_ref, gf_ref, qb_ref, gb_ref, hf_ref, hb_ref, c_sc, n_sc, m_sc):
    step = pl.program_id(1)
    L = ML_CHUNK

    @pl.when(step == 0)
    def _():
        c_sc[...] = jnp.zeros(c_sc.shape, F32)
        n_sc[...] = jnp.zeros(n_sc.shape, F32)
        m_sc[...] = jnp.full(m_sc.shape, NEG, F32)

    r = lax.broadcasted_iota(jnp.int32, (L, L), 0)
    c = lax.broadcasted_iota(jnp.int32, (L, L), 1)
    lo = c <= r
    up = c >= r
    lo_f = lo.astype(F32)
    up_f = up.astype(F32)
    hi = lax.Precision.HIGHEST
    nqk = ML_HEADS * ML_DK

    dirs = ((qf_ref, gf_ref, hf_ref), (qb_ref, gb_ref, hb_ref))
    n_streams = 2 * qf_ref.shape[0]
    units = [(s, h) for s in range(n_streams) for h in range(ML_HEADS)]

    def qkv(s, h):
        x_ref, bb = dirs[s % 2][0], s // 2
        return (x_ref[bb, :, h * ML_DK:(h + 1) * ML_DK],
                x_ref[bb, :, nqk + h * ML_DK:nqk + (h + 1) * ML_DK],
                x_ref[bb, :, 2 * nqk + h * ML_DV:2 * nqk + (h + 1) * ML_DV])

    stats = []
    for s in range(n_streams):
        d = s % 2
        gates = dirs[d][1][s // 2]
        gates_t = gates.T
        tri_col, tri_row = (lo_f, up_f) if d == 0 else (up_f, lo_f)
        cum_col = jnp.dot(tri_col, gates, precision=hi, preferred_element_type=F32)
        cum_row = jnp.dot(gates_t, tri_row, precision=hi, preferred_element_type=F32)
        total = jnp.sum(gates, axis=0, keepdims=True)
        stats.append((gates, gates_t, cum_col, cum_row, total))

    s_mat, q_c = {}, {}
    for s, h in units:
        q, k, _ = qkv(s, h)
        s_mat[s, h] = _dot_nt(q, k)
        q_c[s, h] = _dot_nt(q, c_sc[s, h].astype(BF16))

    c_bar, n_bar, m_bar = {}, {}, {}
    for s, h in units:
        gates, _, cum_col, _, total = stats[s]
        ci = 2 * ML_HEADS * (s % 2) + h
        cf = ci + ML_HEADS
        _, k, v = qkv(s, h)
        a_col = total[:, cf:cf + 1] - cum_col[:, cf:cf + 1] + gates[:, ci:ci + 1]
        m_bar[s, h] = jnp.max(a_col, axis=0, keepdims=True)
        w_col = jnp.exp(a_col - m_bar[s, h])
        c_bar[s, h] = _dot_tn((v.astype(F32) * w_col).astype(BF16), k)
        n_bar[s, h] = jnp.sum(k.astype(F32) * w_col, axis=0, keepdims=True)

    num, den, m_ts = {}, {}, {}
    for s, h in units:
        _, gates_t, cum_col, cum_row, _ = stats[s]
        mask = lo if s % 2 == 0 else up
        ci = 2 * ML_HEADS * (s % 2) + h
        cf = ci + ML_HEADS
        q, _, v = qkv(s, h)
        n_prev = n_sc[s, h][0:1, :]
        m_prev = m_sc[s, h][0:1, 0:1]
        f_col = cum_col[:, cf:cf + 1]
        dmat = jnp.where(mask, f_col - cum_row[cf:cf + 1, :] + gates_t[ci:ci + 1, :], NEG)
        inter = f_col + m_prev
        m_t = jnp.maximum(inter, jnp.max(dmat, axis=-1, keepdims=True))
        w_inter = jnp.exp(inter - m_t)
        qk = s_mat[s, h] * jnp.exp(dmat - m_t)
        qn = jnp.sum(q.astype(F32) * n_prev, axis=-1, keepdims=True)
        num[s, h] = _dot(qk.astype(BF16), v) + w_inter * q_c[s, h]
        den[s, h] = jnp.sum(qk, axis=-1, keepdims=True) + w_inter * qn
        m_ts[s, h] = m_t

    for s, h in units:
        dirs[s % 2][2][s // 2, :, h * ML_DV:(h + 1) * ML_DV] = (
            num[s, h] / jnp.maximum(jnp.abs(den[s, h]), jnp.exp(-m_ts[s, h])))

    for s, h in units:
        total = stats[s][4]
        cf = 2 * ML_HEADS * (s % 2) + h + ML_HEADS
        f_tot = total[:, cf:cf + 1]
        n_prev = n_sc[s, h][0:1, :]
        m_prev = m_sc[s, h][0:1, 0:1]
        m_new = jnp.maximum(f_tot + m_prev, m_bar[s, h])
        decay = jnp.exp(f_tot + m_prev - m_new)
        inj = jnp.exp(m_bar[s, h] - m_new)
        c_sc[s, h] = decay * c_sc[s, h] + inj * c_bar[s, h]
        n_sc[s, h] = jnp.broadcast_to(decay * n_prev + inj * n_bar[s, h], (SUBLANES, ML_DK))
        m_sc[s, h] = jnp.broadcast_to(m_new, (SUBLANES, LANES))


def _mlstm_call(qkv, gates, s_lat):
    b, t, nx = qkv.shape
    nc = t // ML_CHUNK
    nlc = s_lat // ML_CHUNK
    nv = ML_HEADS * ML_DV

    def fwd(bi, s):
        return (bi, (s + nlc) % nc, 0)

    def bwd(bi, s):
        return (bi, nc - 1 - s, 0)

    nb = ML_BATCH if b % ML_BATCH == 0 else 1
    return pl.pallas_call(
        _mlstm_kernel,
        grid=(b // nb, nc),
        in_specs=[pl.BlockSpec((nb, ML_CHUNK, nx), fwd), pl.BlockSpec((nb, ML_CHUNK, LANES), fwd),
                  pl.BlockSpec((nb, ML_CHUNK, nx), bwd), pl.BlockSpec((nb, ML_CHUNK, LANES), bwd)],
        out_specs=(pl.BlockSpec((nb, ML_CHUNK, nv), fwd), pl.BlockSpec((nb, ML_CHUNK, nv), bwd)),
        out_shape=(jax.ShapeDtypeStruct((b, t, nv), F32), jax.ShapeDtypeStruct((b, t, nv), F32)),
        scratch_shapes=[pltpu.VMEM((2 * nb, ML_HEADS, ML_DV, ML_DK), F32),
                        pltpu.VMEM((2 * nb, ML_HEADS, SUBLANES, ML_DK), F32),
                        pltpu.VMEM((2 * nb, ML_HEADS, SUBLANES, LANES), F32)],
        compiler_params=_cparams(("parallel", "arbitrary")),
        name="mlstm",
    )(qkv, gates, qkv, gates)


def _ml_readout(hs, og, hn_ref):
    parts = []
    for h in range(ML_HEADS):
        sl = slice(h * ML_DV, (h + 1) * ML_DV)
        x = hs[:, sl]
        xn = x * lax.rsqrt(jnp.mean(x * x, axis=-1, keepdims=True) + EPS) * hn_ref[:, sl]
        parts.append((_sigmoid(og[:, sl]) * xn).astype(BF16))
    return jnp.concatenate(parts, axis=1)


def _convmlp_kernel(kind, pre, n_lat_tiles, n_all_tiles, *refs):
    n_pre = {None: 0, "attn": 4, "ml": 11}[pre]
    pre_refs = refs[:n_pre]
    hp_ref, h_ref, hn_ref, mod_ref, g_ref, w1_ref, cw_ref, w2_ref, o_ref, perm_sc = refs[n_pre:]
    j = pl.program_id(1)
    m = mod_ref[0]
    sh, sc, gt = (m[0:1], m[1:2], m[2:3]) if kind == "sc" else (m[3:4], m[4:5], m[5:6])
    x = jnp.concatenate([hp_ref[0], h_ref[0], hn_ref[0]], axis=0)
    rows = x.shape[0]
    if pre == "attn":
        yp_ref, y_ref, yn_ref, wo_ref = pre_refs
        y = jnp.concatenate([yp_ref[0], y_ref[0], yn_ref[0]], axis=0)
        skip = BF16_ROWS - HALO
        x = x + m[2:3] * _dot(y, wo_ref[...])[skip:skip + rows]
    elif pre == "ml":
        ext = [jnp.concatenate([pre_refs[3 * i][0], pre_refs[3 * i + 1][0], pre_refs[3 * i + 2][0]], axis=0)
               for i in range(3)]
        gain_ref, wo_ref = pre_refs[9:]
        x = x + m[2:3] * _dot(_ml_readout(ext[0] + ext[1], ext[2], gain_ref), wo_ref[...])
    a = _rms_mod(x, g_ref[...], sh, sc)
    seg_first = (j == 0) | (j == n_lat_tiles)
    seg_last = (j == n_lat_tiles - 1) | (j == n_all_tiles - 1)
    r = lax.broadcasted_iota(jnp.int32, (rows, 1), 0)
    dead = ((r < HALO) & seg_first) | ((r >= rows - HALO) & seg_last)
    a = jnp.where(dead, 0.0, a)

    pitch = rows // SUBLANES
    n_slab = a.shape[1] // LANES
    for k in range(n_slab):
        perm_sc[k] = a[:, k * LANES:(k + 1) * LANES]
    a = jnp.concatenate(
        [jnp.concatenate([perm_sc[k, pl.ds(r, SUBLANES, stride=pitch), :] for k in range(n_slab)], axis=1)
         for r in range(pitch)], axis=0).astype(BF16)

    def conv3(z, col):
        w = cw_ref[:, col:col + CONV_CHUNK]
        prev = jnp.concatenate([pltpu.roll(z[rows - SUBLANES:], 1, 0), z[:rows - SUBLANES]], axis=0)
        nxt = jnp.concatenate([z[SUBLANES:], pltpu.roll(z[:SUBLANES], SUBLANES - 1, 0)], axis=0)
        return prev * w[0:1] + z * w[1:2] + nxt * w[2:3]

    hidden = w2_ref.shape[0]
    parts = w1_ref.shape[1] // hidden

    def up(c):
        return [_dot(a, w1_ref[:, p * hidden + c * CONV_CHUNK:p * hidden + (c + 1) * CONV_CHUNK]) for p in range(parts)]

    n_chunks = hidden // CONV_CHUNK
    acc = None
    u_next = up(0)
    for c in range(n_chunks):
        u = u_next
        if c + 1 < n_chunks:
            u_next = up(c + 1)
        if kind == "sc":
            hid = u[0] * conv3(u[1] * u[2], c * CONV_CHUNK)
        else:
            gg = conv3(u[0], c * CONV_CHUNK)
            uu = conv3(u[1], hidden + c * CONV_CHUNK)
            hid = gg * _sigmoid(gg) * uu
        y = _dot(hid.astype(BF16), w2_ref[c * CONV_CHUNK:(c + 1) * CONV_CHUNK, :])
        acc = y if acc is None else acc + y

    y = gt * acc
    for r in range(pitch):
        for k in range(n_slab):
            perm_sc[k, pl.ds(r, SUBLANES, stride=pitch), :] = y[r * SUBLANES:(r + 1) * SUBLANES, k * LANES:(k + 1) * LANES]
    o_ref[0] = x[HALO:rows - HALO] + jnp.concatenate([perm_sc[k, HALO:rows - HALO, :] for k in range(n_slab)], axis=1)


def _halo_specs(t, width, halo):
    per = TM // halo
    last = t // halo - 1
    return [pl.BlockSpec((1, halo, width), lambda bi, j: (bi, jnp.maximum(j * per - 1, 0), 0)),
            pl.BlockSpec((1, TM, width), lambda bi, j: (bi, j, 0)),
            pl.BlockSpec((1, halo, width), lambda bi, j: (bi, jnp.minimum((j + 1) * per, last), 0))]


def _convmlp_call(kind, h, mods, gain, w1, cw, w2, n_lat_tiles, n_all_tiles, n_tiles, pre=None, pre_ins=()):
    b, t, d = h.shape
    ctx_row = mods.shape[0] - 1
    if pre == "attn":
        y, w_o = pre_ins
        pre_specs = _halo_specs(t, y.shape[2], BF16_ROWS) + [_resident(w_o.shape)]
        pre_args = (y, y, y, w_o)
    elif pre == "ml":
        hf, hb, og, hn, w_o = pre_ins
        pre_specs = _halo_specs(t, hf.shape[2], HALO) * 3 + [_resident(hn.shape), _resident(w_o.shape)]
        pre_args = (hf, hf, hf, hb, hb, hb, og, og, og, hn, w_o)
    else:
        pre_specs, pre_args = [], ()
    return pl.pallas_call(
        functools.partial(_convmlp_kernel, kind, pre, n_lat_tiles, n_all_tiles),
        grid=(b, n_tiles),
        in_specs=pre_specs + _halo_specs(t, d, HALO) + [
            _mod_spec(n_lat_tiles, ctx_row),
            _resident((1, d)), _resident(w1.shape), _resident(cw.shape), _resident(w2.shape)],
        out_specs=pl.BlockSpec((1, TM, d), lambda bi, j: (bi, j, 0)),
        out_shape=jax.ShapeDtypeStruct((b, n_tiles * TM, d), F32),
        scratch_shapes=[pltpu.VMEM((d // LANES, TM + 2 * HALO, LANES), F32)],
        compiler_params=_cparams(("parallel", "arbitrary")),
        name="convmlp_" + kind + ("_" + pre if pre else ""),
    )(*pre_args, h, h, h, mods, gain, w1, cw, w2)


def _rope_tables(s_lat, n_ctx, hd, reps):
    rows = s_lat // GRID_W
    row = np.repeat(np.arange(rows, dtype=np.float32), GRID_W)
    col = np.tile(np.arange(GRID_W, dtype=np.float32), rows)
    n_freq = hd // 4
    inv = jnp.power(ROPE_THETA, -jnp.arange(n_freq, dtype=F32) / n_freq)
    ang = jnp.concatenate([jnp.asarray(row)[:, None] * inv, jnp.asarray(col)[:, None] * inv], axis=-1)
    cos, sin = jnp.cos(ang), jnp.sin(ang)
    cos = jnp.tile(jnp.concatenate([cos, cos], axis=1), (1, reps))
    sin = jnp.tile(jnp.concatenate([-sin, sin], axis=1), (1, reps))
    cos = jnp.concatenate([cos, jnp.ones((n_ctx, LANES), F32)], axis=0)
    sin = jnp.concatenate([sin, jnp.zeros((n_ctx, LANES), F32)], axis=0)
    return cos, sin


def kernel(x, c, ctx, c_ctx, ada_w, ada_b, norm_mix, norm_ffn, ffn_w_up, ffn_conv, ffn_w_down, win_w_qkv, win_q_norm, win_k_norm, win_sink, win_w_o, sc_w_in, sc_conv, sc_w_out, ax_w_qkv, ax_q_norm, ax_k_norm, ax_w_o, ml_w_in, ml_b_gate, ml_h_norm, ml_w_out):
    b, s_lat, d = x.shape
    n_ctx = ctx.shape[1]
    assert d == D_MODEL and s_lat % FLASH_CK == 0 and n_ctx == TM and s_lat % GRID_W == 0
    n_lat_tiles = s_lat // TM
    n_all_tiles = n_lat_tiles + n_ctx // TM
    depth = ada_w.shape[0]

    h = jnp.concatenate([x, ctx], axis=1)
    pad_rows = -(b + 1) % SUBLANES
    cvec = jnp.concatenate([c, c_ctx[None], jnp.zeros((pad_rows, d), F32)], axis=0)
    mods = _ada_call(cvec, ada_w, ada_b)[:, :b + 1].reshape(depth, b + 1, 6, d)

    for i in range(depth):
        kind, j = i % N_MIXERS, i // N_MIXERS
        last = i == depth - 1
        n_tiles = n_lat_tiles if last else n_all_tiles
        m = mods[i]
        g_mix = norm_mix[i][None]
        pre, pre_ins = None, ()
        if kind == 0:
            gain = jnp.concatenate([jnp.tile(win_q_norm[j], WIN_HEADS) * WIN_HD ** -0.5,
                                    jnp.tile(win_k_norm[j], WIN_KV)])[None]
            cos, sin = _rope_tables(s_lat, n_ctx, WIN_HD, 2)
            qkv = _inproj_call("win", h, m, g_mix, win_w_qkv[j].astype(BF16), (gain, cos, sin), n_lat_tiles)
            sink_b = jnp.broadcast_to(win_sink[j][:, None], (WIN_HEADS, LANES))
            pre, pre_ins = "attn", (_win_call(qkv, sink_b, s_lat, n_ctx), win_w_o[j].astype(BF16))
        elif kind == 1:
            h = _convmlp_call("sc", h, m, g_mix, sc_w_in[j].astype(BF16), sc_conv[j], sc_w_out[j].astype(BF16),
                              n_lat_tiles, n_all_tiles, n_tiles)
        elif kind == 2:
            gain = jnp.concatenate([jnp.tile(ax_q_norm[j], AX_HEADS) * (AX_HD ** -0.5 * np.log2(np.e)),
                                    jnp.tile(ax_k_norm[j], AX_KV)])[None]
            cos, sin = _rope_tables(s_lat, n_ctx, AX_HD, 1)
            qk, vt = _inproj_call("ax", h, m, g_mix, ax_w_qkv[j].astype(BF16), (gain, cos, sin), n_lat_tiles)
            pre, pre_ins = "attn", (_flash_call(qk, vt, s_lat, n_ctx), ax_w_o[j].astype(BF16))
        else:
            w = jnp.concatenate([ml_w_in[j], jnp.zeros((d, LANES - 4 * ML_HEADS), F32)], axis=1).astype(BF16)
            bg = jnp.concatenate([ml_b_gate[j], jnp.zeros((LANES - 4 * ML_HEADS,), F32)])[None]
            qkv, og, gates = _inproj_call("ml", h, m, g_mix, w, (bg,), n_lat_tiles)
            hf, hb = _mlstm_call(qkv, gates, s_lat)
            hn = jnp.tile(ml_h_norm[j], ML_HEADS)[None]
            pre, pre_ins = "ml", (hf, hb, og, hn, ml_w_out[j].astype(BF16))
        h = _convmlp_call("ffn", h, m, norm_ffn[i][None], ffn_w_up[i].astype(BF16), ffn_conv[i], ffn_w_down[i].astype(BF16),
                          n_lat_tiles, n_all_tiles, n_tiles, pre, pre_ins)
    return h[:, :s_lat] if h.shape[1] != s_lat else h
```

```python
import functools

import numpy as np
import jax
import jax.numpy as jnp
from jax import lax
from jax.experimental import pallas as pl
from jax.experimental.pallas import tpu as pltpu

D_MODEL = 1024
DEPTH = 4
GRID_W = 64
N_MIXERS = 4
BLOCK = 128
WINDOW = 128
WIN_HEADS = 16
WIN_KV = 4
WIN_HD = 64
AX_HEADS = 8
AX_KV = 2
AX_HD = 128
ML_HEADS = 4
ML_DK = 128
ML_DV = 256
ML_CHUNK = 128
D_FF = 2816
ROPE_THETA = 10000.0
EPS = 1e-6
NEG = -1e30

F32 = jnp.float32
BF16 = jnp.bfloat16

LANES = 128
SUBLANES = 8
TM = 256
HALO = SUBLANES
CONV_CHUNK = 256
PROJ_CHUNK = 256
FLASH_TQ = 256
FLASH_CK = 512
FLASH_UNROLL = 8
ML_BATCH = 1
BF16_ROWS = 16
VT_ROWS = AX_HD + BF16_ROWS
VMEM_LIMIT = 56 * 1024 * 1024


def _cparams(sem):
    return pltpu.CompilerParams(dimension_semantics=sem, vmem_limit_bytes=VMEM_LIMIT)


def _resident(shape):
    nd = len(shape)
    return pl.BlockSpec(shape, lambda *_: (0,) * nd, pipeline_mode=pl.Buffered(1))


def _dot(a, b):
    return jnp.dot(a, b, preferred_element_type=F32)


def _dot_nt(a, b):
    return lax.dot_general(a, b, (((1,), (1,)), ((), ())), preferred_element_type=F32)


def _dot_tn(a, b):
    return lax.dot_general(a, b, (((0,), (0,)), ((), ())), preferred_element_type=F32)


def _sigmoid(x):
    return 1.0 / (1.0 + jnp.exp(-x))


def _rms_mod(x, g, shift, scale):
    y = x * lax.rsqrt(jnp.mean(x * x, axis=-1, keepdims=True) + EPS) * g
    return y * (1.0 + scale) + shift


def _ada_kernel(c_ref, w_ref, b_ref, o_ref):
    c = c_ref[...]
    o_ref[0] = _dot(c * _sigmoid(c), w_ref[0]) + b_ref[0]


def _ada_call(cvec, ada_w, ada_b):
    depth, d, n = ada_w.shape
    nb = n // 4
    return pl.pallas_call(
        _ada_kernel,
        grid=(depth, n // nb),
        in_specs=[pl.BlockSpec(cvec.shape, lambda l, j: (0, 0)),
                  pl.BlockSpec((1, d, nb), lambda l, j: (l, 0, j)),
                  pl.BlockSpec((1, 1, nb), lambda l, j: (l, 0, j))],
        out_specs=pl.BlockSpec((1, cvec.shape[0], nb), lambda l, j: (l, 0, j)),
        out_shape=jax.ShapeDtypeStruct((depth, cvec.shape[0], n), F32),
        compiler_params=_cparams(("arbitrary", "arbitrary")),
        name="ada",
    )(cvec, ada_w, ada_b.reshape(depth, 1, n))


def _tile_rows(n_lat_tiles, h_ref, ctx_ref):
    if ctx_ref is None:
        return h_ref[0]
    return jnp.where(pl.program_id(1) < n_lat_tiles, h_ref[0], ctx_ref[0])


def _inproj_kernel(kind, n_lat_tiles, split, h_ref, *refs):
    ctx_ref = refs[0] if split else None
    mod_ref, g_ref, w_ref, *rest = refs[1:] if split else refs
    m = mod_ref[0, 0]
    a = _rms_mod(_tile_rows(n_lat_tiles, h_ref, ctx_ref), g_ref[...], m[0:1], m[1:2]).astype(BF16)
    tm = a.shape[0]
    lane = lax.broadcasted_iota(jnp.int32, (tm, LANES), 1)

    if kind in ("win", "ax"):
        if kind == "win":
            gain_ref, cos_ref, sin_ref, o_ref = rest
        else:
            gain_ref, cos_ref, sin_ref, o_ref, vt_ref = rest
        cos = cos_ref[...]
        sin = sin_ref[...]
        n_rot = gain_ref.shape[1] // LANES

        def emit(t, xt):
            sl = slice(t * LANES, (t + 1) * LANES)
            if t >= n_rot:
                if kind == "win":
                    o_ref[0, :, sl] = xt.astype(BF16)
                else:
                    vt_ref[0, t - n_rot, 0, :AX_HD, :] = xt.T.astype(BF16)
                    vt_ref[0, t - n_rot, 0, AX_HD:, :] = jnp.ones((VT_ROWS - AX_HD, tm), BF16)
                return
            x2 = xt * xt
            if kind == "win":
                lo = lane < WIN_HD
                s_lo = jnp.sum(jnp.where(lo, x2, 0.0), axis=-1, keepdims=True)
                s_hi = jnp.sum(jnp.where(lo, 0.0, x2), axis=-1, keepdims=True)
                ms = jnp.where(lo, s_lo, s_hi) * (1.0 / WIN_HD)
                xn = xt * lax.rsqrt(ms + EPS) * gain_ref[:, sl]
                first = (lane % WIN_HD) < (WIN_HD // 2)
                sw = jnp.where(first, pltpu.roll(xn, LANES - WIN_HD // 2, 1), pltpu.roll(xn, WIN_HD // 2, 1))
            else:
                ms = jnp.mean(x2, axis=-1, keepdims=True)
                xn = xt * lax.rsqrt(ms + EPS) * gain_ref[:, sl]
                sw = pltpu.roll(xn, AX_HD // 2, 1)
            o_ref[0, :, sl] = (xn * cos + sw * sin).astype(BF16)
    else:
        bg_ref, qkv_ref, og_ref, gate_ref = rest
        nq = ML_HEADS * ML_DK // LANES
        nqkv = nq * 2 + ML_HEADS * ML_DV // LANES
        nog = nqkv + ML_HEADS * ML_DV // LANES

        def emit(t, xt):
            sl = slice(t * LANES, (t + 1) * LANES)
            if t < nq:
                qkv_ref[0, :, sl] = (xt * ML_DK ** -0.5).astype(BF16)
            elif t < nqkv:
                qkv_ref[0, :, sl] = xt.astype(BF16)
            elif t < nog:
                og_ref[0, :, (t - nqkv) * LANES:(t - nqkv + 1) * LANES] = xt
            else:
                g = xt + bg_ref[...]
                is_forget = (lane % (2 * ML_HEADS)) >= ML_HEADS
                log_sig = jnp.minimum(g, 0.0) - jnp.log1p(jnp.exp(-jnp.abs(g)))
                gate_ref[0] = jnp.where(is_forget, log_sig, g)

    n_out = w_ref.shape[1]
    starts = list(range(0, n_out, PROJ_CHUNK))

    def proj(c):
        return _dot(a, w_ref[:, starts[c]:min(starts[c] + PROJ_CHUNK, n_out)])

    y_next = proj(0)
    for c in range(len(starts)):
        y = y_next
        if c + 1 < len(starts):
            y_next = proj(c + 1)
        for i in range(y.shape[1] // LANES):
            emit(starts[c] // LANES + i, y[:, i * LANES:(i + 1) * LANES])


def _mod_spec(n_lat_tiles, ctx_row, layer):
    return pl.BlockSpec((1, 1, 6, D_MODEL), lambda b, j: (layer, jnp.where(j < n_lat_tiles, b, ctx_row), 0, 0))


def _stream_specs(h, ctx, n_lat_tiles):
    d = h.shape[2]
    if ctx is None:
        return h.shape[1], (h,), [pl.BlockSpec((1, TM, d), lambda bi, j: (bi, j, 0))]
    return (h.shape[1] + ctx.shape[1], (h, ctx),
            [pl.BlockSpec((1, TM, d), lambda bi, j: (bi, jnp.minimum(j, n_lat_tiles - 1), 0)),
             pl.BlockSpec((1, TM, d), lambda bi, j: (bi, 0, 0))])


def _inproj_call(kind, h, ctx, mods, layer, gain, w, extras, n_lat_tiles):
    b, _, d = h.shape
    n = w.shape[1]
    ctx_row = b
    t, streams, stream_specs = _stream_specs(h, ctx, n_lat_tiles)
    in_specs = stream_specs + [
                _mod_spec(n_lat_tiles, ctx_row, layer),
                _resident((1, d)),
                _resident(w.shape)]
    if kind in ("win", "ax"):
        hgain, cos, sin = extras
        in_specs += [_resident(hgain.shape),
                     pl.BlockSpec((TM, LANES), lambda bi, j: (j, 0)),
                     pl.BlockSpec((TM, LANES), lambda bi, j: (j, 0))]
        if kind == "win":
            out_shape = jax.ShapeDtypeStruct((b, t, n), BF16)
            out_specs = pl.BlockSpec((1, TM, n), lambda bi, j: (bi, j, 0))
        else:
            n_qk = n - AX_KV * AX_HD
            out_shape = (jax.ShapeDtypeStruct((b, t, n_qk), BF16),
                         jax.ShapeDtypeStruct((b, AX_KV, t // TM, VT_ROWS, TM), BF16))
            out_specs = (pl.BlockSpec((1, TM, n_qk), lambda bi, j: (bi, j, 0)),
                         pl.BlockSpec((1, AX_KV, 1, VT_ROWS, TM), lambda bi, j: (bi, 0, j, 0, 0)))
        args = (hgain, cos, sin)
    else:
        (bg,) = extras
        nqkv = 2 * ML_HEADS * ML_DK + ML_HEADS * ML_DV
        nv = ML_HEADS * ML_DV
        in_specs += [_resident(bg.shape)]
        out_shape = (jax.ShapeDtypeStruct((b, t, nqkv), BF16),
                     jax.ShapeDtypeStruct((b, t, nv), F32),
                     jax.ShapeDtypeStruct((b, t, LANES), F32))
        out_specs = (pl.BlockSpec((1, TM, nqkv), lambda bi, j: (bi, j, 0)),
                     pl.BlockSpec((1, TM, nv), lambda bi, j: (bi, j, 0)),
                     pl.BlockSpec((1, TM, LANES), lambda bi, j: (bi, j, 0)))
        args = (bg,)
    return pl.pallas_call(
        functools.partial(_inproj_kernel, kind, n_lat_tiles, ctx is not None),
        grid=(b, t // TM),
        in_specs=in_specs, out_specs=out_specs, out_shape=out_shape,
        compiler_params=_cparams(("parallel", "arbitrary")),
        name="inproj_" + kind,
    )(*streams, mods, gain, w, *args)


def _win_heads(q_ref, k, v, bias, sink_ref, o_ref):
    tq = q_ref.shape[1]
    lane = lax.broadcasted_iota(jnp.int32, (tq, LANES), 1)
    lo = lane < WIN_HD
    group = WIN_HEADS // WIN_KV
    if bias is not None:
        bias = jnp.concatenate([bias] * group, axis=0)

    def scores(kv):
        kt, khalf = kv // 2, kv % 2
        qs = []
        for g in range(group):
            head = kv * group + g
            tile, half = head // 2, head % 2
            qt = q_ref[0, :, tile * LANES:(tile + 1) * LANES].astype(F32)
            qh = jnp.where(lo, qt, 0.0) if half == 0 else jnp.where(lo, 0.0, qt)
            if half != khalf:
                qh = pltpu.roll(qh, WIN_HD, 1)
            qs.append(qh.astype(BF16))
        return _dot_nt(jnp.concatenate(qs, axis=0), k[:, kt * LANES:(kt + 1) * LANES])

    s_next = scores(0)
    for kv in range(WIN_KV):
        kt, khalf = kv // 2, kv % 2
        s = s_next
        if kv + 1 < WIN_KV:
            s_next = scores(kv + 1)
        if bias is not None:
            s = s + bias
        sink = jnp.concatenate(
            [jnp.broadcast_to(sink_ref[kv * group + g:kv * group + g + 1, :][:, 0:1], (tq, 1)) for g in range(group)],
            axis=0)
        mx = jnp.maximum(jnp.max(s, axis=-1, keepdims=True), sink)
        e = jnp.exp(s - mx)
        den = jnp.sum(e, axis=-1, keepdims=True) + jnp.exp(sink - mx)
        o = _dot(e.astype(BF16), v[:, kt * LANES:(kt + 1) * LANES]) / den
        for pair in range(group // 2):
            halves = []
            for half in range(2):
                og = o[(2 * pair + half) * tq:(2 * pair + half + 1) * tq]
                halves.append(pltpu.roll(og, WIN_HD, 1) if half != khalf else og)
            tile = (kv * group) // 2 + pair
            o_ref[0, :, tile * LANES:(tile + 1) * LANES] = jnp.where(lo, halves[0], halves[1]).astype(BF16)


def _win_kernel(n_lat, q_ref, kp, kc, kn, kx, vp, vc, vn, vx, sink_ref, o_ref):
    i = pl.program_id(1)
    tq = q_ref.shape[1]
    n_ctx = kx.shape[1]

    @pl.when(i < n_lat)
    def _():
        nk = 3 * tq + n_ctx
        col = lax.broadcasted_iota(jnp.int32, (1, nk), 1)
        t = lax.broadcasted_iota(jnp.int32, (tq, nk), 1) - lax.broadcasted_iota(jnp.int32, (tq, nk), 0)
        big = 4 * nk
        t_min = jnp.where(col < tq, jnp.where(i > 0, 0, big), -big)
        t_max = jnp.where((col >= 2 * tq) & (col < 3 * tq), jnp.where(i < n_lat - 1, 2 * tq, -big), big)
        bias = jnp.where(t < t_min, NEG, 0.0) + jnp.where(t > t_max, NEG, 0.0)
        k = jnp.concatenate([kp[0], kc[0], kn[0], kx[0]], axis=0)
        v = jnp.concatenate([vp[0], vc[0], vn[0], vx[0]], axis=0)
        _win_heads(q_ref, k, v, bias, sink_ref, o_ref)

    @pl.when(i >= n_lat)
    def _():
        _win_heads(q_ref, kx[0], vx[0], None, sink_ref, o_ref)


def _win_call(qkv, sink_b, s_lat, n_ctx):
    b, t, _ = qkv.shape
    n_lat = s_lat // BLOCK
    nq = t // BLOCK
    dq = WIN_HEADS * WIN_HD
    kw = WIN_KV * WIN_HD
    kcol = dq // kw
    vcol = kcol + 1
    last = n_lat - 1
    xrow = s_lat // n_ctx

    def kv_specs(col):
        return [pl.BlockSpec((1, BLOCK, kw), lambda bi, i: (bi, jnp.clip(i - 1, 0, last), col)),
                pl.BlockSpec((1, BLOCK, kw), lambda bi, i: (bi, jnp.minimum(i, last), col)),
                pl.BlockSpec((1, BLOCK, kw), lambda bi, i: (bi, jnp.minimum(i + 1, last), col)),
                pl.BlockSpec((1, n_ctx, kw), lambda bi, i: (bi, xrow, col))]

    return pl.pallas_call(
        functools.partial(_win_kernel, n_lat),
        grid=(b, nq),
        in_specs=[pl.BlockSpec((1, BLOCK, dq), lambda bi, i: (bi, i, 0))] + kv_specs(kcol) + kv_specs(vcol)
                 + [_resident(sink_b.shape)],
        out_specs=pl.BlockSpec((1, BLOCK, dq), lambda bi, i: (bi, i, 0)),
        out_shape=jax.ShapeDtypeStruct((b, t, dq), BF16),
        compiler_params=_cparams(("parallel", "arbitrary")),
        name="win_attn",
    )(qkv, *([qkv] * 8), sink_b)


def _flash_kernel(n_lat_q, n_full, s_lat, n_ctx, q_ref, k_ref, vt_ref, o_ref, m_sc, acc_sc, s_sc):
    qi = pl.program_id(2)
    tq = q_ref.shape[1]
    group = AX_HEADS // AX_KV
    per = FLASH_CK // TM
    q = jnp.concatenate([q_ref[0, :, g * AX_HD:(g + 1) * AX_HD] for g in range(group)], axis=0)

    m_sc[...] = jnp.full(m_sc.shape, NEG, F32)
    acc_sc[...] = jnp.zeros(acc_sc.shape, F32)

    def scores(slot, start, size):
        s_sc[slot, :size, :] = _dot_nt(k_ref[0, pl.ds(start, size), :], q)

    def update(slot, block, size):
        s = s_sc[slot, :size, :]
        m_prev = m_sc[...]
        m_next = jnp.maximum(m_prev, jnp.max(s, axis=0, keepdims=True))
        p = jnp.exp2(s - m_next).astype(BF16)
        alpha = jnp.exp2(m_prev - m_next)
        pv = None
        for i in range(size // TM):
            part = _dot(vt_ref[0, 0, block + i], p[i * TM:(i + 1) * TM])
            pv = part if pv is None else pv + part
        acc_sc[...] = acc_sc[...] * alpha + pv
        m_sc[...] = m_next

    @pl.when(qi < n_lat_q)
    def _():
        def at(c):
            return pl.multiple_of(c * FLASH_CK, FLASH_CK)

        scores(0, 0, FLASH_CK)

        def body(i, carry):
            c = FLASH_UNROLL * i
            for u in range(FLASH_UNROLL):
                scores((u + 1) % 2, at(c + u + 1), FLASH_CK)
                update(u % 2, (c + u) * per, FLASH_CK)
            return carry

        n_iter = (n_full - 1) // FLASH_UNROLL
        lax.fori_loop(0, n_iter, body, 0)
        tail = [(c * FLASH_CK, c * per, FLASH_CK) for c in range(FLASH_UNROLL * n_iter, n_full)]
        tail.append((s_lat, s_lat // TM, n_ctx))
        for idx, (start, block, size) in enumerate(tail):
            if idx + 1 < len(tail):
                scores((idx + 1) % 2, tail[idx + 1][0], tail[idx + 1][2])
            update(idx % 2, block, size)

    @pl.when(qi >= n_lat_q)
    def _():
        scores(0, s_lat, n_ctx)
        update(0, s_lat // TM, n_ctx)

    acc = acc_sc[...]
    o = acc[:AX_HD] / acc[AX_HD:AX_HD + 1]
    for g in range(group):
        o_ref[0, :, g * AX_HD:(g + 1) * AX_HD] = o[:, g * tq:(g + 1) * tq].T.astype(BF16)


def _flash_call(qk, vt, s_lat, n_ctx):
    b, t, _ = qk.shape
    tq = FLASH_TQ
    group = AX_HEADS // AX_KV
    gw = group * AX_HD
    dq = AX_HEADS * AX_HD
    kcol = dq // AX_HD
    m_cols = group * tq
    return pl.pallas_call(
        functools.partial(_flash_kernel, s_lat // tq, s_lat // FLASH_CK, s_lat, n_ctx),
        grid=(b, AX_KV, t // tq),
        in_specs=[pl.BlockSpec((1, tq, gw), lambda bi, j, i: (bi, i, j)),
                  pl.BlockSpec((1, t, AX_HD), lambda bi, j, i: (bi, 0, kcol + j)),
                  pl.BlockSpec((1, 1) + vt.shape[2:], lambda bi, j, i: (bi, j, 0, 0, 0))],
        out_specs=pl.BlockSpec((1, tq, gw), lambda bi, j, i: (bi, i, j)),
        out_shape=jax.ShapeDtypeStruct((b, t, dq), BF16),
        scratch_shapes=[pltpu.VMEM((1, m_cols), F32), pltpu.VMEM((VT_ROWS, m_cols), F32),
                        pltpu.VMEM((2, FLASH_CK, m_cols), F32)],
        compiler_params=_cparams(("parallel", "parallel", "arbitrary")),
        name="flash_attn",
    )(qk, qk, vt)


def _mlstm_kernel(qf_ref, gf_ref, qb_ref, gb_ref, hf_ref, hb_ref, c_sc, n_sc, m_sc):
    step = pl.program_id(1)
    L = ML_CHUNK

    @pl.when(step == 0)
    def _():
        c_sc[...] = jnp.zeros(c_sc.shape, F32)
        n_sc[...] = jnp.zeros(n_sc.shape, F32)
        m_sc[...] = jnp.full(m_sc.shape, NEG, F32)

    r = lax.broadcasted_iota(jnp.int32, (L, L), 0)
    c = lax.broadcasted_iota(jnp.int32, (L, L), 1)
    lo = c <= r
    up = c >= r
    lo_f = lo.astype(F32)
    up_f = up.astype(F32)
    hi = lax.Precision.HIGHEST
    nqk = ML_HEADS * ML_DK

    dirs = ((qf_ref, gf_ref, hf_ref), (qb_ref, gb_ref, hb_ref))
    n_streams = 2 * qf_ref.shape[0]
    units = [(s, h) for s in range(n_streams) for h in range(ML_HEADS)]

    def qkv(s, h):
        x_ref, bb = dirs[s % 2][0], s // 2
        return (x_ref[bb, :, h * ML_DK:(h + 1) * ML_DK],
                x_ref[bb, :, nqk + h * ML_DK:nqk + (h + 1) * ML_DK],
                x_ref[bb, :, 2 * nqk + h * ML_DV:2 * nqk + (h + 1) * ML_DV])

    stats = []
    for s in range(n_streams):
        d = s % 2
        gates = dirs[d][1][s // 2]
        gates_t = gates.T
        tri_col, tri_row = (lo_f, up_f) if d == 0 else (up_f, lo_f)
        cum_col = jnp.dot(tri_col, gates, precision=hi, preferred_element_type=F32)
        cum_row = jnp.dot(gates_t, tri_row, precision=hi, preferred_element_type=F32)
        total = jnp.sum(gates, axis=0, keepdims=True)
        stats.append((gates, gates_t, cum_col, cum_row, total))

    s_mat, q_c = {}, {}
    for s, h in units:
        q, k, _ = qkv(s, h)
        s_mat[s, h] = _dot_nt(q, k)
        q_c[s, h] = _dot_nt(q, c_sc[s, h].astype(BF16))

    c_bar, n_bar, m_bar = {}, {}, {}
    for s, h in units:
        gates, _, cum_col, _, total = stats[s]
        ci = 2 * ML_HEADS * (s % 2) + h
        cf = ci + ML_HEADS
        _, k, v = qkv(s, h)
        a_col = total[:, cf:cf + 1] - cum_col[:, cf:cf + 1] + gates[:, ci:ci + 1]
        m_bar[s, h] = jnp.max(a_col, axis=0, keepdims=True)
        w_col = jnp.exp(a_col - m_bar[s, h])
        c_bar[s, h] = _dot_tn((v.astype(F32) * w_col).astype(BF16), k)
        n_bar[s, h] = jnp.sum(k.astype(F32) * w_col, axis=0, keepdims=True)

    num, den, m_ts = {}, {}, {}
    for s, h in units:
        _, gates_t, cum_col, cum_row, _ = stats[s]
        mask = lo if s % 2 == 0 else up
        ci = 2 * ML_HEADS * (s % 2) + h
        cf = ci + ML_HEADS
        q, _, v = qkv(s, h)
        n_prev = n_sc[s, h][0:1, :]
        m_prev = m_sc[s, h][0:1, 0:1]
        f_col = cum_col[:, cf:cf + 1]
        dmat = jnp.where(mask, f_col - cum_row[cf:cf + 1, :] + gates_t[ci:ci + 1, :], NEG)
        inter = f_col + m_prev
        m_t = jnp.maximum(inter, jnp.max(dmat, axis=-1, keepdims=True))
        w_inter = jnp.exp(inter - m_t)
        qk = s_mat[s, h] * jnp.exp(dmat - m_t)
        qn = jnp.sum(q.astype(F32) * n_prev, axis=-1, keepdims=True)
        num[s, h] = _dot(qk.astype(BF16), v) + w_inter * q_c[s, h]
        den[s, h] = jnp.sum(qk, axis=-1, keepdims=True) + w_inter * qn
        m_ts[s, h] = m_t

    for s, h in units:
        dirs[s % 2][2][s // 2, :, h * ML_DV:(h + 1) * ML_DV] = (
            num[s, h] / jnp.maximum(jnp.abs(den[s, h]), jnp.exp(-m_ts[s, h])))

    for s, h in units:
        total = stats[s][4]
        cf = 2 * ML_HEADS * (s % 2) + h + ML_HEADS
        f_tot = total[:, cf:cf + 1]
        n_prev = n_sc[s, h][0:1, :]
        m_prev = m_sc[s, h][0:1, 0:1]
        m_new = jnp.maximum(f_tot + m_prev, m_bar[s, h])
        decay = jnp.exp(f_tot + m_prev - m_new)
        inj = jnp.exp(m_bar[s, h] - m_new)
        c_sc[s, h] = decay * c_sc[s, h] + inj * c_bar[s, h]
        n_sc[s, h] = jnp.broadcast_to(decay * n_prev + inj * n_bar[s, h], (SUBLANES, ML_DK))
        m_sc[s, h] = jnp.broadcast_to(m_new, (SUBLANES, LANES))


def _mlstm_call(qkv, gates, s_lat):
    b, t, nx = qkv.shape
    nc = t // ML_CHUNK
    nlc = s_lat // ML_CHUNK
    nv = ML_HEADS * ML_DV

    def fwd(bi, s):
        return (bi, (s + nlc) % nc, 0)

    def bwd(bi, s):
        return (bi, nc - 1 - s, 0)

    nb = ML_BATCH if b % ML_BATCH == 0 else 1
    return pl.pallas_call(
        _mlstm_kernel,
        grid=(b // nb, nc),
        in_specs=[pl.BlockSpec((nb, ML_CHUNK, nx), fwd), pl.BlockSpec((nb, ML_CHUNK, LANES), fwd),
                  pl.BlockSpec((nb, ML_CHUNK, nx), bwd), pl.BlockSpec((nb, ML_CHUNK, LANES), bwd)],
        out_specs=(pl.BlockSpec((nb, ML_CHUNK, nv), fwd), pl.BlockSpec((nb, ML_CHUNK, nv), bwd)),
        out_shape=(jax.ShapeDtypeStruct((b, t, nv), F32), jax.ShapeDtypeStruct((b, t, nv), F32)),
        scratch_shapes=[pltpu.VMEM((2 * nb, ML_HEADS, ML_DV, ML_DK), F32),
                        pltpu.VMEM((2 * nb, ML_HEADS, SUBLANES, ML_DK), F32),
                        pltpu.VMEM((2 * nb, ML_HEADS, SUBLANES, LANES), F32)],
        compiler_params=_cparams(("parallel", "arbitrary")),
        name="mlstm",
    )(qkv, gates, qkv, gates)


def _ml_readout(hs, og, hn_ref):
    parts = []
    for h in range(ML_HEADS):
        sl = slice(h * ML_DV, (h + 1) * ML_DV)
        x = hs[:, sl]
        xn = x * lax.rsqrt(jnp.mean(x * x, axis=-1, keepdims=True) + EPS) * hn_ref[:, sl]
        parts.append((_sigmoid(og[:, sl]) * xn).astype(BF16))
    return jnp.concatenate(parts, axis=1)


def _convmlp_kernel(kind, pre, split, n_lat_tiles, n_all_tiles, *refs):
    n_pre = {None: 0, "attn": 4, "ml": 11}[pre]
    pre_refs = refs[:n_pre]
    hp_ref, h_ref, hn_ref = refs[n_pre:n_pre + 3]
    ctx_ref = refs[n_pre + 3] if split else None
    mod_ref, g_ref, w1_ref, cw_ref, w2_ref, o_ref, perm_sc = refs[n_pre + 3 + split:]
    j = pl.program_id(1)
    m = mod_ref[0, 0]
    sh, sc, gt = (m[0:1], m[1:2], m[2:3]) if kind == "sc" else (m[3:4], m[4:5], m[5:6])
    x = jnp.concatenate([hp_ref[0], _tile_rows(n_lat_tiles, h_ref, ctx_ref), hn_ref[0]], axis=0)
    rows = x.shape[0]
    if pre == "attn":
        yp_ref, y_ref, yn_ref, wo_ref = pre_refs
        y = jnp.concatenate([yp_ref[0], y_ref[0], yn_ref[0]], axis=0)
        skip = BF16_ROWS - HALO
        x = x + m[2:3] * _dot(y, wo_ref[...])[skip:skip + rows]
    elif pre == "ml":
        ext = [jnp.concatenate([pre_refs[3 * i][0], pre_refs[3 * i + 1][0], pre_refs[3 * i + 2][0]], axis=0)
               for i in range(3)]
        gain_ref, wo_ref = pre_refs[9:]
        x = x + m[2:3] * _dot(_ml_readout(ext[0] + ext[1], ext[2], gain_ref), wo_ref[...])
    a = _rms_mod(x, g_ref[...], sh, sc)
    seg_first = (j == 0) | (j == n_lat_tiles)
    seg_last = (j == n_lat_tiles - 1) | (j == n_all_tiles - 1)
    r = lax.broadcasted_iota(jnp.int32, (rows, 1), 0)
    dead = ((r < HALO) & seg_first) | ((r >= rows - HALO) & seg_last)
    a = jnp.where(dead, 0.0, a)

    pitch = rows // SUBLANES
    n_slab = a.shape[1] // LANES
    for k in range(n_slab):
        perm_sc[k] = a[:, k * LANES:(k + 1) * LANES]
    a = jnp.concatenate(
        [jnp.concatenate([perm_sc[k, pl.ds(r, SUBLANES, stride=pitch), :] for k in range(n_slab)], axis=1)
         for r in range(pitch)], axis=0).astype(BF16)

    def conv3(z, col):
        w = cw_ref[:, col:col + CONV_CHUNK]
        prev = jnp.concatenate([pltpu.roll(z[rows - SUBLANES:], 1, 0), z[:rows - SUBLANES]], axis=0)
        nxt = jnp.concatenate([z[SUBLANES:], pltpu.roll(z[:SUBLANES], SUBLANES - 1, 0)], axis=0)
        return prev * w[0:1] + z * w[1:2] + nxt * w[2:3]

    hidden = w2_ref.shape[0]
    parts = w1_ref.shape[1] // hidden

    def up(c):
        return [_dot(a, w1_ref[:, p * hidden + c * CONV_CHUNK:p * hidden + (c + 1) * CONV_CHUNK]) for p in range(parts)]

    n_chunks = hidden // CONV_CHUNK
    acc = None
    u_next = up(0)
    for c in range(n_chunks):
        u = u_next
        if c + 1 < n_chunks:
            u_next = up(c + 1)
        if kind == "sc":
            hid = u[0] * conv3(u[1] * u[2], c * CONV_CHUNK)
        else:
            gg = conv3(u[0], c * CONV_CHUNK)
            uu = conv3(u[1], hidden + c * CONV_CHUNK)
            hid = gg * _sigmoid(gg) * uu
        y = _dot(hid.astype(BF16), w2_ref[c * CONV_CHUNK:(c + 1) * CONV_CHUNK, :])
        acc = y if acc is None else acc + y

    y = gt * acc
    for r in range(pitch):
        for k in range(n_slab):
            perm_sc[k, pl.ds(r, SUBLANES, stride=pitch), :] = y[r * SUBLANES:(r + 1) * SUBLANES, k * LANES:(k + 1) * LANES]
    o_ref[0] = x[HALO:rows - HALO] + jnp.concatenate([perm_sc[k, HALO:rows - HALO, :] for k in range(n_slab)], axis=1)


def _halo_specs(t, width, halo):
    per = TM // halo
    last = t // halo - 1
    last_tile = t // TM - 1
    return [pl.BlockSpec((1, halo, width), lambda bi, j: (bi, jnp.maximum(j * per - 1, 0), 0)),
            pl.BlockSpec((1, TM, width), lambda bi, j: (bi, jnp.minimum(j, last_tile), 0)),
            pl.BlockSpec((1, halo, width), lambda bi, j: (bi, jnp.minimum((j + 1) * per, last), 0))]


def _convmlp_call(kind, h, ctx, mods, layer, gain, w1, cw, w2, n_lat_tiles, n_all_tiles, n_tiles, pre=None, pre_ins=()):
    b, t_h, d = h.shape
    t = t_h + (0 if ctx is None else ctx.shape[1])
    ctx_row = b
    ctx_specs = [] if ctx is None else [pl.BlockSpec((1, TM, d), lambda bi, j: (bi, 0, 0))]
    ctx_args = () if ctx is None else (ctx,)
    if pre == "attn":
        y, w_o = pre_ins
        pre_specs = _halo_specs(t, y.shape[2], BF16_ROWS) + [_resident(w_o.shape)]
        pre_args = (y, y, y, w_o)
    elif pre == "ml":
        hf, hb, og, hn, w_o = pre_ins
        pre_specs = _halo_specs(t, hf.shape[2], HALO) * 3 + [_resident(hn.shape), _resident(w_o.shape)]
        pre_args = (hf, hf, hf, hb, hb, hb, og, og, og, hn, w_o)
    else:
        pre_specs, pre_args = [], ()
    return pl.pallas_call(
        functools.partial(_convmlp_kernel, kind, pre, ctx is not None, n_lat_tiles, n_all_tiles),
        grid=(b, n_tiles),
        in_specs=pre_specs + _halo_specs(t_h, d, HALO) + ctx_specs + [
            _mod_spec(n_lat_tiles, ctx_row, layer),
            _resident((1, d)), _resident(w1.shape), _resident(cw.shape), _resident(w2.shape)],
        out_specs=pl.BlockSpec((1, TM, d), lambda bi, j: (bi, j, 0)),
        out_shape=jax.ShapeDtypeStruct((b, n_tiles * TM, d), F32),
        scratch_shapes=[pltpu.VMEM((d // LANES, TM + 2 * HALO, LANES), F32)],
        compiler_params=_cparams(("parallel", "arbitrary")),
        name="convmlp_" + kind + ("_" + pre if pre else ""),
    )(*pre_args, h, h, h, *ctx_args, mods, gain, w1, cw, w2)


def _rope_tables(s_lat, n_ctx, hd, reps):
    rows = s_lat // GRID_W
    row = np.repeat(np.arange(rows, dtype=np.float32), GRID_W)
    col = np.tile(np.arange(GRID_W, dtype=np.float32), rows)
    n_freq = hd // 4
    inv = jnp.power(ROPE_THETA, -jnp.arange(n_freq, dtype=F32) / n_freq)
    ang = jnp.concatenate([jnp.asarray(row)[:, None] * inv, jnp.asarray(col)[:, None] * inv], axis=-1)
    cos, sin = jnp.cos(ang), jnp.sin(ang)
    cos = jnp.tile(jnp.concatenate([cos, cos], axis=1), (1, reps))
    sin = jnp.tile(jnp.concatenate([-sin, sin], axis=1), (1, reps))
    cos = jnp.concatenate([cos, jnp.ones((n_ctx, LANES), F32)], axis=0)
    sin = jnp.concatenate([sin, jnp.zeros((n_ctx, LANES), F32)], axis=0)
    return cos, sin


def kernel(x, c, ctx, c_ctx, ada_w, ada_b, norm_mix, norm_ffn, ffn_w_up, ffn_conv, ffn_w_down, win_w_qkv, win_q_norm, win_k_norm, win_sink, win_w_o, sc_w_in, sc_conv, sc_w_out, ax_w_qkv, ax_q_norm, ax_k_norm, ax_w_o, ml_w_in, ml_b_gate, ml_h_norm, ml_w_out):
    b, s_lat, d = x.shape
    n_ctx = ctx.shape[1]
    assert d == D_MODEL and s_lat % FLASH_CK == 0 and n_ctx == TM and s_lat % GRID_W == 0
    n_lat_tiles = s_lat // TM
    n_all_tiles = n_lat_tiles + n_ctx // TM
    depth = ada_w.shape[0]

    h, hc = x, ctx
    pad_rows = -(b + 1) % SUBLANES
    cvec = jnp.concatenate([c, c_ctx[None], jnp.zeros((pad_rows, d), F32)], axis=0)
    mods = _ada_call(cvec, ada_w, ada_b).reshape(depth, b + 1 + pad_rows, 6, d)

    for i in range(depth):
        kind, j = i % N_MIXERS, i // N_MIXERS
        last = i == depth - 1
        n_tiles = n_lat_tiles if last else n_all_tiles
        g_mix = norm_mix[i][None]
        pre, pre_ins = None, ()
        if kind == 0:
            gain = jnp.concatenate([jnp.tile(win_q_norm[j], WIN_HEADS) * WIN_HD ** -0.5,
                                    jnp.tile(win_k_norm[j], WIN_KV)])[None]
            cos, sin = _rope_tables(s_lat, n_ctx, WIN_HD, 2)
            qkv = _inproj_call("win", h, hc, mods, i, g_mix, win_w_qkv[j].astype(BF16), (gain, cos, sin), n_lat_tiles)
            sink_b = jnp.broadcast_to(win_sink[j][:, None], (WIN_HEADS, LANES))
            pre, pre_ins = "attn", (_win_call(qkv, sink_b, s_lat, n_ctx), win_w_o[j].astype(BF16))
        elif kind == 1:
            h = _convmlp_call("sc", h, hc, mods, i, g_mix, sc_w_in[j].astype(BF16), sc_conv[j], sc_w_out[j].astype(BF16),
                              n_lat_tiles, n_all_tiles, n_tiles)
            hc = None
        elif kind == 2:
            gain = jnp.concatenate([jnp.tile(ax_q_norm[j], AX_HEADS) * (AX_HD ** -0.5 * np.log2(np.e)),
                                    jnp.tile(ax_k_norm[j], AX_KV)])[None]
            cos, sin = _rope_tables(s_lat, n_ctx, AX_HD, 1)
            qk, vt = _inproj_call("ax", h, hc, mods, i, g_mix, ax_w_qkv[j].astype(BF16), (gain, cos, sin), n_lat_tiles)
            pre, pre_ins = "attn", (_flash_call(qk, vt, s_lat, n_ctx), ax_w_o[j].astype(BF16))
        else:
            w = jnp.concatenate([ml_w_in[j], jnp.zeros((d, LANES - 4 * ML_HEADS), F32)], axis=1).astype(BF16)
            bg = jnp.concatenate([ml_b_gate[j], jnp.zeros((LANES - 4 * ML_HEADS,), F32)])[None]
            qkv, og, gates = _inproj_call("ml", h, hc, mods, i, g_mix, w, (bg,), n_lat_tiles)
            hf, hb = _mlstm_call(qkv, gates, s_lat)
            hn = jnp.tile(ml_h_norm[j], ML_HEADS)[None]
            pre, pre_ins = "ml", (hf, hb, og, hn, ml_w_out[j].astype(BF16))
        h = _convmlp_call("ffn", h, hc, mods, i, norm_ffn[i][None], ffn_w_up[i].astype(BF16), ffn_conv[i], ffn_w_down[i].astype(BF16),
                          n_lat_tiles, n_all_tiles, n_tiles, pre, pre_ins)
        hc = None
    return h[:, :s_lat] if h.shape[1] != s_lat else h
```

```python
import functools

import numpy as np
import jax
import jax.numpy as jnp
from jax import lax
from jax.experimental import pallas as pl
from jax.experimental.pallas import tpu as pltpu

D_MODEL = 1024
DEPTH = 4
GRID_W = 64
N_MIXERS = 4
BLOCK = 128
WINDOW = 128
WIN_HEADS = 16
WIN_KV = 4
WIN_HD = 64
AX_HEADS = 8
AX_KV = 2
AX_HD = 128
ML_HEADS = 4
ML_DK = 128
ML_DV = 256
ML_CHUNK = 128
D_FF = 2816
ROPE_THETA = 10000.0
EPS = 1e-6
NEG = -1e30

F32 = jnp.float32
BF16 = jnp.bfloat16

LANES = 128
SUBLANES = 8
TM = 256
HALO = SUBLANES
CONV_CHUNK = 256
PROJ_CHUNK = 256
FLASH_TQ = 256
FLASH_CK = 512
FLASH_UNROLL = 8
ML_BATCH = 1
BF16_ROWS = 16
VT_ROWS = AX_HD + BF16_ROWS
VMEM_LIMIT = 56 * 1024 * 1024


def _cparams(sem):
    return pltpu.CompilerParams(dimension_semantics=sem, vmem_limit_bytes=VMEM_LIMIT)


def _resident(shape):
    nd = len(shape)
    return pl.BlockSpec(shape, lambda *_: (0,) * nd, pipeline_mode=pl.Buffered(1))


def _dot(a, b):
    return jnp.dot(a, b, preferred_element_type=F32)


def _dot_nt(a, b):
    return lax.dot_general(a, b, (((1,), (1,)), ((), ())), preferred_element_type=F32)


def _dot_tn(a, b):
    return lax.dot_general(a, b, (((0,), (0,)), ((), ())), preferred_element_type=F32)


def _sigmoid(x):
    return 1.0 / (1.0 + jnp.exp(-x))


def _rms_mod(x, g, shift, scale):
    y = x * lax.rsqrt(jnp.mean(x * x, axis=-1, keepdims=True) + EPS) * g
    return y * (1.0 + scale) + shift


def _ada_kernel(c_ref, w_ref, b_ref, o_ref):
    c = c_ref[...]
    o_ref[0] = _dot(c * _sigmoid(c), w_ref[0]) + b_ref[0]


def _ada_call(cvec, ada_w, ada_b):
    depth, d, n = ada_w.shape
    nb = n // 4
    return pl.pallas_call(
        _ada_kernel,
        grid=(depth, n // nb),
        in_specs=[pl.BlockSpec(cvec.shape, lambda l, j: (0, 0)),
                  pl.BlockSpec((1, d, nb), lambda l, j: (l, 0, j)),
                  pl.BlockSpec((1, 1, nb), lambda l, j: (l, 0, j))],
        out_specs=pl.BlockSpec((1, cvec.shape[0], nb), lambda l, j: (l, 0, j)),
        out_shape=jax.ShapeDtypeStruct((depth, cvec.shape[0], n), F32),
        compiler_params=_cparams(("arbitrary", "arbitrary")),
        name="ada",
    )(cvec, ada_w, ada_b.reshape(depth, 1, n))


def _tile_rows(n_lat_tiles, h_ref, ctx_ref):
    if ctx_ref is None:
        return h_ref[0]
    return jnp.where(pl.program_id(1) < n_lat_tiles, h_ref[0], ctx_ref[0])


def _inproj_kernel(kind, n_lat_tiles, split, h_ref, *refs):
    ctx_ref = refs[0] if split else None
    mod_ref, g_ref, w_ref, *rest = refs[1:] if split else refs
    m = mod_ref[0, 0]
    a = _rms_mod(_tile_rows(n_lat_tiles, h_ref, ctx_ref), g_ref[...], m[0:1], m[1:2]).astype(BF16)
    tm = a.shape[0]
    lane = lax.broadcasted_iota(jnp.int32, (tm, LANES), 1)

    if kind in ("win", "ax"):
        if kind == "win":
            gain_ref, cos_ref, sin_ref, o_ref = rest
        else:
            gain_ref, cos_ref, sin_ref, o_ref, vt_ref = rest
        cos = cos_ref[...]
        sin = sin_ref[...]
        n_rot = gain_ref.shape[1] // LANES
        head_sum = (lax.broadcasted_iota(jnp.int32, (LANES, LANES), 0) // WIN_HD
                    == lax.broadcasted_iota(jnp.int32, (LANES, LANES), 1) // WIN_HD).astype(BF16)

        def emit(t, xt):
            sl = slice(t * LANES, (t + 1) * LANES)
            if t >= n_rot:
                if kind == "win":
                    o_ref[0, :, sl] = xt.astype(BF16)
                else:
                    vt_ref[0, t - n_rot, 0, :AX_HD, :] = xt.T.astype(BF16)
                    vt_ref[0, t - n_rot, 0, AX_HD:, :] = jnp.ones((VT_ROWS - AX_HD, tm), BF16)
                return
            x2 = xt * xt
            if kind == "win":
                hi = x2.astype(BF16)
                lo = (x2 - hi.astype(F32)).astype(BF16)
                ms = (_dot(hi, head_sum) + _dot(lo, head_sum)) * (1.0 / WIN_HD)
                xn = xt * lax.rsqrt(ms + EPS) * gain_ref[:, sl]
                first = (lane % WIN_HD) < (WIN_HD // 2)
                sw = jnp.where(first, pltpu.roll(xn, LANES - WIN_HD // 2, 1), pltpu.roll(xn, WIN_HD // 2, 1))
            else:
                ms = jnp.mean(x2, axis=-1, keepdims=True)
                xn = xt * lax.rsqrt(ms + EPS) * gain_ref[:, sl]
                sw = pltpu.roll(xn, AX_HD // 2, 1)
            o_ref[0, :, sl] = (xn * cos + sw * sin).astype(BF16)
    else:
        bg_ref, qkv_ref, og_ref, gate_ref = rest
        nq = ML_HEADS * ML_DK // LANES
        nqkv = nq * 2 + ML_HEADS * ML_DV // LANES
        nog = nqkv + ML_HEADS * ML_DV // LANES

        def emit(t, xt):
            sl = slice(t * LANES, (t + 1) * LANES)
            if t < nq:
                qkv_ref[0, :, sl] = (xt * ML_DK ** -0.5).astype(BF16)
            elif t < nqkv:
                qkv_ref[0, :, sl] = xt.astype(BF16)
            elif t < nog:
                og_ref[0, :, (t - nqkv) * LANES:(t - nqkv + 1) * LANES] = xt
            else:
                g = xt + bg_ref[...]
                is_forget = (lane % (2 * ML_HEADS)) >= ML_HEADS
                log_sig = jnp.minimum(g, 0.0) - jnp.log1p(jnp.exp(-jnp.abs(g)))
                gate_ref[0] = jnp.where(is_forget, log_sig, g)

    n_out = w_ref.shape[1]
    starts = list(range(0, n_out, PROJ_CHUNK))

    def proj(c):
        return _dot(a, w_ref[:, starts[c]:min(starts[c] + PROJ_CHUNK, n_out)])

    y_next = proj(0)
    for c in range(len(starts)):
        y = y_next
        if c + 1 < len(starts):
            y_next = proj(c + 1)
        for i in range(y.shape[1] // LANES):
            emit(starts[c] // LANES + i, y[:, i * LANES:(i + 1) * LANES])


def _mod_spec(n_lat_tiles, ctx_row, layer):
    return pl.BlockSpec((1, 1, 6, D_MODEL), lambda b, j: (layer, jnp.where(j < n_lat_tiles, b, ctx_row), 0, 0))


def _stream_specs(h, ctx, n_lat_tiles):
    d = h.shape[2]
    if ctx is None:
        return h.shape[1], (h,), [pl.BlockSpec((1, TM, d), lambda bi, j: (bi, j, 0))]
    return (h.shape[1] + ctx.shape[1], (h, ctx),
            [pl.BlockSpec((1, TM, d), lambda bi, j: (bi, jnp.minimum(j, n_lat_tiles - 1), 0)),
             pl.BlockSpec((1, TM, d), lambda bi, j: (bi, 0, 0))])


def _inproj_call(kind, h, ctx, mods, layer, gain, w, extras, n_lat_tiles):
    b, _, d = h.shape
    n = w.shape[1]
    ctx_row = b
    t, streams, stream_specs = _stream_specs(h, ctx, n_lat_tiles)
    in_specs = stream_specs + [
                _mod_spec(n_lat_tiles, ctx_row, layer),
                _resident((1, d)),
                _resident(w.shape)]
    if kind in ("win", "ax"):
        hgain, cos, sin = extras
        in_specs += [_resident(hgain.shape),
                     pl.BlockSpec((TM, LANES), lambda bi, j: (j, 0)),
                     pl.BlockSpec((TM, LANES), lambda bi, j: (j, 0))]
        if kind == "win":
            out_shape = jax.ShapeDtypeStruct((b, t, n), BF16)
            out_specs = pl.BlockSpec((1, TM, n), lambda bi, j: (bi, j, 0))
        else:
            n_qk = n - AX_KV * AX_HD
            out_shape = (jax.ShapeDtypeStruct((b, t, n_qk), BF16),
                         jax.ShapeDtypeStruct((b, AX_KV, t // TM, VT_ROWS, TM), BF16))
            out_specs = (pl.BlockSpec((1, TM, n_qk), lambda bi, j: (bi, j, 0)),
                         pl.BlockSpec((1, AX_KV, 1, VT_ROWS, TM), lambda bi, j: (bi, 0, j, 0, 0)))
        args = (hgain, cos, sin)
    else:
        (bg,) = extras
        nqkv = 2 * ML_HEADS * ML_DK + ML_HEADS * ML_DV
        nv = ML_HEADS * ML_DV
        in_specs += [_resident(bg.shape)]
        out_shape = (jax.ShapeDtypeStruct((b, t, nqkv), BF16),
                     jax.ShapeDtypeStruct((b, t, nv), F32),
                     jax.ShapeDtypeStruct((b, t, LANES), F32))
        out_specs = (pl.BlockSpec((1, TM, nqkv), lambda bi, j: (bi, j, 0)),
                     pl.BlockSpec((1, TM, nv), lambda bi, j: (bi, j, 0)),
                     pl.BlockSpec((1, TM, LANES), lambda bi, j: (bi, j, 0)))
        args = (bg,)
    return pl.pallas_call(
        functools.partial(_inproj_kernel, kind, n_lat_tiles, ctx is not None),
        grid=(b, t // TM),
        in_specs=in_specs, out_specs=out_specs, out_shape=out_shape,
        compiler_params=_cparams(("parallel", "arbitrary")),
        name="inproj_" + kind,
    )(*streams, mods, gain, w, *args)


def _win_heads(q_ref, k, v, bias, sink_ref, o_ref):
    tq = q_ref.shape[1]
    lane = lax.broadcasted_iota(jnp.int32, (tq, LANES), 1)
    lo = lane < WIN_HD
    group = WIN_HEADS // WIN_KV
    if bias is not None:
        bias = jnp.concatenate([bias] * group, axis=0)

    def scores(kv):
        kt, khalf = kv // 2, kv % 2
        qs = []
        for g in range(group):
            head = kv * group + g
            tile, half = head // 2, head % 2
            qt = q_ref[0, :, tile * LANES:(tile + 1) * LANES].astype(F32)
            qh = jnp.where(lo, qt, 0.0) if half == 0 else jnp.where(lo, 0.0, qt)
            if half != khalf:
                qh = pltpu.roll(qh, WIN_HD, 1)
            qs.append(qh.astype(BF16))
        return _dot_nt(jnp.concatenate(qs, axis=0), k[:, kt * LANES:(kt + 1) * LANES])

    s_next = scores(0)
    for kv in range(WIN_KV):
        kt, khalf = kv // 2, kv % 2
        s = s_next
        if kv + 1 < WIN_KV:
            s_next = scores(kv + 1)
        if bias is not None:
            s = s + bias
        sink = jnp.concatenate(
            [jnp.broadcast_to(sink_ref[kv * group + g:kv * group + g + 1, :][:, 0:1], (tq, 1)) for g in range(group)],
            axis=0)
        mx = jnp.maximum(jnp.max(s, axis=-1, keepdims=True), sink)
        e = jnp.exp(s - mx)
        den = jnp.sum(e, axis=-1, keepdims=True) + jnp.exp(sink - mx)
        o = _dot(e.astype(BF16), v[:, kt * LANES:(kt + 1) * LANES]) / den
        for pair in range(group // 2):
            halves = []
            for half in range(2):
                og = o[(2 * pair + half) * tq:(2 * pair + half + 1) * tq]
                halves.append(pltpu.roll(og, WIN_HD, 1) if half != khalf else og)
            tile = (kv * group) // 2 + pair
            o_ref[0, :, tile * LANES:(tile + 1) * LANES] = jnp.where(lo, halves[0], halves[1]).astype(BF16)


def _win_kernel(n_lat, q_ref, kp, kc, kn, kx, vp, vc, vn, vx, sink_ref, o_ref):
    i = pl.program_id(1)
    tq = q_ref.shape[1]
    n_ctx = kx.shape[1]

    @pl.when(i < n_lat)
    def _():
        nk = 3 * tq + n_ctx
        col = lax.broadcasted_iota(jnp.int32, (1, nk), 1)
        t = lax.broadcasted_iota(jnp.int32, (tq, nk), 1) - lax.broadcasted_iota(jnp.int32, (tq, nk), 0)
        big = 4 * nk
        t_min = jnp.where(col < tq, jnp.where(i > 0, 0, big), -big)
        t_max = jnp.where((col >= 2 * tq) & (col < 3 * tq), jnp.where(i < n_lat - 1, 2 * tq, -big), big)
        bias = jnp.where(t < t_min, NEG, 0.0) + jnp.where(t > t_max, NEG, 0.0)
        k = jnp.concatenate([kp[0], kc[0], kn[0], kx[0]], axis=0)
        v = jnp.concatenate([vp[0], vc[0], vn[0], vx[0]], axis=0)
        _win_heads(q_ref, k, v, bias, sink_ref, o_ref)

    @pl.when(i >= n_lat)
    def _():
        _win_heads(q_ref, kx[0], vx[0], None, sink_ref, o_ref)


def _win_call(qkv, sink_b, s_lat, n_ctx):
    b, t, _ = qkv.shape
    n_lat = s_lat // BLOCK
    nq = t // BLOCK
    dq = WIN_HEADS * WIN_HD
    kw = WIN_KV * WIN_HD
    kcol = dq // kw
    vcol = kcol + 1
    last = n_lat - 1
    xrow = s_lat // n_ctx

    def kv_specs(col):
        return [pl.BlockSpec((1, BLOCK, kw), lambda bi, i: (bi, jnp.clip(i - 1, 0, last), col)),
                pl.BlockSpec((1, BLOCK, kw), lambda bi, i: (bi, jnp.minimum(i, last), col)),
                pl.BlockSpec((1, BLOCK, kw), lambda bi, i: (bi, jnp.minimum(i + 1, last), col)),
                pl.BlockSpec((1, n_ctx, kw), lambda bi, i: (bi, xrow, col))]

    return pl.pallas_call(
        functools.partial(_win_kernel, n_lat),
        grid=(b, nq),
        in_specs=[pl.BlockSpec((1, BLOCK, dq), lambda bi, i: (bi, i, 0))] + kv_specs(kcol) + kv_specs(vcol)
                 + [_resident(sink_b.shape)],
        out_specs=pl.BlockSpec((1, BLOCK, dq), lambda bi, i: (bi, i, 0)),
        out_shape=jax.ShapeDtypeStruct((b, t, dq), BF16),
        compiler_params=_cparams(("parallel", "arbitrary")),
        name="win_attn",
    )(qkv, *([qkv] * 8), sink_b)


def _flash_kernel(n_lat_q, n_full, s_lat, n_ctx, q_ref, k_ref, vt_ref, o_ref, m_sc, acc_sc, s_sc):
    qi = pl.program_id(2)
    tq = q_ref.shape[1]
    group = AX_HEADS // AX_KV
    per = FLASH_CK // TM
    q = jnp.concatenate([q_ref[0, :, g * AX_HD:(g + 1) * AX_HD] for g in range(group)], axis=0)

    m_sc[...] = jnp.full(m_sc.shape, NEG, F32)
    acc_sc[...] = jnp.zeros(acc_sc.shape, F32)

    def scores(slot, start, size):
        s_sc[slot, :size, :] = _dot_nt(k_ref[0, pl.ds(start, size), :], q)

    def update(slot, block, size):
        s = s_sc[slot, :size, :]
        m_prev = m_sc[...]
        m_next = jnp.maximum(m_prev, jnp.max(s, axis=0, keepdims=True))
        p = jnp.exp2(s - m_next).astype(BF16)
        alpha = jnp.exp2(m_prev - m_next)
        pv = None
        for i in range(size // TM):
            part = _dot(vt_ref[0, 0, block + i], p[i * TM:(i + 1) * TM])
            pv = part if pv is None else pv + part
        acc_sc[...] = acc_sc[...] * alpha + pv
        m_sc[...] = m_next

    @pl.when(qi < n_lat_q)
    def _():
        def at(c):
            return pl.multiple_of(c * FLASH_CK, FLASH_CK)

        scores(0, 0, FLASH_CK)

        def body(i, carry):
            c = FLASH_UNROLL * i
            for u in range(FLASH_UNROLL):
                scores((u + 1) % 2, at(c + u + 1), FLASH_CK)
                update(u % 2, (c + u) * per, FLASH_CK)
            return carry

        n_iter = (n_full - 1) // FLASH_UNROLL
        lax.fori_loop(0, n_iter, body, 0)
        tail = [(c * FLASH_CK, c * per, FLASH_CK) for c in range(FLASH_UNROLL * n_iter, n_full)]
        tail.append((s_lat, s_lat // TM, n_ctx))
        for idx, (start, block, size) in enumerate(tail):
            if idx + 1 < len(tail):
                scores((idx + 1) % 2, tail[idx + 1][0], tail[idx + 1][2])
            update(idx % 2, block, size)

    @pl.when(qi >= n_lat_q)
    def _():
        scores(0, s_lat, n_ctx)
        update(0, s_lat // TM, n_ctx)

    acc = acc_sc[...]
    o = acc[:AX_HD] / acc[AX_HD:AX_HD + 1]
    for g in range(group):
        o_ref[0, :, g * AX_HD:(g + 1) * AX_HD] = o[:, g * tq:(g + 1) * tq].T.astype(BF16)


def _flash_call(qk, vt, s_lat, n_ctx):
    b, t, _ = qk.shape
    tq = FLASH_TQ
    group = AX_HEADS // AX_KV
    gw = group * AX_HD
    dq = AX_HEADS * AX_HD
    kcol = dq // AX_HD
    m_cols = group * tq
    return pl.pallas_call(
        functools.partial(_flash_kernel, s_lat // tq, s_lat // FLASH_CK, s_lat, n_ctx),
        grid=(b, AX_KV, t // tq),
        in_specs=[pl.BlockSpec((1, tq, gw), lambda bi, j, i: (bi, i, j)),
                  pl.BlockSpec((1, t, AX_HD), lambda bi, j, i: (bi, 0, kcol + j)),
                  pl.BlockSpec((1, 1) + vt.shape[2:], lambda bi, j, i: (bi, j, 0, 0, 0))],
        out_specs=pl.BlockSpec((1, tq, gw), lambda bi, j, i: (bi, i, j)),
        out_shape=jax.ShapeDtypeStruct((b, t, dq), BF16),
        scratch_shapes=[pltpu.VMEM((1, m_cols), F32), pltpu.VMEM((VT_ROWS, m_cols), F32),
                        pltpu.VMEM((2, FLASH_CK, m_cols), F32)],
        compiler_params=_cparams(("parallel", "parallel", "arbitrary")),
        name="flash_attn",
    )(qk, qk, vt)


def _mlstm_kernel(qf_ref, gf_ref, qb_ref, gb_ref, hf_ref, hb_ref, c_sc, n_sc, m_sc):
    step = pl.program_id(1)
    L = ML_CHUNK

    @pl.when(step == 0)
    def _():
        c_sc[...] = jnp.zeros(c_sc.shape, F32)
        n_sc[...] = jnp.zeros(n_sc.shape, F32)
        m_sc[...] = jnp.full(m_sc.shape, NEG, F32)

    r = lax.broadcasted_iota(jnp.int32, (L, L), 0)
    c = lax.broadcasted_iota(jnp.int32, (L, L), 1)
    lo = c <= r
    up = c >= r
    lo_f = lo.astype(F32)
    up_f = up.astype(F32)
    hi = lax.Precision.HIGHEST
    nqk = ML_HEADS * ML_DK

    dirs = ((qf_ref, gf_ref, hf_ref), (qb_ref, gb_ref, hb_ref))
    n_streams = 2 * qf_ref.shape[0]
    units = [(s, h) for s in range(n_streams) for h in range(ML_HEADS)]

    def qkv(s, h):
        x_ref, bb = dirs[s % 2][0], s // 2
        return (x_ref[bb, :, h * ML_DK:(h + 1) * ML_DK],
                x_ref[bb, :, nqk + h * ML_DK:nqk + (h + 1) * ML_DK],
                x_ref[bb, :, 2 * nqk + h * ML_DV:2 * nqk + (h + 1) * ML_DV])

    stats = []
    for s in range(n_streams):
        d = s % 2
        gates = dirs[d][1][s // 2]
        gates_t = gates.T
        tri_col, tri_row = (lo_f, up_f) if d == 0 else (up_f, lo_f)
        cum_col = jnp.dot(tri_col, gates, precision=hi, preferred_element_type=F32)
        cum_row = jnp.dot(gates_t, tri_row, precision=hi, preferred_element_type=F32)
        total = jnp.sum(gates, axis=0, keepdims=True)
        stats.append((gates, gates_t, cum_col, cum_row, total))

    s_mat, q_c = {}, {}
    for s, h in units:
        q, k, _ = qkv(s, h)
        s_mat[s, h] = _dot_nt(q, k)
        q_c[s, h] = _dot_nt(q, c_sc[s, h].astype(BF16))

    c_bar, n_bar, m_bar = {}, {}, {}
    for s, h in units:
        gates, _, cum_col, _, total = stats[s]
        ci = 2 * ML_HEADS * (s % 2) + h
        cf = ci + ML_HEADS
        _, k, v = qkv(s, h)
        a_col = total[:, cf:cf + 1] - cum_col[:, cf:cf + 1] + gates[:, ci:ci + 1]
        m_bar[s, h] = jnp.max(a_col, axis=0, keepdims=True)
        w_col = jnp.exp(a_col - m_bar[s, h])
        c_bar[s, h] = _dot_tn((v.astype(F32) * w_col).astype(BF16), k)
        n_bar[s, h] = jnp.sum(k.astype(F32) * w_col, axis=0, keepdims=True)

    num, den, m_ts = {}, {}, {}
    for s, h in units:
        _, gates_t, cum_col, cum_row, _ = stats[s]
        mask = lo if s % 2 == 0 else up
        ci = 2 * ML_HEADS * (s % 2) + h
        cf = ci + ML_HEADS
        q, _, v = qkv(s, h)
        n_prev = n_sc[s, h][0:1, :]
        m_prev = m_sc[s, h][0:1, 0:1]
        f_col = cum_col[:, cf:cf + 1]
        dmat = jnp.where(mask, f_col - cum_row[cf:cf + 1, :] + gates_t[ci:ci + 1, :], NEG)
        inter = f_col + m_prev
        m_t = jnp.maximum(inter, jnp.max(dmat, axis=-1, keepdims=True))
        w_inter = jnp.exp(inter - m_t)
        qk = s_mat[s, h] * jnp.exp(dmat - m_t)
        qn = jnp.sum(q.astype(F32) * n_prev, axis=-1, keepdims=True)
        num[s, h] = _dot(qk.astype(BF16), v) + w_inter * q_c[s, h]
        den[s, h] = jnp.sum(qk, axis=-1, keepdims=True) + w_inter * qn
        m_ts[s, h] = m_t

    for s, h in units:
        dirs[s % 2][2][s // 2, :, h * ML_DV:(h + 1) * ML_DV] = (
            num[s, h] / jnp.maximum(jnp.abs(den[s, h]), jnp.exp(-m_ts[s, h])))

    for s, h in units:
        total = stats[s][4]
        cf = 2 * ML_HEADS * (s % 2) + h + ML_HEADS
        f_tot = total[:, cf:cf + 1]
        n_prev = n_sc[s, h][0:1, :]
        m_prev = m_sc[s, h][0:1, 0:1]
        m_new = jnp.maximum(f_tot + m_prev, m_bar[s, h])
        decay = jnp.exp(f_tot + m_prev - m_new)
        inj = jnp.exp(m_bar[s, h] - m_new)
        c_sc[s, h] = decay * c_sc[s, h] + inj * c_bar[s, h]
        n_sc[s, h] = jnp.broadcast_to(decay * n_prev + inj * n_bar[s, h], (SUBLANES, ML_DK))
        m_sc[s, h] = jnp.broadcast_to(m_new, (SUBLANES, LANES))


def _mlstm_call(qkv, gates, s_lat):
    b, t, nx = qkv.shape
    nc = t // ML_CHUNK
    nlc = s_lat // ML_CHUNK
    nv = ML_HEADS * ML_DV

    def fwd(bi, s):
        return (bi, (s + nlc) % nc, 0)

    def bwd(bi, s):
        return (bi, nc - 1 - s, 0)

    nb = ML_BATCH if b % ML_BATCH == 0 else 1
    return pl.pallas_call(
        _mlstm_kernel,
        grid=(b // nb, nc),
        in_specs=[pl.BlockSpec((nb, ML_CHUNK, nx), fwd), pl.BlockSpec((nb, ML_CHUNK, LANES), fwd),
                  pl.BlockSpec((nb, ML_CHUNK, nx), bwd), pl.BlockSpec((nb, ML_CHUNK, LANES), bwd)],
        out_specs=(pl.BlockSpec((nb, ML_CHUNK, nv), fwd), pl.BlockSpec((nb, ML_CHUNK, nv), bwd)),
        out_shape=(jax.ShapeDtypeStruct((b, t, nv), F32), jax.ShapeDtypeStruct((b, t, nv), F32)),
        scratch_shapes=[pltpu.VMEM((2 * nb, ML_HEADS, ML_DV, ML_DK), F32),
                        pltpu.VMEM((2 * nb, ML_HEADS, SUBLANES, ML_DK), F32),
                        pltpu.VMEM((2 * nb, ML_HEADS, SUBLANES, LANES), F32)],
        compiler_params=_cparams(("parallel", "arbitrary")),
        name="mlstm",
    )(qkv, gates, qkv, gates)


def _ml_readout(hs, og, hn_ref):
    parts = []
    for h in range(ML_HEADS):
        sl = slice(h * ML_DV, (h + 1) * ML_DV)
        x = hs[:, sl]
        xn = x * lax.rsqrt(jnp.mean(x * x, axis=-1, keepdims=True) + EPS) * hn_ref[:, sl]
        parts.append((_sigmoid(og[:, sl]) * xn).astype(BF16))
    return jnp.concatenate(parts, axis=1)


def _convmlp_kernel(kind, pre, split, n_lat_tiles, n_all_tiles, *refs):
    n_pre = {None: 0, "attn": 4, "ml": 11}[pre]
    pre_refs = refs[:n_pre]
    hp_ref, h_ref, hn_ref = refs[n_pre:n_pre + 3]
    ctx_ref = refs[n_pre + 3] if split else None
    mod_ref, g_ref, w1_ref, cw_ref, w2_ref, o_ref, perm_sc = refs[n_pre + 3 + split:]
    j = pl.program_id(1)
    m = mod_ref[0, 0]
    sh, sc, gt = (m[0:1], m[1:2], m[2:3]) if kind == "sc" else (m[3:4], m[4:5], m[5:6])
    x = jnp.concatenate([hp_ref[0], _tile_rows(n_lat_tiles, h_ref, ctx_ref), hn_ref[0]], axis=0)
    rows = x.shape[0]
    if pre == "attn":
        yp_ref, y_ref, yn_ref, wo_ref = pre_refs
        y = jnp.concatenate([yp_ref[0], y_ref[0], yn_ref[0]], axis=0)
        skip = BF16_ROWS - HALO
        x = x + m[2:3] * _dot(y, wo_ref[...])[skip:skip + rows]
    elif pre == "ml":
        ext = [jnp.concatenate([pre_refs[3 * i][0], pre_refs[3 * i + 1][0], pre_refs[3 * i + 2][0]], axis=0)
               for i in range(3)]
        gain_ref, wo_ref = pre_refs[9:]
        x = x + m[2:3] * _dot(_ml_readout(ext[0] + ext[1], ext[2], gain_ref), wo_ref[...])
    a = _rms_mod(x, g_ref[...], sh, sc)
    seg_first = (j == 0) | (j == n_lat_tiles)
    seg_last = (j == n_lat_tiles - 1) | (j == n_all_tiles - 1)
    r = lax.broadcasted_iota(jnp.int32, (rows, 1), 0)
    dead = ((r < HALO) & seg_first) | ((r >= rows - HALO) & seg_last)
    a = jnp.where(dead, 0.0, a)

    pitch = rows // SUBLANES
    n_slab = a.shape[1] // LANES
    for k in range(n_slab):
        perm_sc[k] = a[:, k * LANES:(k + 1) * LANES]
    a = jnp.concatenate(
        [jnp.concatenate([perm_sc[k, pl.ds(r, SUBLANES, stride=pitch), :] for k in range(n_slab)], axis=1)
         for r in range(pitch)], axis=0).astype(BF16)

    def conv3(z, col):
        w = cw_ref[:, col:col + CONV_CHUNK]
        prev = jnp.concatenate([pltpu.roll(z[rows - SUBLANES:], 1, 0), z[:rows - SUBLANES]], axis=0)
        nxt = jnp.concatenate([z[SUBLANES:], pltpu.roll(z[:SUBLANES], SUBLANES - 1, 0)], axis=0)
        return prev * w[0:1] + z * w[1:2] + nxt * w[2:3]

    hidden = w2_ref.shape[0]
    parts = w1_ref.shape[1] // hidden

    def up(c):
        return [_dot(a, w1_ref[:, p * hidden + c * CONV_CHUNK:p * hidden + (c + 1) * CONV_CHUNK]) for p in range(parts)]

    n_chunks = hidden // CONV_CHUNK
    acc = None
    u_next = up(0)
    for c in range(n_chunks):
        u = u_next
        if c + 1 < n_chunks:
            u_next = up(c + 1)
        if kind == "sc":
            hid = u[0] * conv3(u[1] * u[2], c * CONV_CHUNK)
        else:
            gg = conv3(u[0], c * CONV_CHUNK)
            uu = conv3(u[1], hidden + c * CONV_CHUNK)
            hid = gg * _sigmoid(gg) * uu
        y = _dot(hid.astype(BF16), w2_ref[c * CONV_CHUNK:(c + 1) * CONV_CHUNK, :])
        acc = y if acc is None else acc + y

    y = gt * acc
    for r in range(pitch):
        for k in range(n_slab):
            perm_sc[k, pl.ds(r, SUBLANES, stride=pitch), :] = y[r * SUBLANES:(r + 1) * SUBLANES, k * LANES:(k + 1) * LANES]
    o_ref[0] = x[HALO:rows - HALO] + jnp.concatenate([perm_sc[k, HALO:rows - HALO, :] for k in range(n_slab)], axis=1)


def _halo_specs(t, width, halo):
    per = TM // halo
    last = t // halo - 1
    last_tile = t // TM - 1
    return [pl.BlockSpec((1, halo, width), lambda bi, j: (bi, jnp.maximum(j * per - 1, 0), 0)),
            pl.BlockSpec((1, TM, width), lambda bi, j: (bi, jnp.minimum(j, last_tile), 0)),
            pl.BlockSpec((1, halo, width), lambda bi, j: (bi, jnp.minimum((j + 1) * per, last), 0))]


def _convmlp_call(kind, h, ctx, mods, layer, gain, w1, cw, w2, n_lat_tiles, n_all_tiles, n_tiles, pre=None, pre_ins=()):
    b, t_h, d = h.shape
    t = t_h + (0 if ctx is None else ctx.shape[1])
    ctx_row = b
    ctx_specs = [] if ctx is None else [pl.BlockSpec((1, TM, d), lambda bi, j: (bi, 0, 0))]
    ctx_args = () if ctx is None else (ctx,)
    if pre == "attn":
        y, w_o = pre_ins
        pre_specs = _halo_specs(t, y.shape[2], BF16_ROWS) + [_resident(w_o.shape)]
        pre_args = (y, y, y, w_o)
    elif pre == "ml":
        hf, hb, og, hn, w_o = pre_ins
        pre_specs = _halo_specs(t, hf.shape[2], HALO) * 3 + [_resident(hn.shape), _resident(w_o.shape)]
        pre_args = (hf, hf, hf, hb, hb, hb, og, og, og, hn, w_o)
    else:
        pre_specs, pre_args = [], ()
    return pl.pallas_call(
        functools.partial(_convmlp_kernel, kind, pre, ctx is not None, n_lat_tiles, n_all_tiles),
        grid=(b, n_tiles),
        in_specs=pre_specs + _halo_specs(t_h, d, HALO) + ctx_specs + [
            _mod_spec(n_lat_tiles, ctx_row, layer),
            _resident((1, d)), _resident(w1.shape), _resident(cw.shape), _resident(w2.shape)],
        out_specs=pl.BlockSpec((1, TM, d), lambda bi, j: (bi, j, 0)),
        out_shape=jax.ShapeDtypeStruct((b, n_tiles * TM, d), F32),
        scratch_shapes=[pltpu.VMEM((d // LANES, TM + 2 * HALO, LANES), F32)],
        compiler_params=_cparams(("parallel", "arbitrary")),
        name="convmlp_" + kind + ("_" + pre if pre else ""),
    )(*pre_args, h, h, h, *ctx_args, mods, gain, w1, cw, w2)


def _rope_tables(s_lat, n_ctx, hd, reps):
    rows = s_lat // GRID_W
    row = np.repeat(np.arange(rows, dtype=np.float32), GRID_W)
    col = np.tile(np.arange(GRID_W, dtype=np.float32), rows)
    n_freq = hd // 4
    inv = jnp.power(ROPE_THETA, -jnp.arange(n_freq, dtype=F32) / n_freq)
    ang = jnp.concatenate([jnp.asarray(row)[:, None] * inv, jnp.asarray(col)[:, None] * inv], axis=-1)
    cos, sin = jnp.cos(ang), jnp.sin(ang)
    cos = jnp.tile(jnp.concatenate([cos, cos], axis=1), (1, reps))
    sin = jnp.tile(jnp.concatenate([-sin, sin], axis=1), (1, reps))
    cos = jnp.concatenate([cos, jnp.ones((n_ctx, LANES), F32)], axis=0)
    sin = jnp.concatenate([sin, jnp.zeros((n_ctx, LANES), F32)], axis=0)
    return cos, sin


def kernel(x, c, ctx, c_ctx, ada_w, ada_b, norm_mix, norm_ffn, ffn_w_up, ffn_conv, ffn_w_down, win_w_qkv, win_q_norm, win_k_norm, win_sink, win_w_o, sc_w_in, sc_conv, sc_w_out, ax_w_qkv, ax_q_norm, ax_k_norm, ax_w_o, ml_w_in, ml_b_gate, ml_h_norm, ml_w_out):
    b, s_lat, d = x.shape
    n_ctx = ctx.shape[1]
    assert d == D_MODEL and s_lat % FLASH_CK == 0 and n_ctx == TM and s_lat % GRID_W == 0
    n_lat_tiles = s_lat // TM
    n_all_tiles = n_lat_tiles + n_ctx // TM
    depth = ada_w.shape[0]

    h, hc = x, ctx
    pad_rows = -(b + 1) % SUBLANES
    cvec = jnp.concatenate([c, c_ctx[None], jnp.zeros((pad_rows, d), F32)], axis=0)
    mods = _ada_call(cvec, ada_w, ada_b).reshape(depth, b + 1 + pad_rows, 6, d)

    for i in range(depth):
        kind, j = i % N_MIXERS, i // N_MIXERS
        last = i == depth - 1
        n_tiles = n_lat_tiles if last else n_all_tiles
        g_mix = norm_mix[i][None]
        pre, pre_ins = None, ()
        if kind == 0:
            gain = jnp.concatenate([jnp.tile(win_q_norm[j], WIN_HEADS) * WIN_HD ** -0.5,
                                    jnp.tile(win_k_norm[j], WIN_KV)])[None]
            cos, sin = _rope_tables(s_lat, n_ctx, WIN_HD, 2)
            qkv = _inproj_call("win", h, hc, mods, i, g_mix, win_w_qkv[j].astype(BF16), (gain, cos, sin), n_lat_tiles)
            sink_b = jnp.broadcast_to(win_sink[j][:, None], (WIN_HEADS, LANES))
            pre, pre_ins = "attn", (_win_call(qkv, sink_b, s_lat, n_ctx), win_w_o[j].astype(BF16))
        elif kind == 1:
            h = _convmlp_call("sc", h, hc, mods, i, g_mix, sc_w_in[j].astype(BF16), sc_conv[j], sc_w_out[j].astype(BF16),
                              n_lat_tiles, n_all_tiles, n_tiles)
            hc = None
        elif kind == 2:
            gain = jnp.concatenate([jnp.tile(ax_q_norm[j], AX_HEADS) * (AX_HD ** -0.5 * np.log2(np.e)),
                                    jnp.tile(ax_k_norm[j], AX_KV)])[None]
            cos, sin = _rope_tables(s_lat, n_ctx, AX_HD, 1)
            qk, vt = _inproj_call("ax", h, hc, mods, i, g_mix, ax_w_qkv[j].astype(BF16), (gain, cos, sin), n_lat_tiles)
            pre, pre_ins = "attn", (_flash_call(qk, vt, s_lat, n_ctx), ax_w_o[j].astype(BF16))
        else:
            w = jnp.concatenate([ml_w_in[j], jnp.zeros((d, LANES - 4 * ML_HEADS), F32)], axis=1).astype(BF16)
            bg = jnp.concatenate([ml_b_gate[j], jnp.zeros((LANES - 4 * ML_HEADS,), F32)])[None]
            qkv, og, gates = _inproj_call("ml", h, hc, mods, i, g_mix, w, (bg,), n_lat_tiles)
            hf, hb = _mlstm_call(qkv, gates, s_lat)
            hn = jnp.tile(ml_h_norm[j], ML_HEADS)[None]
            pre, pre_ins = "ml", (hf, hb, og, hn, ml_w_out[j].astype(BF16))
        h = _convmlp_call("ffn", h, hc, mods, i, norm_ffn[i][None], ffn_w_up[i].astype(BF16), ffn_conv[i], ffn_w_down[i].astype(BF16),
                          n_lat_tiles, n_all_tiles, n_tiles, pre, pre_ins)
        hc = None
    return h[:, :s_lat] if h.shape[1] != s_lat else h
```

```python
import functools

import numpy as np
import jax
import jax.numpy as jnp
from jax import lax
from jax.experimental import pallas as pl
from jax.experimental.pallas import tpu as pltpu

D_MODEL = 1024
DEPTH = 4
GRID_W = 64
N_MIXERS = 4
BLOCK = 128
WINDOW = 128
WIN_HEADS = 16
WIN_KV = 4
WIN_HD = 64
AX_HEADS = 8
AX_KV = 2
AX_HD = 128
ML_HEADS = 4
ML_DK = 128
ML_DV = 256
ML_CHUNK = 256
D_FF = 2816
ROPE_THETA = 10000.0
EPS = 1e-6
NEG = -1e30

F32 = jnp.float32
BF16 = jnp.bfloat16

LANES = 128
SUBLANES = 8
TM = 256
HALO = SUBLANES
CONV_CHUNK = 256
PROJ_CHUNK = 256
FLASH_TQ = 256
FLASH_CK = 512
FLASH_UNROLL = 8
ML_BATCH = 1
WIN_TQ = 128
BF16_ROWS = 16
VT_ROWS = AX_HD + BF16_ROWS
VMEM_LIMIT = 56 * 1024 * 1024


def _cparams(sem):
    return pltpu.CompilerParams(dimension_semantics=sem, vmem_limit_bytes=VMEM_LIMIT)


def _resident(shape):
    nd = len(shape)
    return pl.BlockSpec(shape, lambda *_: (0,) * nd, pipeline_mode=pl.Buffered(1))


def _dot(a, b):
    return jnp.dot(a, b, preferred_element_type=F32)


def _dot_nt(a, b):
    return lax.dot_general(a, b, (((1,), (1,)), ((), ())), preferred_element_type=F32)


def _dot_tn(a, b):
    return lax.dot_general(a, b, (((0,), (0,)), ((), ())), preferred_element_type=F32)


def _sigmoid(x):
    return 1.0 / (1.0 + jnp.exp(-x))


def _rms_mod(x, g, shift, scale):
    y = x * lax.rsqrt(jnp.mean(x * x, axis=-1, keepdims=True) + EPS) * g
    return y * (1.0 + scale) + shift


def _ada_kernel(c_ref, w_ref, b_ref, o_ref):
    c = c_ref[...]
    o_ref[0] = _dot(c * _sigmoid(c), w_ref[0]) + b_ref[0]


def _ada_call(cvec, ada_w, ada_b):
    depth, d, n = ada_w.shape
    nb = n // 4
    return pl.pallas_call(
        _ada_kernel,
        grid=(depth, n // nb),
        in_specs=[pl.BlockSpec(cvec.shape, lambda l, j: (0, 0)),
                  pl.BlockSpec((1, d, nb), lambda l, j: (l, 0, j)),
                  pl.BlockSpec((1, 1, nb), lambda l, j: (l, 0, j))],
        out_specs=pl.BlockSpec((1, cvec.shape[0], nb), lambda l, j: (l, 0, j)),
        out_shape=jax.ShapeDtypeStruct((depth, cvec.shape[0], n), F32),
        compiler_params=_cparams(("arbitrary", "arbitrary")),
        name="ada",
    )(cvec, ada_w, ada_b.reshape(depth, 1, n))


def _tile_rows(n_lat_tiles, h_ref, ctx_ref):
    if ctx_ref is None:
        return h_ref[0]
    return jnp.where(pl.program_id(1) < n_lat_tiles, h_ref[0], ctx_ref[0])


def _inproj_kernel(kind, n_lat_tiles, split, h_ref, *refs):
    ctx_ref = refs[0] if split else None
    mod_ref, g_ref, w_ref, *rest = refs[1:] if split else refs
    m = mod_ref[0, 0]
    a = _rms_mod(_tile_rows(n_lat_tiles, h_ref, ctx_ref), g_ref[...], m[0:1], m[1:2]).astype(BF16)
    tm = a.shape[0]
    lane = lax.broadcasted_iota(jnp.int32, (tm, LANES), 1)

    if kind in ("win", "ax"):
        if kind == "win":
            gain_ref, cos_ref, sin_ref, o_ref = rest
        else:
            gain_ref, cos_ref, sin_ref, o_ref, vt_ref = rest
        cos = cos_ref[...]
        sin = sin_ref[...]
        n_rot = gain_ref.shape[1] // LANES
        head_sum = (lax.broadcasted_iota(jnp.int32, (LANES, LANES), 0) // WIN_HD
                    == lax.broadcasted_iota(jnp.int32, (LANES, LANES), 1) // WIN_HD).astype(BF16)

        def emit(t, xt):
            sl = slice(t * LANES, (t + 1) * LANES)
            if t >= n_rot:
                if kind == "win":
                    o_ref[0, :, sl] = xt.astype(BF16)
                else:
                    vt_ref[0, t - n_rot, 0, :AX_HD, :] = xt.T.astype(BF16)
                    vt_ref[0, t - n_rot, 0, AX_HD:, :] = jnp.ones((VT_ROWS - AX_HD, tm), BF16)
                return
            x2 = xt * xt
            if kind == "win":
                hi = x2.astype(BF16)
                lo = (x2 - hi.astype(F32)).astype(BF16)
                ms = (_dot(hi, head_sum) + _dot(lo, head_sum)) * (1.0 / WIN_HD)
                xn = xt * lax.rsqrt(ms + EPS) * gain_ref[:, sl]
                first = (lane % WIN_HD) < (WIN_HD // 2)
                sw = jnp.where(first, pltpu.roll(xn, LANES - WIN_HD // 2, 1), pltpu.roll(xn, WIN_HD // 2, 1))
            else:
                ms = jnp.mean(x2, axis=-1, keepdims=True)
                xn = xt * lax.rsqrt(ms + EPS) * gain_ref[:, sl]
                sw = pltpu.roll(xn, AX_HD // 2, 1)
            o_ref[0, :, sl] = (xn * cos + sw * sin).astype(BF16)
    else:
        bg_ref, qkv_ref, og_ref, gate_ref = rest
        nq = ML_HEADS * ML_DK // LANES
        nqkv = nq * 2 + ML_HEADS * ML_DV // LANES
        nog = nqkv + ML_HEADS * ML_DV // LANES

        def emit(t, xt):
            sl = slice(t * LANES, (t + 1) * LANES)
            if t < nq:
                qkv_ref[0, :, sl] = (xt * ML_DK ** -0.5).astype(BF16)
            elif t < nqkv:
                qkv_ref[0, :, sl] = xt.astype(BF16)
            elif t < nog:
                og_ref[0, :, (t - nqkv) * LANES:(t - nqkv + 1) * LANES] = xt
            else:
                g = xt + bg_ref[...]
                is_forget = (lane % (2 * ML_HEADS)) >= ML_HEADS
                log_sig = jnp.minimum(g, 0.0) - jnp.log1p(jnp.exp(-jnp.abs(g)))
                gate_ref[0] = jnp.where(is_forget, log_sig, g)

    n_out = w_ref.shape[1]
    starts = list(range(0, n_out, PROJ_CHUNK))

    def proj(c):
        return _dot(a, w_ref[:, starts[c]:min(starts[c] + PROJ_CHUNK, n_out)])

    y_next = proj(0)
    for c in range(len(starts)):
        y = y_next
        if c + 1 < len(starts):
            y_next = proj(c + 1)
        for i in range(y.shape[1] // LANES):
            emit(starts[c] // LANES + i, y[:, i * LANES:(i + 1) * LANES])


def _mod_spec(n_lat_tiles, ctx_row, layer):
    return pl.BlockSpec((1, 1, 6, D_MODEL), lambda b, j: (layer, jnp.where(j < n_lat_tiles, b, ctx_row), 0, 0))


def _stream_specs(h, ctx, n_lat_tiles):
    d = h.shape[2]
    if ctx is None:
        return h.shape[1], (h,), [pl.BlockSpec((1, TM, d), lambda bi, j: (bi, j, 0))]
    return (h.shape[1] + ctx.shape[1], (h, ctx),
            [pl.BlockSpec((1, TM, d), lambda bi, j: (bi, jnp.minimum(j, n_lat_tiles - 1), 0)),
             pl.BlockSpec((1, TM, d), lambda bi, j: (bi, 0, 0))])


def _inproj_call(kind, h, ctx, mods, layer, gain, w, extras, n_lat_tiles):
    b, _, d = h.shape
    n = w.shape[1]
    ctx_row = b
    t, streams, stream_specs = _stream_specs(h, ctx, n_lat_tiles)
    in_specs = stream_specs + [
                _mod_spec(n_lat_tiles, ctx_row, layer),
                _resident((1, d)),
                _resident(w.shape)]
    if kind in ("win", "ax"):
        hgain, cos, sin = extras
        in_specs += [_resident(hgain.shape),
                     pl.BlockSpec((TM, LANES), lambda bi, j: (j, 0)),
                     pl.BlockSpec((TM, LANES), lambda bi, j: (j, 0))]
        if kind == "win":
            out_shape = jax.ShapeDtypeStruct((b, t, n), BF16)
            out_specs = pl.BlockSpec((1, TM, n), lambda bi, j: (bi, j, 0))
        else:
            n_qk = n - AX_KV * AX_HD
            out_shape = (jax.ShapeDtypeStruct((b, t, n_qk), BF16),
                         jax.ShapeDtypeStruct((b, AX_KV, t // TM, VT_ROWS, TM), BF16))
            out_specs = (pl.BlockSpec((1, TM, n_qk), lambda bi, j: (bi, j, 0)),
                         pl.BlockSpec((1, AX_KV, 1, VT_ROWS, TM), lambda bi, j: (bi, 0, j, 0, 0)))
        args = (hgain, cos, sin)
    else:
        (bg,) = extras
        nqkv = 2 * ML_HEADS * ML_DK + ML_HEADS * ML_DV
        nv = ML_HEADS * ML_DV
        in_specs += [_resident(bg.shape)]
        out_shape = (jax.ShapeDtypeStruct((b, t, nqkv), BF16),
                     jax.ShapeDtypeStruct((b, t, nv), F32),
                     jax.ShapeDtypeStruct((b, t, LANES), F32))
        out_specs = (pl.BlockSpec((1, TM, nqkv), lambda bi, j: (bi, j, 0)),
                     pl.BlockSpec((1, TM, nv), lambda bi, j: (bi, j, 0)),
                     pl.BlockSpec((1, TM, LANES), lambda bi, j: (bi, j, 0)))
        args = (bg,)
    return pl.pallas_call(
        functools.partial(_inproj_kernel, kind, n_lat_tiles, ctx is not None),
        grid=(b, t // TM),
        in_specs=in_specs, out_specs=out_specs, out_shape=out_shape,
        compiler_params=_cparams(("parallel", "arbitrary")),
        name="inproj_" + kind,
    )(*streams, mods, gain, w, *args)


def _win_heads(q_ref, k, v, bias, sink_ref, o_ref):
    tq = q_ref.shape[1]
    lane = lax.broadcasted_iota(jnp.int32, (tq, LANES), 1)
    lo = lane < WIN_HD
    group = WIN_HEADS // WIN_KV
    if bias is not None:
        bias = jnp.concatenate([bias] * group, axis=0)

    def scores(kv):
        kt, khalf = kv // 2, kv % 2
        qs = []
        for g in range(group):
            head = kv * group + g
            tile, half = head // 2, head % 2
            qt = q_ref[0, :, tile * LANES:(tile + 1) * LANES].astype(F32)
            qh = jnp.where(lo, qt, 0.0) if half == 0 else jnp.where(lo, 0.0, qt)
            if half != khalf:
                qh = pltpu.roll(qh, WIN_HD, 1)
            qs.append(qh.astype(BF16))
        return _dot_nt(jnp.concatenate(qs, axis=0), k[:, kt * LANES:(kt + 1) * LANES])

    s_next = scores(0)
    for kv in range(WIN_KV):
        kt, khalf = kv // 2, kv % 2
        s = s_next
        if kv + 1 < WIN_KV:
            s_next = scores(kv + 1)
        if bias is not None:
            s = s + bias
        sink = jnp.concatenate(
            [jnp.broadcast_to(sink_ref[kv * group + g:kv * group + g + 1, :][:, 0:1], (tq, 1)) for g in range(group)],
            axis=0)
        mx = jnp.maximum(jnp.max(s, axis=-1, keepdims=True), sink)
        e = jnp.exp(s - mx)
        den = jnp.sum(e, axis=-1, keepdims=True) + jnp.exp(sink - mx)
        o = _dot(e.astype(BF16), v[:, kt * LANES:(kt + 1) * LANES]) / den
        for pair in range(group // 2):
            halves = []
            for half in range(2):
                og = o[(2 * pair + half) * tq:(2 * pair + half + 1) * tq]
                halves.append(pltpu.roll(og, WIN_HD, 1) if half != khalf else og)
            tile = (kv * group) // 2 + pair
            o_ref[0, :, tile * LANES:(tile + 1) * LANES] = jnp.where(lo, halves[0], halves[1]).astype(BF16)


def _win_kernel(n_lat, q_ref, kp, kc, kn, kx, vp, vc, vn, vx, sink_ref, o_ref):
    i = pl.program_id(1)
    tq = q_ref.shape[1]
    n_ctx = kx.shape[1]

    @pl.when(i < n_lat)
    def _():
        w = WINDOW
        nk = tq + 2 * w + n_ctx
        col = lax.broadcasted_iota(jnp.int32, (1, nk), 1)
        t = lax.broadcasted_iota(jnp.int32, (tq, nk), 1) - lax.broadcasted_iota(jnp.int32, (tq, nk), 0)
        big = 4 * nk
        t_min = jnp.where(col < w, jnp.where(i > 0, 0, big), jnp.where(col < tq + 2 * w, 0, -big))
        t_max = jnp.where(col >= tq + 2 * w, big, jnp.where(col >= tq + w, jnp.where(i < n_lat - 1, 2 * w, -big), 2 * w))
        bias = jnp.where(t < t_min, NEG, 0.0) + jnp.where(t > t_max, NEG, 0.0)
        k = jnp.concatenate([kp[0], kc[0], kn[0], kx[0]], axis=0)
        v = jnp.concatenate([vp[0], vc[0], vn[0], vx[0]], axis=0)
        _win_heads(q_ref, k, v, bias, sink_ref, o_ref)

    @pl.when(i >= n_lat)
    def _():
        _win_heads(q_ref, kx[0], vx[0], None, sink_ref, o_ref)


def _win_call(qkv, sink_b, s_lat, n_ctx):
    b, t, _ = qkv.shape
    tq = WIN_TQ
    n_lat = s_lat // tq
    nq = t // tq
    dq = WIN_HEADS * WIN_HD
    kw = WIN_KV * WIN_HD
    kcol = dq // kw
    vcol = kcol + 1
    per = tq // WINDOW
    last_w = s_lat // WINDOW - 1
    xrow = s_lat // n_ctx

    def kv_specs(col):
        return [pl.BlockSpec((1, WINDOW, kw), lambda bi, i: (bi, jnp.clip(i * per - 1, 0, last_w), col)),
                pl.BlockSpec((1, tq, kw), lambda bi, i: (bi, jnp.minimum(i, n_lat - 1), col)),
                pl.BlockSpec((1, WINDOW, kw), lambda bi, i: (bi, jnp.minimum((i + 1) * per, last_w), col)),
                pl.BlockSpec((1, n_ctx, kw), lambda bi, i: (bi, xrow, col))]

    return pl.pallas_call(
        functools.partial(_win_kernel, n_lat),
        grid=(b, nq),
        in_specs=[pl.BlockSpec((1, tq, dq), lambda bi, i: (bi, i, 0))] + kv_specs(kcol) + kv_specs(vcol)
                 + [_resident(sink_b.shape)],
        out_specs=pl.BlockSpec((1, tq, dq), lambda bi, i: (bi, i, 0)),
        out_shape=jax.ShapeDtypeStruct((b, t, dq), BF16),
        compiler_params=_cparams(("parallel", "arbitrary")),
        name="win_attn",
    )(qkv, *([qkv] * 8), sink_b)


def _flash_kernel(n_lat_q, n_full, s_lat, n_ctx, q_ref, k_ref, vt_ref, o_ref, m_sc, acc_sc, s_sc):
    qi = pl.program_id(2)
    tq = q_ref.shape[1]
    group = AX_HEADS // AX_KV
    per = FLASH_CK // TM
    q = jnp.concatenate([q_ref[0, :, g * AX_HD:(g + 1) * AX_HD] for g in range(group)], axis=0)

    m_sc[...] = jnp.full(m_sc.shape, NEG, F32)
    acc_sc[...] = jnp.zeros(acc_sc.shape, F32)

    def scores(slot, start, size):
        s_sc[slot, :size, :] = _dot_nt(k_ref[0, pl.ds(start, size), :], q)

    def update(slot, block, size):
        s = s_sc[slot, :size, :]
        m_prev = m_sc[...]
        m_next = jnp.maximum(m_prev, jnp.max(s, axis=0, keepdims=True))
        p = jnp.exp2(s - m_next).astype(BF16)
        alpha = jnp.exp2(m_prev - m_next)
        pv = None
        for i in range(size // TM):
            part = _dot(vt_ref[0, 0, block + i], p[i * TM:(i + 1) * TM])
            pv = part if pv is None else pv + part
        acc_sc[...] = acc_sc[...] * alpha + pv
        m_sc[...] = m_next

    @pl.when(qi < n_lat_q)
    def _():
        def at(c):
            return pl.multiple_of(c * FLASH_CK, FLASH_CK)

        scores(0, 0, FLASH_CK)

        def body(i, carry):
            c = FLASH_UNROLL * i
            for u in range(FLASH_UNROLL):
                scores((u + 1) % 2, at(c + u + 1), FLASH_CK)
                update(u % 2, (c + u) * per, FLASH_CK)
            return carry

        n_iter = (n_full - 1) // FLASH_UNROLL
        lax.fori_loop(0, n_iter, body, 0)
        tail = [(c * FLASH_CK, c * per, FLASH_CK) for c in range(FLASH_UNROLL * n_iter, n_full)]
        tail.append((s_lat, s_lat // TM, n_ctx))
        for idx, (start, block, size) in enumerate(tail):
            if idx + 1 < len(tail):
                scores((idx + 1) % 2, tail[idx + 1][0], tail[idx + 1][2])
            update(idx % 2, block, size)

    @pl.when(qi >= n_lat_q)
    def _():
        scores(0, s_lat, n_ctx)
        update(0, s_lat // TM, n_ctx)

    acc = acc_sc[...]
    o = acc[:AX_HD] / acc[AX_HD:AX_HD + 1]
    for g in range(group):
        o_ref[0, :, g * AX_HD:(g + 1) * AX_HD] = o[:, g * tq:(g + 1) * tq].T.astype(BF16)


def _flash_call(qk, vt, s_lat, n_ctx):
    b, t, _ = qk.shape
    tq = FLASH_TQ
    group = AX_HEADS // AX_KV
    gw = group * AX_HD
    dq = AX_HEADS * AX_HD
    kcol = dq // AX_HD
    m_cols = group * tq
    return pl.pallas_call(
        functools.partial(_flash_kernel, s_lat // tq, s_lat // FLASH_CK, s_lat, n_ctx),
        grid=(b, AX_KV, t // tq),
        in_specs=[pl.BlockSpec((1, tq, gw), lambda bi, j, i: (bi, i, j)),
                  pl.BlockSpec((1, t, AX_HD), lambda bi, j, i: (bi, 0, kcol + j)),
                  pl.BlockSpec((1, 1) + vt.shape[2:], lambda bi, j, i: (bi, j, 0, 0, 0))],
        out_specs=pl.BlockSpec((1, tq, gw), lambda bi, j, i: (bi, i, j)),
        out_shape=jax.ShapeDtypeStruct((b, t, dq), BF16),
        scratch_shapes=[pltpu.VMEM((1, m_cols), F32), pltpu.VMEM((VT_ROWS, m_cols), F32),
                        pltpu.VMEM((2, FLASH_CK, m_cols), F32)],
        compiler_params=_cparams(("parallel", "parallel", "arbitrary")),
        name="flash_attn",
    )(qk, qk, vt)


def _mlstm_kernel(qf_ref, gf_ref, qb_ref, gb_ref, hf_ref, hb_ref, c_sc, n_sc, m_sc):
    step = pl.program_id(1)
    L = ML_CHUNK

    @pl.when(step == 0)
    def _():
        c_sc[...] = jnp.zeros(c_sc.shape, F32)
        n_sc[...] = jnp.zeros(n_sc.shape, F32)
        m_sc[...] = jnp.full(m_sc.shape, NEG, F32)

    r = lax.broadcasted_iota(jnp.int32, (L, L), 0)
    c = lax.broadcasted_iota(jnp.int32, (L, L), 1)
    lo = c <= r
    up = c >= r
    lo_f = lo.astype(F32)
    up_f = up.astype(F32)
    hi = lax.Precision.HIGHEST
    nqk = ML_HEADS * ML_DK

    dirs = ((qf_ref, gf_ref, hf_ref), (qb_ref, gb_ref, hb_ref))
    n_streams = 2 * qf_ref.shape[0]
    units = [(s, h) for s in range(n_streams) for h in range(ML_HEADS)]

    def qkv(s, h):
        x_ref, bb = dirs[s % 2][0], s // 2
        return (x_ref[bb, :, h * ML_DK:(h + 1) * ML_DK],
                x_ref[bb, :, nqk + h * ML_DK:nqk + (h + 1) * ML_DK],
                x_ref[bb, :, 2 * nqk + h * ML_DV:2 * nqk + (h + 1) * ML_DV])

    stats = []
    for s in range(n_streams):
        d = s % 2
        gates = dirs[d][1][s // 2]
        gates_t = gates.T
        tri_col, tri_row = (lo_f, up_f) if d == 0 else (up_f, lo_f)
        cum_col = jnp.dot(tri_col, gates, precision=hi, preferred_element_type=F32)
        cum_row = jnp.dot(gates_t, tri_row, precision=hi, preferred_element_type=F32)
        total = jnp.sum(gates, axis=0, keepdims=True)
        stats.append((gates, gates_t, cum_col, cum_row, total))

    s_mat, q_c = {}, {}
    for s, h in units:
        q, k, _ = qkv(s, h)
        s_mat[s, h] = _dot_nt(q, k)
        q_c[s, h] = _dot_nt(q, c_sc[s, h].astype(BF16))

    c_bar, n_bar, m_bar = {}, {}, {}
    for s, h in units:
        gates, _, cum_col, _, total = stats[s]
        ci = 2 * ML_HEADS * (s % 2) + h
        cf = ci + ML_HEADS
        _, k, v = qkv(s, h)
        a_col = total[:, cf:cf + 1] - cum_col[:, cf:cf + 1] + gates[:, ci:ci + 1]
        m_bar[s, h] = jnp.max(a_col, axis=0, keepdims=True)
        w_col = jnp.exp(a_col - m_bar[s, h])
        c_bar[s, h] = _dot_tn((v.astype(F32) * w_col).astype(BF16), k)
        n_bar[s, h] = jnp.sum(k.astype(F32) * w_col, axis=0, keepdims=True)

    num, den, m_ts = {}, {}, {}
    for s, h in units:
        _, gates_t, cum_col, cum_row, _ = stats[s]
        mask = lo if s % 2 == 0 else up
        ci = 2 * ML_HEADS * (s % 2) + h
        cf = ci + ML_HEADS
        q, _, v = qkv(s, h)
        n_prev = n_sc[s, h][0:1, :]
        m_prev = m_sc[s, h][0:1, 0:1]
        f_col = cum_col[:, cf:cf + 1]
        dmat = jnp.where(mask, f_col - cum_row[cf:cf + 1, :] + gates_t[ci:ci + 1, :], NEG)
        inter = f_col + m_prev
        m_t = jnp.maximum(inter, jnp.max(dmat, axis=-1, keepdims=True))
        w_inter = jnp.exp(inter - m_t)
        qk = s_mat[s, h] * jnp.exp(dmat - m_t)
        qn = jnp.sum(q.astype(F32) * n_prev, axis=-1, keepdims=True)
        num[s, h] = _dot(qk.astype(BF16), v) + w_inter * q_c[s, h]
        den[s, h] = jnp.sum(qk, axis=-1, keepdims=True) + w_inter * qn
        m_ts[s, h] = m_t

    for s, h in units:
        dirs[s % 2][2][s // 2, :, h * ML_DV:(h + 1) * ML_DV] = (
            num[s, h] / jnp.maximum(jnp.abs(den[s, h]), jnp.exp(-m_ts[s, h])))

    for s, h in units:
        total = stats[s][4]
        cf = 2 * ML_HEADS * (s % 2) + h + ML_HEADS
        f_tot = total[:, cf:cf + 1]
        n_prev = n_sc[s, h][0:1, :]
        m_prev = m_sc[s, h][0:1, 0:1]
        m_new = jnp.maximum(f_tot + m_prev, m_bar[s, h])
        decay = jnp.exp(f_tot + m_prev - m_new)
        inj = jnp.exp(m_bar[s, h] - m_new)
        c_sc[s, h] = decay * c_sc[s, h] + inj * c_bar[s, h]
        n_sc[s, h] = jnp.broadcast_to(decay * n_prev + inj * n_bar[s, h], (SUBLANES, ML_DK))
        m_sc[s, h] = jnp.broadcast_to(m_new, (SUBLANES, LANES))


def _mlstm_call(qkv, gates, s_lat):
    b, t, nx = qkv.shape
    nc = t // ML_CHUNK
    nlc = s_lat // ML_CHUNK
    nv = ML_HEADS * ML_DV

    def fwd(bi, s):
        return (bi, (s + nlc) % nc, 0)

    def bwd(bi, s):
        return (bi, nc - 1 - s, 0)

    nb = ML_BATCH if b % ML_BATCH == 0 else 1
    return pl.pallas_call(
        _mlstm_kernel,
        grid=(b // nb, nc),
        in_specs=[pl.BlockSpec((nb, ML_CHUNK, nx), fwd), pl.BlockSpec((nb, ML_CHUNK, LANES), fwd),
                  pl.BlockSpec((nb, ML_CHUNK, nx), bwd), pl.BlockSpec((nb, ML_CHUNK, LANES), bwd)],
        out_specs=(pl.BlockSpec((nb, ML_CHUNK, nv), fwd), pl.BlockSpec((nb, ML_CHUNK, nv), bwd)),
        out_shape=(jax.ShapeDtypeStruct((b, t, nv), F32), jax.ShapeDtypeStruct((b, t, nv), F32)),
        scratch_shapes=[pltpu.VMEM((2 * nb, ML_HEADS, ML_DV, ML_DK), F32),
                        pltpu.VMEM((2 * nb, ML_HEADS, SUBLANES, ML_DK), F32),
                        pltpu.VMEM((2 * nb, ML_HEADS, SUBLANES, LANES), F32)],
        compiler_params=_cparams(("parallel", "arbitrary")),
        name="mlstm",
    )(qkv, gates, qkv, gates)


def _ml_readout(hs, og, hn_ref):
    parts = []
    for h in range(ML_HEADS):
        sl = slice(h * ML_DV, (h + 1) * ML_DV)
        x = hs[:, sl]
        xn = x * lax.rsqrt(jnp.mean(x * x, axis=-1, keepdims=True) + EPS) * hn_ref[:, sl]
        parts.append((_sigmoid(og[:, sl]) * xn).astype(BF16))
    return jnp.concatenate(parts, axis=1)


def _convmlp_kernel(kind, pre, split, n_lat_tiles, n_all_tiles, *refs):
    n_pre = {None: 0, "attn": 4, "ml": 11}[pre]
    pre_refs = refs[:n_pre]
    hp_ref, h_ref, hn_ref = refs[n_pre:n_pre + 3]
    ctx_ref = refs[n_pre + 3] if split else None
    mod_ref, g_ref, w1_ref, cw_ref, w2_ref, o_ref, perm_sc = refs[n_pre + 3 + split:]
    j = pl.program_id(1)
    m = mod_ref[0, 0]
    sh, sc, gt = (m[0:1], m[1:2], m[2:3]) if kind == "sc" else (m[3:4], m[4:5], m[5:6])
    x = jnp.concatenate([hp_ref[0], _tile_rows(n_lat_tiles, h_ref, ctx_ref), hn_ref[0]], axis=0)
    rows = x.shape[0]
    if pre == "attn":
        yp_ref, y_ref, yn_ref, wo_ref = pre_refs
        y = jnp.concatenate([yp_ref[0], y_ref[0], yn_ref[0]], axis=0)
        skip = BF16_ROWS - HALO
        x = x + m[2:3] * _dot(y, wo_ref[...])[skip:skip + rows]
    elif pre == "ml":
        ext = [jnp.concatenate([pre_refs[3 * i][0], pre_refs[3 * i + 1][0], pre_refs[3 * i + 2][0]], axis=0)
               for i in range(3)]
        gain_ref, wo_ref = pre_refs[9:]
        x = x + m[2:3] * _dot(_ml_readout(ext[0] + ext[1], ext[2], gain_ref), wo_ref[...])
    a = _rms_mod(x, g_ref[...], sh, sc)
    seg_first = (j == 0) | (j == n_lat_tiles)
    seg_last = (j == n_lat_tiles - 1) | (j == n_all_tiles - 1)
    r = lax.broadcasted_iota(jnp.int32, (rows, 1), 0)
    dead = ((r < HALO) & seg_first) | ((r >= rows - HALO) & seg_last)
    a = jnp.where(dead, 0.0, a)

    pitch = rows // SUBLANES
    n_slab = a.shape[1] // LANES
    for k in range(n_slab):
        perm_sc[k] = a[:, k * LANES:(k + 1) * LANES]
    a = jnp.concatenate(
        [jnp.concatenate([perm_sc[k, pl.ds(r, SUBLANES, stride=pitch), :] for k in range(n_slab)], axis=1)
         for r in range(pitch)], axis=0).astype(BF16)

    def conv3(z, col):
        w = cw_ref[:, col:col + CONV_CHUNK]
        prev = jnp.concatenate([pltpu.roll(z[rows - SUBLANES:], 1, 0), z[:rows - SUBLANES]], axis=0)
        nxt = jnp.concatenate([z[SUBLANES:], pltpu.roll(z[:SUBLANES], SUBLANES - 1, 0)], axis=0)
        return prev * w[0:1] + z * w[1:2] + nxt * w[2:3]

    hidden = w2_ref.shape[0]
    parts = w1_ref.shape[1] // hidden

    def up(c):
        return [_dot(a, w1_ref[:, p * hidden + c * CONV_CHUNK:p * hidden + (c + 1) * CONV_CHUNK]) for p in range(parts)]

    n_chunks = hidden // CONV_CHUNK
    acc = None
    u_next = up(0)
    for c in range(n_chunks):
        u = u_next
        if c + 1 < n_chunks:
            u_next = up(c + 1)
        if kind == "sc":
            hid = u[0] * conv3(u[1] * u[2], c * CONV_CHUNK)
        else:
            gg = conv3(u[0], c * CONV_CHUNK)
            uu = conv3(u[1], hidden + c * CONV_CHUNK)
            hid = gg * _sigmoid(gg) * uu
        y = _dot(hid.astype(BF16), w2_ref[c * CONV_CHUNK:(c + 1) * CONV_CHUNK, :])
        acc = y if acc is None else acc + y

    y = gt * acc
    for r in range(pitch):
        for k in range(n_slab):
            perm_sc[k, pl.ds(r, SUBLANES, stride=pitch), :] = y[r * SUBLANES:(r + 1) * SUBLANES, k * LANES:(k + 1) * LANES]
    o_ref[0] = x[HALO:rows - HALO] + jnp.concatenate([perm_sc[k, HALO:rows - HALO, :] for k in range(n_slab)], axis=1)


def _halo_specs(t, width, halo):
    per = TM // halo
    last = t // halo - 1
    last_tile = t // TM - 1
    return [pl.BlockSpec((1, halo, width), lambda bi, j: (bi, jnp.maximum(j * per - 1, 0), 0)),
            pl.BlockSpec((1, TM, width), lambda bi, j: (bi, jnp.minimum(j, last_tile), 0)),
            pl.BlockSpec((1, halo, width), lambda bi, j: (bi, jnp.minimum((j + 1) * per, last), 0))]


def _convmlp_call(kind, h, ctx, mods, layer, gain, w1, cw, w2, n_lat_tiles, n_all_tiles, n_tiles, pre=None, pre_ins=()):
    b, t_h, d = h.shape
    t = t_h + (0 if ctx is None else ctx.shape[1])
    ctx_row = b
    ctx_specs = [] if ctx is None else [pl.BlockSpec((1, TM, d), lambda bi, j: (bi, 0, 0))]
    ctx_args = () if ctx is None else (ctx,)
    if pre == "attn":
        y, w_o = pre_ins
        pre_specs = _halo_specs(t, y.shape[2], BF16_ROWS) + [_resident(w_o.shape)]
        pre_args = (y, y, y, w_o)
    elif pre == "ml":
        hf, hb, og, hn, w_o = pre_ins
        pre_specs = _halo_specs(t, hf.shape[2], HALO) * 3 + [_resident(hn.shape), _resident(w_o.shape)]
        pre_args = (hf, hf, hf, hb, hb, hb, og, og, og, hn, w_o)
    else:
        pre_specs, pre_args = [], ()
    return pl.pallas_call(
        functools.partial(_convmlp_kernel, kind, pre, ctx is not None, n_lat_tiles, n_all_tiles),
        grid=(b, n_tiles),
        in_specs=pre_specs + _halo_specs(t_h, d, HALO) + ctx_specs + [
            _mod_spec(n_lat_tiles, ctx_row, layer),
            _resident((1, d)), _resident(w1.shape), _resident(cw.shape), _resident(w2.shape)],
        out_specs=pl.BlockSpec((1, TM, d), lambda bi, j: (bi, j, 0)),
        out_shape=jax.ShapeDtypeStruct((b, n_tiles * TM, d), F32),
        scratch_shapes=[pltpu.VMEM((d // LANES, TM + 2 * HALO, LANES), F32)],
        compiler_params=_cparams(("parallel", "arbitrary")),
        name="convmlp_" + kind + ("_" + pre if pre else ""),
    )(*pre_args, h, h, h, *ctx_args, mods, gain, w1, cw, w2)


def _rope_tables(s_lat, n_ctx, hd, reps):
    rows = s_lat // GRID_W
    row = np.repeat(np.arange(rows, dtype=np.float32), GRID_W)
    col = np.tile(np.arange(GRID_W, dtype=np.float32), rows)
    n_freq = hd // 4
    inv = jnp.power(ROPE_THETA, -jnp.arange(n_freq, dtype=F32) / n_freq)
    ang = jnp.concatenate([jnp.asarray(row)[:, None] * inv, jnp.asarray(col)[:, None] * inv], axis=-1)
    cos, sin = jnp.cos(ang), jnp.sin(ang)
    cos = jnp.tile(jnp.concatenate([cos, cos], axis=1), (1, reps))
    sin = jnp.tile(jnp.concatenate([-sin, sin], axis=1), (1, reps))
    cos = jnp.concatenate([cos, jnp.ones((n_ctx, LANES), F32)], axis=0)
    sin = jnp.concatenate([sin, jnp.zeros((n_ctx, LANES), F32)], axis=0)
    return cos, sin


def kernel(x, c, ctx, c_ctx, ada_w, ada_b, norm_mix, norm_ffn, ffn_w_up, ffn_conv, ffn_w_down, win_w_qkv, win_q_norm, win_k_norm, win_sink, win_w_o, sc_w_in, sc_conv, sc_w_out, ax_w_qkv, ax_q_norm, ax_k_norm, ax_w_o, ml_w_in, ml_b_gate, ml_h_norm, ml_w_out):
    b, s_lat, d = x.shape
    n_ctx = ctx.shape[1]
    assert d == D_MODEL and s_lat % FLASH_CK == 0 and n_ctx == TM and s_lat % GRID_W == 0
    n_lat_tiles = s_lat // TM
    n_all_tiles = n_lat_tiles + n_ctx // TM
    depth = ada_w.shape[0]

    h, hc = x, ctx
    pad_rows = -(b + 1) % SUBLANES
    cvec = jnp.concatenate([c, c_ctx[None], jnp.zeros((pad_rows, d), F32)], axis=0)
    mods = _ada_call(cvec, ada_w, ada_b).reshape(depth, b + 1 + pad_rows, 6, d)

    for i in range(depth):
        kind, j = i % N_MIXERS, i // N_MIXERS
        last = i == depth - 1
        n_tiles = n_lat_tiles if last else n_all_tiles
        g_mix = norm_mix[i][None]
        pre, pre_ins = None, ()
        if kind == 0:
            gain = jnp.concatenate([jnp.tile(win_q_norm[j], WIN_HEADS) * WIN_HD ** -0.5,
                                    jnp.tile(win_k_norm[j], WIN_KV)])[None]
            cos, sin = _rope_tables(s_lat, n_ctx, WIN_HD, 2)
            qkv = _inproj_call("win", h, hc, mods, i, g_mix, win_w_qkv[j].astype(BF16), (gain, cos, sin), n_lat_tiles)
            sink_b = jnp.broadcast_to(win_sink[j][:, None], (WIN_HEADS, LANES))
            pre, pre_ins = "attn", (_win_call(qkv, sink_b, s_lat, n_ctx), win_w_o[j].astype(BF16))
        elif kind == 1:
            h = _convmlp_call("sc", h, hc, mods, i, g_mix, sc_w_in[j].astype(BF16), sc_conv[j], sc_w_out[j].astype(BF16),
                              n_lat_tiles, n_all_tiles, n_tiles)
            hc = None
        elif kind == 2:
            gain = jnp.concatenate([jnp.tile(ax_q_norm[j], AX_HEADS) * (AX_HD ** -0.5 * np.log2(np.e)),
                                    jnp.tile(ax_k_norm[j], AX_KV)])[None]
            cos, sin = _rope_tables(s_lat, n_ctx, AX_HD, 1)
            qk, vt = _inproj_call("ax", h, hc, mods, i, g_mix, ax_w_qkv[j].astype(BF16), (gain, cos, sin), n_lat_tiles)
            pre, pre_ins = "attn", (_flash_call(qk, vt, s_lat, n_ctx), ax_w_o[j].astype(BF16))
        else:
            w = jnp.concatenate([ml_w_in[j], jnp.zeros((d, LANES - 4 * ML_HEADS), F32)], axis=1).astype(BF16)
            bg = jnp.concatenate([ml_b_gate[j], jnp.zeros((LANES - 4 * ML_HEADS,), F32)])[None]
            qkv, og, gates = _inproj_call("ml", h, hc, mods, i, g_mix, w, (bg,), n_lat_tiles)
            hf, hb = _mlstm_call(qkv, gates, s_lat)
            hn = jnp.tile(ml_h_norm[j], ML_HEADS)[None]
            pre, pre_ins = "ml", (hf, hb, og, hn, ml_w_out[j].astype(BF16))
        h = _convmlp_call("ffn", h, hc, mods, i, norm_ffn[i][None], ffn_w_up[i].astype(BF16), ffn_conv[i], ffn_w_down[i].astype(BF16),
                          n_lat_tiles, n_all_tiles, n_tiles, pre, pre_ins)
        hc = None
    return h[:, :s_lat] if h.shape[1] != s_lat else h
```

```python
import functools

import numpy as np
import jax
import jax.numpy as jnp
from jax import lax
from jax.experimental import pallas as pl
from jax.experimental.pallas import tpu as pltpu

D_MODEL = 1024
GRID_W = 64
N_MIXERS = 4
WINDOW = 128
WIN_HEADS = 16
WIN_KV = 4
WIN_HD = 64
AX_HEADS = 8
AX_KV = 2
AX_HD = 128
ML_HEADS = 4
ML_DK = 128
ML_DV = 256
ML_CHUNK = 256
ROPE_THETA = 10000.0
EPS = 1e-6
NEG = -1e30

F32 = jnp.float32
BF16 = jnp.bfloat16

LANES = 128
SUBLANES = 8
TM = 256
HALO = SUBLANES
CONV_CHUNK = 256
PROJ_CHUNK = 256
FLASH_TQ = 256
FLASH_CK = 512
FLASH_UNROLL = 8
WIN_TQ = 128
BF16_ROWS = 16
VT_ROWS = AX_HD + BF16_ROWS
V7X_VMEM_BYTES = 64 * 1024 * 1024
VMEM_LIMIT = V7X_VMEM_BYTES * 7 // 8


def _cparams(sem):
    return pltpu.CompilerParams(dimension_semantics=sem, vmem_limit_bytes=VMEM_LIMIT)


def _resident(shape):
    nd = len(shape)
    return pl.BlockSpec(shape, lambda *_: (0,) * nd, pipeline_mode=pl.Buffered(1))


def _dot(a, b):
    return jnp.dot(a, b, preferred_element_type=F32)


def _dot_nt(a, b):
    return lax.dot_general(a, b, (((1,), (1,)), ((), ())), preferred_element_type=F32)


def _dot_tn(a, b):
    return lax.dot_general(a, b, (((0,), (0,)), ((), ())), preferred_element_type=F32)


def _sigmoid(x):
    return 1.0 / (1.0 + jnp.exp(-x))


def _rms_mod(x, g, shift, scale):
    y = x * lax.rsqrt(jnp.mean(x * x, axis=-1, keepdims=True) + EPS) * g
    return y * (1.0 + scale) + shift


def _ada_kernel(c_ref, w_ref, b_ref, o_ref):
    c = c_ref[...]
    o_ref[0] = _dot(c * _sigmoid(c), w_ref[0]) + b_ref[0]


def _ada_call(cvec, ada_w, ada_b):
    depth, d, n = ada_w.shape
    nb = n // 4
    return pl.pallas_call(
        _ada_kernel,
        grid=(depth, n // nb),
        in_specs=[pl.BlockSpec(cvec.shape, lambda l, j: (0, 0)),
                  pl.BlockSpec((1, d, nb), lambda l, j: (l, 0, j)),
                  pl.BlockSpec((1, 1, nb), lambda l, j: (l, 0, j))],
        out_specs=pl.BlockSpec((1, cvec.shape[0], nb), lambda l, j: (l, 0, j)),
        out_shape=jax.ShapeDtypeStruct((depth, cvec.shape[0], n), F32),
        compiler_params=_cparams(("arbitrary", "arbitrary")),
        name="ada",
    )(cvec, ada_w, ada_b.reshape(depth, 1, n))


def _tile_rows(n_lat_tiles, h_ref, ctx_ref):
    if ctx_ref is None:
        return h_ref[0]
    return jnp.where(pl.program_id(1) < n_lat_tiles, h_ref[0], ctx_ref[0])


def _inproj_kernel(kind, n_lat_tiles, split, h_ref, *refs):
    ctx_ref = refs[0] if split else None
    mod_ref, g_ref, w_ref, *rest = refs[1:] if split else refs
    m = mod_ref[0, 0]
    a = _rms_mod(_tile_rows(n_lat_tiles, h_ref, ctx_ref), g_ref[...], m[0:1], m[1:2]).astype(BF16)
    tm = a.shape[0]
    lane = lax.broadcasted_iota(jnp.int32, (tm, LANES), 1)

    if kind in ("win", "ax"):
        if kind == "win":
            gain_ref, cos_ref, sin_ref, o_ref = rest
        else:
            gain_ref, cos_ref, sin_ref, o_ref, vt_ref = rest
        cos = cos_ref[...]
        sin = sin_ref[...]
        n_rot = gain_ref.shape[1] // LANES
        if kind == "win":
            head_sum = (lax.broadcasted_iota(jnp.int32, (LANES, LANES), 0) // WIN_HD
                        == lax.broadcasted_iota(jnp.int32, (LANES, LANES), 1) // WIN_HD).astype(BF16)

        def emit(t, xt):
            sl = slice(t * LANES, (t + 1) * LANES)
            if t >= n_rot:
                if kind == "win":
                    o_ref[0, :, sl] = xt.astype(BF16)
                else:
                    vt_ref[0, t - n_rot, 0, :AX_HD, :] = xt.T.astype(BF16)
                    vt_ref[0, t - n_rot, 0, AX_HD:, :] = jnp.ones((VT_ROWS - AX_HD, tm), BF16)
                return
            x2 = xt * xt
            if kind == "win":
                hi = x2.astype(BF16)
                lo = (x2 - hi.astype(F32)).astype(BF16)
                ms = (_dot(hi, head_sum) + _dot(lo, head_sum)) * (1.0 / WIN_HD)
                xn = xt * lax.rsqrt(ms + EPS) * gain_ref[:, sl]
                first = (lane % WIN_HD) < (WIN_HD // 2)
                sw = jnp.where(first, pltpu.roll(xn, LANES - WIN_HD // 2, 1), pltpu.roll(xn, WIN_HD // 2, 1))
            else:
                ms = jnp.mean(x2, axis=-1, keepdims=True)
                xn = xt * lax.rsqrt(ms + EPS) * gain_ref[:, sl]
                sw = pltpu.roll(xn, AX_HD // 2, 1)
            o_ref[0, :, sl] = (xn * cos + sw * sin).astype(BF16)
    else:
        bg_ref, qkv_ref, og_ref, gate_ref = rest
        nq = ML_HEADS * ML_DK // LANES
        nqkv = nq * 2 + ML_HEADS * ML_DV // LANES
        nog = nqkv + ML_HEADS * ML_DV // LANES

        def emit(t, xt):
            sl = slice(t * LANES, (t + 1) * LANES)
            if t < nq:
                qkv_ref[0, :, sl] = (xt * ML_DK ** -0.5).astype(BF16)
            elif t < nqkv:
                qkv_ref[0, :, sl] = xt.astype(BF16)
            elif t < nog:
                og_ref[0, :, (t - nqkv) * LANES:(t - nqkv + 1) * LANES] = xt
            else:
                g = xt + bg_ref[...]
                is_forget = (lane % (2 * ML_HEADS)) >= ML_HEADS
                log_sig = jnp.minimum(g, 0.0) - jnp.log1p(jnp.exp(-jnp.abs(g)))
                gate_ref[0] = jnp.where(is_forget, log_sig, g)

    n_out = w_ref.shape[1]
    starts = list(range(0, n_out, PROJ_CHUNK))

    def proj(c):
        return _dot(a, w_ref[:, starts[c]:min(starts[c] + PROJ_CHUNK, n_out)])

    y_next = proj(0)
    for c in range(len(starts)):
        y = y_next
        if c + 1 < len(starts):
            y_next = proj(c + 1)
        for i in range(y.shape[1] // LANES):
            emit(starts[c] // LANES + i, y[:, i * LANES:(i + 1) * LANES])


def _mod_spec(n_lat_tiles, ctx_row, layer):
    return pl.BlockSpec((1, 1, 6, D_MODEL), lambda b, j: (layer, jnp.where(j < n_lat_tiles, b, ctx_row), 0, 0))


def _stream_specs(h, ctx, n_lat_tiles):
    d = h.shape[2]
    if ctx is None:
        return h.shape[1], (h,), [pl.BlockSpec((1, TM, d), lambda bi, j: (bi, j, 0))]
    return (h.shape[1] + ctx.shape[1], (h, ctx),
            [pl.BlockSpec((1, TM, d), lambda bi, j: (bi, jnp.minimum(j, n_lat_tiles - 1), 0)),
             pl.BlockSpec((1, TM, d), lambda bi, j: (bi, 0, 0))])


def _inproj_call(kind, h, ctx, mods, layer, gain, w, extras, n_lat_tiles):
    b, _, d = h.shape
    n = w.shape[1]
    ctx_row = b
    t, streams, stream_specs = _stream_specs(h, ctx, n_lat_tiles)
    in_specs = stream_specs + [
                _mod_spec(n_lat_tiles, ctx_row, layer),
                _resident((1, d)),
                _resident(w.shape)]
    if kind in ("win", "ax"):
        hgain, cos, sin = extras
        in_specs += [_resident(hgain.shape),
                     pl.BlockSpec((TM, LANES), lambda bi, j: (j, 0)),
                     pl.BlockSpec((TM, LANES), lambda bi, j: (j, 0))]
        if kind == "win":
            out_shape = jax.ShapeDtypeStruct((b, t, n), BF16)
            out_specs = pl.BlockSpec((1, TM, n), lambda bi, j: (bi, j, 0))
        else:
            n_qk = n - AX_KV * AX_HD
            out_shape = (jax.ShapeDtypeStruct((b, t, n_qk), BF16),
                         jax.ShapeDtypeStruct((b, AX_KV, t // TM, VT_ROWS, TM), BF16))
            out_specs = (pl.BlockSpec((1, TM, n_qk), lambda bi, j: (bi, j, 0)),
                         pl.BlockSpec((1, AX_KV, 1, VT_ROWS, TM), lambda bi, j: (bi, 0, j, 0, 0)))
        args = (hgain, cos, sin)
    else:
        (bg,) = extras
        nqkv = 2 * ML_HEADS * ML_DK + ML_HEADS * ML_DV
        nv = ML_HEADS * ML_DV
        in_specs += [_resident(bg.shape)]
        out_shape = (jax.ShapeDtypeStruct((b, t, nqkv), BF16),
                     jax.ShapeDtypeStruct((b, t, nv), F32),
                     jax.ShapeDtypeStruct((b, t, LANES), F32))
        out_specs = (pl.BlockSpec((1, TM, nqkv), lambda bi, j: (bi, j, 0)),
                     pl.BlockSpec((1, TM, nv), lambda bi, j: (bi, j, 0)),
                     pl.BlockSpec((1, TM, LANES), lambda bi, j: (bi, j, 0)))
        args = (bg,)
    return pl.pallas_call(
        functools.partial(_inproj_kernel, kind, n_lat_tiles, ctx is not None),
        grid=(b, t // TM),
        in_specs=in_specs, out_specs=out_specs, out_shape=out_shape,
        compiler_params=_cparams(("parallel", "arbitrary")),
        name="inproj_" + kind,
    )(*streams, mods, gain, w, *args)


def _win_heads(q_ref, k, v, bias, sink_ref, o_ref):
    tq = q_ref.shape[1]
    lane = lax.broadcasted_iota(jnp.int32, (tq, LANES), 1)
    lo = lane < WIN_HD
    group = WIN_HEADS // WIN_KV
    if bias is not None:
        bias = jnp.concatenate([bias] * group, axis=0)

    def scores(kv):
        kt, khalf = kv // 2, kv % 2
        qs = []
        for g in range(group):
            head = kv * group + g
            tile, half = head // 2, head % 2
            qt = q_ref[0, :, tile * LANES:(tile + 1) * LANES].astype(F32)
            qh = jnp.where(lo, qt, 0.0) if half == 0 else jnp.where(lo, 0.0, qt)
            if half != khalf:
                qh = pltpu.roll(qh, WIN_HD, 1)
            qs.append(qh.astype(BF16))
        return _dot_nt(jnp.concatenate(qs, axis=0), k[:, kt * LANES:(kt + 1) * LANES])

    s_next = scores(0)
    for kv in range(WIN_KV):
        kt, khalf = kv // 2, kv % 2
        s = s_next
        if kv + 1 < WIN_KV:
            s_next = scores(kv + 1)
        if bias is not None:
            s = s + bias
        sink = jnp.concatenate(
            [jnp.broadcast_to(sink_ref[kv * group + g:kv * group + g + 1, :][:, 0:1], (tq, 1)) for g in range(group)],
            axis=0)
        mx = jnp.maximum(jnp.max(s, axis=-1, keepdims=True), sink)
        e = jnp.exp(s - mx)
        den = jnp.sum(e, axis=-1, keepdims=True) + jnp.exp(sink - mx)
        o = _dot(e.astype(BF16), v[:, kt * LANES:(kt + 1) * LANES]) / den
        for pair in range(group // 2):
            halves = []
            for half in range(2):
                og = o[(2 * pair + half) * tq:(2 * pair + half + 1) * tq]
                halves.append(pltpu.roll(og, WIN_HD, 1) if half != khalf else og)
            tile = (kv * group) // 2 + pair
            o_ref[0, :, tile * LANES:(tile + 1) * LANES] = jnp.where(lo, halves[0], halves[1]).astype(BF16)


def _win_kernel(n_lat, q_ref, kp, kc, kn, kx, vp, vc, vn, vx, sink_ref, o_ref):
    i = pl.program_id(1)
    tq = q_ref.shape[1]
    n_ctx = kx.shape[1]

    @pl.when(i < n_lat)
    def _():
        w = WINDOW
        nk = tq + 2 * w + n_ctx
        col = lax.broadcasted_iota(jnp.int32, (1, nk), 1)
        t = lax.broadcasted_iota(jnp.int32, (tq, nk), 1) - lax.broadcasted_iota(jnp.int32, (tq, nk), 0)
        big = 4 * nk
        t_min = jnp.where(col < w, jnp.where(i > 0, 0, big), jnp.where(col < tq + 2 * w, 0, -big))
        t_max = jnp.where(col >= tq + 2 * w, big, jnp.where(col >= tq + w, jnp.where(i < n_lat - 1, 2 * w, -big), 2 * w))
        bias = jnp.where(t < t_min, NEG, 0.0) + jnp.where(t > t_max, NEG, 0.0)
        k = jnp.concatenate([kp[0], kc[0], kn[0], kx[0]], axis=0)
        v = jnp.concatenate([vp[0], vc[0], vn[0], vx[0]], axis=0)
        _win_heads(q_ref, k, v, bias, sink_ref, o_ref)

    @pl.when(i >= n_lat)
    def _():
        _win_heads(q_ref, kx[0], vx[0], None, sink_ref, o_ref)


def _win_call(qkv, sink_b, s_lat, n_ctx):
    b, t, _ = qkv.shape
    tq = WIN_TQ
    n_lat = s_lat // tq
    nq = t // tq
    dq = WIN_HEADS * WIN_HD
    kw = WIN_KV * WIN_HD
    kcol = dq // kw
    vcol = kcol + 1
    per = tq // WINDOW
    last_w = s_lat // WINDOW - 1
    xrow = s_lat // n_ctx

    def kv_specs(col):
        return [pl.BlockSpec((1, WINDOW, kw), lambda bi, i: (bi, jnp.clip(i * per - 1, 0, last_w), col)),
                pl.BlockSpec((1, tq, kw), lambda bi, i: (bi, jnp.minimum(i, n_lat - 1), col)),
                pl.BlockSpec((1, WINDOW, kw), lambda bi, i: (bi, jnp.minimum((i + 1) * per, last_w), col)),
                pl.BlockSpec((1, n_ctx, kw), lambda bi, i: (bi, xrow, col))]

    return pl.pallas_call(
        functools.partial(_win_kernel, n_lat),
        grid=(b, nq),
        in_specs=[pl.BlockSpec((1, tq, dq), lambda bi, i: (bi, i, 0))] + kv_specs(kcol) + kv_specs(vcol)
                 + [_resident(sink_b.shape)],
        out_specs=pl.BlockSpec((1, tq, dq), lambda bi, i: (bi, i, 0)),
        out_shape=jax.ShapeDtypeStruct((b, t, dq), BF16),
        compiler_params=_cparams(("parallel", "arbitrary")),
        name="win_attn",
    )(qkv, *([qkv] * 8), sink_b)


def _flash_kernel(n_lat_q, n_full, s_lat, n_ctx, q_ref, k_ref, vt_ref, o_ref, m_sc, acc_sc, s_sc):
    qi = pl.program_id(2)
    tq = q_ref.shape[1]
    group = AX_HEADS // AX_KV
    per = FLASH_CK // TM
    q = jnp.concatenate([q_ref[0, :, g * AX_HD:(g + 1) * AX_HD] for g in range(group)], axis=0)

    m_sc[...] = jnp.full(m_sc.shape, NEG, F32)
    acc_sc[...] = jnp.zeros(acc_sc.shape, F32)

    def scores(slot, start, size):
        s_sc[slot, :size, :] = _dot_nt(k_ref[0, pl.ds(start, size), :], q)

    def update(slot, block, size):
        s = s_sc[slot, :size, :]
        m_prev = m_sc[...]
        m_next = jnp.maximum(m_prev, jnp.max(s, axis=0, keepdims=True))
        p = jnp.exp2(s - m_next).astype(BF16)
        alpha = jnp.exp2(m_prev - m_next)
        pv = None
        for i in range(size // TM):
            part = _dot(vt_ref[0, 0, block + i], p[i * TM:(i + 1) * TM])
            pv = part if pv is None else pv + part
        acc_sc[...] = acc_sc[...] * alpha + pv
        m_sc[...] = m_next

    @pl.when(qi < n_lat_q)
    def _():
        def at(c):
            return pl.multiple_of(c * FLASH_CK, FLASH_CK)

        scores(0, 0, FLASH_CK)

        def body(i, carry):
            c = FLASH_UNROLL * i
            for u in range(FLASH_UNROLL):
                scores((u + 1) % 2, at(c + u + 1), FLASH_CK)
                update(u % 2, (c + u) * per, FLASH_CK)
            return carry

        n_iter = (n_full - 1) // FLASH_UNROLL
        lax.fori_loop(0, n_iter, body, 0)
        tail = [(c * FLASH_CK, c * per, FLASH_CK) for c in range(FLASH_UNROLL * n_iter, n_full)]
        tail.append((s_lat, s_lat // TM, n_ctx))
        for idx, (start, block, size) in enumerate(tail):
            if idx + 1 < len(tail):
                scores((idx + 1) % 2, tail[idx + 1][0], tail[idx + 1][2])
            update(idx % 2, block, size)

    @pl.when(qi >= n_lat_q)
    def _():
        scores(0, s_lat, n_ctx)
        update(0, s_lat // TM, n_ctx)

    acc = acc_sc[...]
    o = acc[:AX_HD] / acc[AX_HD:AX_HD + 1]
    for g in range(group):
        o_ref[0, :, g * AX_HD:(g + 1) * AX_HD] = o[:, g * tq:(g + 1) * tq].T.astype(BF16)


def _flash_call(qk, vt, s_lat, n_ctx):
    b, t, _ = qk.shape
    tq = FLASH_TQ
    group = AX_HEADS // AX_KV
    gw = group * AX_HD
    dq = AX_HEADS * AX_HD
    kcol = dq // AX_HD
    m_cols = group * tq
    return pl.pallas_call(
        functools.partial(_flash_kernel, s_lat // tq, s_lat // FLASH_CK, s_lat, n_ctx),
        grid=(b, AX_KV, t // tq),
        in_specs=[pl.BlockSpec((1, tq, gw), lambda bi, j, i: (bi, i, j)),
                  pl.BlockSpec((1, t, AX_HD), lambda bi, j, i: (bi, 0, kcol + j)),
                  pl.BlockSpec((1, 1) + vt.shape[2:], lambda bi, j, i: (bi, j, 0, 0, 0))],
        out_specs=pl.BlockSpec((1, tq, gw), lambda bi, j, i: (bi, i, j)),
        out_shape=jax.ShapeDtypeStruct((b, t, dq), BF16),
        scratch_shapes=[pltpu.VMEM((1, m_cols), F32), pltpu.VMEM((VT_ROWS, m_cols), F32),
                        pltpu.VMEM((2, FLASH_CK, m_cols), F32)],
        compiler_params=_cparams(("parallel", "parallel", "arbitrary")),
        name="flash_attn",
    )(qk, qk, vt)


def _mlstm_kernel(qf_ref, gf_ref, qb_ref, gb_ref, hf_ref, hb_ref, c_sc, n_sc, m_sc):
    step = pl.program_id(1)
    L = ML_CHUNK

    @pl.when(step == 0)
    def _():
        c_sc[...] = jnp.zeros(c_sc.shape, F32)
        n_sc[...] = jnp.zeros(n_sc.shape, F32)
        m_sc[...] = jnp.full(m_sc.shape, NEG, F32)

    r = lax.broadcasted_iota(jnp.int32, (L, L), 0)
    c = lax.broadcasted_iota(jnp.int32, (L, L), 1)
    lo = c <= r
    up = c >= r
    lo_f = lo.astype(F32)
    up_f = up.astype(F32)
    hi = lax.Precision.HIGHEST
    nqk = ML_HEADS * ML_DK

    dirs = ((qf_ref, gf_ref, hf_ref), (qb_ref, gb_ref, hb_ref))
    n_streams = 2 * qf_ref.shape[0]
    units = [(s, h) for s in range(n_streams) for h in range(ML_HEADS)]

    def qkv(s, h):
        x_ref, bb = dirs[s % 2][0], s // 2
        return (x_ref[bb, :, h * ML_DK:(h + 1) * ML_DK],
                x_ref[bb, :, nqk + h * ML_DK:nqk + (h + 1) * ML_DK],
                x_ref[bb, :, 2 * nqk + h * ML_DV:2 * nqk + (h + 1) * ML_DV])

    stats = []
    for s in range(n_streams):
        d = s % 2
        gates = dirs[d][1][s // 2]
        gates_t = gates.T
        tri_col, tri_row = (lo_f, up_f) if d == 0 else (up_f, lo_f)
        cum_col = jnp.dot(tri_col, gates, precision=hi, preferred_element_type=F32)
        cum_row = jnp.dot(gates_t, tri_row, precision=hi, preferred_element_type=F32)
        total = jnp.sum(gates, axis=0, keepdims=True)
        stats.append((gates, gates_t, cum_col, cum_row, total))

    s_mat, q_c = {}, {}
    for s, h in units:
        q, k, _ = qkv(s, h)
        s_mat[s, h] = _dot_nt(q, k)
        q_c[s, h] = _dot_nt(q, c_sc[s, h].astype(BF16))

    c_bar, n_bar, m_bar = {}, {}, {}
    for s, h in units:
        gates, _, cum_col, _, total = stats[s]
        ci = 2 * ML_HEADS * (s % 2) + h
        cf = ci + ML_HEADS
        _, k, v = qkv(s, h)
        a_col = total[:, cf:cf + 1] - cum_col[:, cf:cf + 1] + gates[:, ci:ci + 1]
        m_bar[s, h] = jnp.max(a_col, axis=0, keepdims=True)
        w_col = jnp.exp(a_col - m_bar[s, h])
        c_bar[s, h] = _dot_tn((v.astype(F32) * w_col).astype(BF16), k)
        n_bar[s, h] = jnp.sum(k.astype(F32) * w_col, axis=0, keepdims=True)

    num, den, m_ts = {}, {}, {}
    for s, h in units:
        _, gates_t, cum_col, cum_row, _ = stats[s]
        mask = lo if s % 2 == 0 else up
        ci = 2 * ML_HEADS * (s % 2) + h
        cf = ci + ML_HEADS
        q, _, v = qkv(s, h)
        n_prev = n_sc[s, h][0:1, :]
        m_prev = m_sc[s, h][0:1, 0:1]
        f_col = cum_col[:, cf:cf + 1]
        dmat = jnp.where(mask, f_col - cum_row[cf:cf + 1, :] + gates_t[ci:ci + 1, :], NEG)
        inter = f_col + m_prev
        m_t = jnp.maximum(inter, jnp.max(dmat, axis=-1, keepdims=True))
        w_inter = jnp.exp(inter - m_t)
        qk = s_mat[s, h] * jnp.exp(dmat - m_t)
        qn = jnp.sum(q.astype(F32) * n_prev, axis=-1, keepdims=True)
        num[s, h] = _dot(qk.astype(BF16), v) + w_inter * q_c[s, h]
        den[s, h] = jnp.sum(qk, axis=-1, keepdims=True) + w_inter * qn
        m_ts[s, h] = m_t

    for s, h in units:
        dirs[s % 2][2][s // 2, :, h * ML_DV:(h + 1) * ML_DV] = (
            num[s, h] / jnp.maximum(jnp.abs(den[s, h]), jnp.exp(-m_ts[s, h])))

    for s, h in units:
        total = stats[s][4]
        cf = 2 * ML_HEADS * (s % 2) + h + ML_HEADS
        f_tot = total[:, cf:cf + 1]
        n_prev = n_sc[s, h][0:1, :]
        m_prev = m_sc[s, h][0:1, 0:1]
        m_new = jnp.maximum(f_tot + m_prev, m_bar[s, h])
        decay = jnp.exp(f_tot + m_prev - m_new)
        inj = jnp.exp(m_bar[s, h] - m_new)
        c_sc[s, h] = decay * c_sc[s, h] + inj * c_bar[s, h]
        n_sc[s, h] = jnp.broadcast_to(decay * n_prev + inj * n_bar[s, h], (SUBLANES, ML_DK))
        m_sc[s, h] = jnp.broadcast_to(m_new, (SUBLANES, LANES))


def _mlstm_call(qkv, gates, s_lat):
    b, t, nx = qkv.shape
    nc = t // ML_CHUNK
    nlc = s_lat // ML_CHUNK
    nv = ML_HEADS * ML_DV

    def fwd(bi, s):
        return (bi, (s + nlc) % nc, 0)

    def bwd(bi, s):
        return (bi, nc - 1 - s, 0)

    nb = 1
    return pl.pallas_call(
        _mlstm_kernel,
        grid=(b // nb, nc),
        in_specs=[pl.BlockSpec((nb, ML_CHUNK, nx), fwd), pl.BlockSpec((nb, ML_CHUNK, LANES), fwd),
                  pl.BlockSpec((nb, ML_CHUNK, nx), bwd), pl.BlockSpec((nb, ML_CHUNK, LANES), bwd)],
        out_specs=(pl.BlockSpec((nb, ML_CHUNK, nv), fwd), pl.BlockSpec((nb, ML_CHUNK, nv), bwd)),
        out_shape=(jax.ShapeDtypeStruct((b, t, nv), F32), jax.ShapeDtypeStruct((b, t, nv), F32)),
        scratch_shapes=[pltpu.VMEM((2 * nb, ML_HEADS, ML_DV, ML_DK), F32),
                        pltpu.VMEM((2 * nb, ML_HEADS, SUBLANES, ML_DK), F32),
                        pltpu.VMEM((2 * nb, ML_HEADS, SUBLANES, LANES), F32)],
        compiler_params=_cparams(("parallel", "arbitrary")),
        name="mlstm",
    )(qkv, gates, qkv, gates)


def _ml_readout(hs, og, hn_ref):
    parts = []
    for h in range(ML_HEADS):
        sl = slice(h * ML_DV, (h + 1) * ML_DV)
        x = hs[:, sl]
        xn = x * lax.rsqrt(jnp.mean(x * x, axis=-1, keepdims=True) + EPS) * hn_ref[:, sl]
        parts.append((_sigmoid(og[:, sl]) * xn).astype(BF16))
    return jnp.concatenate(parts, axis=1)


def _convmlp_kernel(kind, pre, split, n_lat_tiles, n_all_tiles, *refs):
    n_pre = {None: 0, "attn": 4, "ml": 11}[pre]
    pre_refs = refs[:n_pre]
    hp_ref, h_ref, hn_ref = refs[n_pre:n_pre + 3]
    ctx_ref = refs[n_pre + 3] if split else None
    mod_ref, g_ref, w1_ref, cw_ref, w2_ref, o_ref, perm_sc = refs[n_pre + 3 + split:]
    j = pl.program_id(1)
    m = mod_ref[0, 0]
    sh, sc, gt = (m[0:1], m[1:2], m[2:3]) if kind == "sc" else (m[3:4], m[4:5], m[5:6])
    x = jnp.concatenate([hp_ref[0], _tile_rows(n_lat_tiles, h_ref, ctx_ref), hn_ref[0]], axis=0)
    rows = x.shape[0]
    if pre == "attn":
        yp_ref, y_ref, yn_ref, wo_ref = pre_refs
        y = jnp.concatenate([yp_ref[0], y_ref[0], yn_ref[0]], axis=0)
        skip = BF16_ROWS - HALO
        x = x + m[2:3] * _dot(y, wo_ref[...])[skip:skip + rows]
    elif pre == "ml":
        ext = [jnp.concatenate([pre_refs[3 * i][0], pre_refs[3 * i + 1][0], pre_refs[3 * i + 2][0]], axis=0)
               for i in range(3)]
        gain_ref, wo_ref = pre_refs[9:]
        x = x + m[2:3] * _dot(_ml_readout(ext[0] + ext[1], ext[2], gain_ref), wo_ref[...])
    a = _rms_mod(x, g_ref[...], sh, sc)
    seg_first = (j == 0) | (j == n_lat_tiles)
    seg_last = (j == n_lat_tiles - 1) | (j == n_all_tiles - 1)
    r = lax.broadcasted_iota(jnp.int32, (rows, 1), 0)
    dead = ((r < HALO) & seg_first) | ((r >= rows - HALO) & seg_last)
    a = jnp.where(dead, 0.0, a)

    pitch = rows // SUBLANES
    n_slab = a.shape[1] // LANES
    for k in range(n_slab):
        perm_sc[k] = a[:, k * LANES:(k + 1) * LANES]
    a = jnp.concatenate(
        [jnp.concatenate([perm_sc[k, pl.ds(r, SUBLANES, stride=pitch), :] for k in range(n_slab)], axis=1)
         for r in range(pitch)], axis=0).astype(BF16)

    def conv3(z, col):
        w = cw_ref[:, col:col + CONV_CHUNK]
        prev = jnp.concatenate([pltpu.roll(z[rows - SUBLANES:], 1, 0), z[:rows - SUBLANES]], axis=0)
        nxt = jnp.concatenate([z[SUBLANES:], pltpu.roll(z[:SUBLANES], SUBLANES - 1, 0)], axis=0)
        return prev * w[0:1] + z * w[1:2] + nxt * w[2:3]

    hidden = w2_ref.shape[0]
    parts = w1_ref.shape[1] // hidden

    def up(c):
        return [_dot(a, w1_ref[:, p * hidden + c * CONV_CHUNK:p * hidden + (c + 1) * CONV_CHUNK]) for p in range(parts)]

    n_chunks = hidden // CONV_CHUNK
    acc = None
    u_next = up(0)
    for c in range(n_chunks):
        u = u_next
        if c + 1 < n_chunks:
            u_next = up(c + 1)
        if kind == "sc":
            hid = u[0] * conv3(u[1] * u[2], c * CONV_CHUNK)
        else:
            gg = conv3(u[0], c * CONV_CHUNK)
            uu = conv3(u[1], hidden + c * CONV_CHUNK)
            hid = gg * _sigmoid(gg) * uu
        y = _dot(hid.astype(BF16), w2_ref[c * CONV_CHUNK:(c + 1) * CONV_CHUNK, :])
        acc = y if acc is None else acc + y

    y = gt * acc
    for r in range(pitch):
        for k in range(n_slab):
            perm_sc[k, pl.ds(r, SUBLANES, stride=pitch), :] = y[r * SUBLANES:(r + 1) * SUBLANES, k * LANES:(k + 1) * LANES]
    o_ref[0] = x[HALO:rows - HALO] + jnp.concatenate([perm_sc[k, HALO:rows - HALO, :] for k in range(n_slab)], axis=1)


def _halo_specs(t, width, halo):
    per = TM // halo
    last = t // halo - 1
    last_tile = t // TM - 1
    return [pl.BlockSpec((1, halo, width), lambda bi, j: (bi, jnp.maximum(j * per - 1, 0), 0)),
            pl.BlockSpec((1, TM, width), lambda bi, j: (bi, jnp.minimum(j, last_tile), 0)),
            pl.BlockSpec((1, halo, width), lambda bi, j: (bi, jnp.minimum((j + 1) * per, last), 0))]


def _convmlp_call(kind, h, ctx, mods, layer, gain, w1, cw, w2, n_lat_tiles, n_all_tiles, n_tiles, pre=None, pre_ins=()):
    b, t_h, d = h.shape
    t = t_h + (0 if ctx is None else ctx.shape[1])
    ctx_row = b
    ctx_specs = [] if ctx is None else [pl.BlockSpec((1, TM, d), lambda bi, j: (bi, 0, 0))]
    ctx_args = () if ctx is None else (ctx,)
    if pre == "attn":
        y, w_o = pre_ins
        pre_specs = _halo_specs(t, y.shape[2], BF16_ROWS) + [_resident(w_o.shape)]
        pre_args = (y, y, y, w_o)
    elif pre == "ml":
        hf, hb, og, hn, w_o = pre_ins
        pre_specs = _halo_specs(t, hf.shape[2], HALO) * 3 + [_resident(hn.shape), _resident(w_o.shape)]
        pre_args = (hf, hf, hf, hb, hb, hb, og, og, og, hn, w_o)
    else:
        pre_specs, pre_args = [], ()
    return pl.pallas_call(
        functools.partial(_convmlp_kernel, kind, pre, ctx is not None, n_lat_tiles, n_all_tiles),
        grid=(b, n_tiles),
        in_specs=pre_specs + _halo_specs(t_h, d, HALO) + ctx_specs + [
            _mod_spec(n_lat_tiles, ctx_row, layer),
            _resident((1, d)), _resident(w1.shape), _resident(cw.shape), _resident(w2.shape)],
        out_specs=pl.BlockSpec((1, TM, d), lambda bi, j: (bi, j, 0)),
        out_shape=jax.ShapeDtypeStruct((b, n_tiles * TM, d), F32),
        scratch_shapes=[pltpu.VMEM((d // LANES, TM + 2 * HALO, LANES), F32)],
        compiler_params=_cparams(("parallel", "arbitrary")),
        name="convmlp_" + kind + ("_" + pre if pre else ""),
    )(*pre_args, h, h, h, *ctx_args, mods, gain, w1, cw, w2)


def _rope_tables(s_lat, n_ctx, hd, reps):
    rows = s_lat // GRID_W
    row = np.repeat(np.arange(rows, dtype=np.float32), GRID_W)
    col = np.tile(np.arange(GRID_W, dtype=np.float32), rows)
    n_freq = hd // 4
    inv = jnp.power(ROPE_THETA, -jnp.arange(n_freq, dtype=F32) / n_freq)
    ang = jnp.concatenate([jnp.asarray(row)[:, None] * inv, jnp.asarray(col)[:, None] * inv], axis=-1)
    cos, sin = jnp.cos(ang), jnp.sin(ang)
    cos = jnp.tile(jnp.concatenate([cos, cos], axis=1), (1, reps))
    sin = jnp.tile(jnp.concatenate([-sin, sin], axis=1), (1, reps))
    cos = jnp.concatenate([cos, jnp.ones((n_ctx, LANES), F32)], axis=0)
    sin = jnp.concatenate([sin, jnp.zeros((n_ctx, LANES), F32)], axis=0)
    return cos, sin


def kernel(x, c, ctx, c_ctx, ada_w, ada_b, norm_mix, norm_ffn, ffn_w_up, ffn_conv, ffn_w_down, win_w_qkv, win_q_norm, win_k_norm, win_sink, win_w_o, sc_w_in, sc_conv, sc_w_out, ax_w_qkv, ax_q_norm, ax_k_norm, ax_w_o, ml_w_in, ml_b_gate, ml_h_norm, ml_w_out):
    b, s_lat, d = x.shape
    n_ctx = ctx.shape[1]
    assert d == D_MODEL and n_ctx == TM and s_lat % FLASH_CK == 0 and s_lat % GRID_W == 0
    assert FLASH_CK % TM == 0 and n_ctx % FLASH_TQ == 0 and n_ctx % ML_CHUNK == 0
    assert n_ctx % WIN_TQ == 0 and WIN_TQ % WINDOW == 0
    n_lat_tiles = s_lat // TM
    n_all_tiles = n_lat_tiles + n_ctx // TM
    depth = ada_w.shape[0]

    h, hc = x, ctx
    pad_rows = -(b + 1) % SUBLANES
    cvec = jnp.concatenate([c, c_ctx[None], jnp.zeros((pad_rows, d), F32)], axis=0)
    mods = _ada_call(cvec, ada_w, ada_b).reshape(depth, b + 1 + pad_rows, 6, d)

    for i in range(depth):
        kind, j = i % N_MIXERS, i // N_MIXERS
        last = i == depth - 1
        n_tiles = n_lat_tiles if last else n_all_tiles
        g_mix = norm_mix[i][None]
        pre, pre_ins = None, ()
        if kind == 0:
            gain = jnp.concatenate([jnp.tile(win_q_norm[j], WIN_HEADS) * WIN_HD ** -0.5,
                                    jnp.tile(win_k_norm[j], WIN_KV)])[None]
            cos, sin = _rope_tables(s_lat, n_ctx, WIN_HD, 2)
            qkv = _inproj_call("win", h, hc, mods, i, g_mix, win_w_qkv[j].astype(BF16), (gain, cos, sin), n_lat_tiles)
            sink_b = jnp.broadcast_to(win_sink[j][:, None], (WIN_HEADS, LANES))
            pre, pre_ins = "attn", (_win_call(qkv, sink_b, s_lat, n_ctx), win_w_o[j].astype(BF16))
        elif kind == 1:
            h = _convmlp_call("sc", h, hc, mods, i, g_mix, sc_w_in[j].astype(BF16), sc_conv[j], sc_w_out[j].astype(BF16),
                              n_lat_tiles, n_all_tiles, n_tiles)
            hc = None
        elif kind == 2:
            gain = jnp.concatenate([jnp.tile(ax_q_norm[j], AX_HEADS) * (AX_HD ** -0.5 * np.log2(np.e)),
                                    jnp.tile(ax_k_norm[j], AX_KV)])[None]
            cos, sin = _rope_tables(s_lat, n_ctx, AX_HD, 1)
            qk, vt = _inproj_call("ax", h, hc, mods, i, g_mix, ax_w_qkv[j].astype(BF16), (gain, cos, sin), n_lat_tiles)
            pre, pre_ins = "attn", (_flash_call(qk, vt, s_lat, n_ctx), ax_w_o[j].astype(BF16))
        else:
            w = jnp.concatenate([ml_w_in[j], jnp.zeros((d, LANES - 4 * ML_HEADS), F32)], axis=1).astype(BF16)
            bg = jnp.concatenate([ml_b_gate[j], jnp.zeros((LANES - 4 * ML_HEADS,), F32)])[None]
            qkv, og, gates = _inproj_call("ml", h, hc, mods, i, g_mix, w, (bg,), n_lat_tiles)
            hf, hb = _mlstm_call(qkv, gates, s_lat)
            hn = jnp.tile(ml_h_norm[j], ML_HEADS)[None]
            pre, pre_ins = "ml", (hf, hb, og, hn, ml_w_out[j].astype(BF16))
        h = _convmlp_call("ffn", h, hc, mods, i, norm_ffn[i][None], ffn_w_up[i].astype(BF16), ffn_conv[i], ffn_w_down[i].astype(BF16),
                          n_lat_tiles, n_all_tiles, n_tiles, pre, pre_ins)
        hc = None
    return h[:, :s_lat] if h.shape[1] != s_lat else h
```

```python
import functools

import numpy as np
import jax
import jax.numpy as jnp
from jax import lax
from jax.experimental import pallas as pl
from jax.experimental.pallas import tpu as pltpu

D_MODEL = 1024
GRID_W = 64
N_MIXERS = 4
WINDOW = 128
WIN_HEADS = 16
WIN_KV = 4
WIN_HD = 64
AX_HEADS = 8
AX_KV = 2
AX_HD = 128
ML_HEADS = 4
ML_DK = 128
ML_DV = 256
ML_CHUNK = 256
ROPE_THETA = 10000.0
EPS = 1e-6
NEG = -1e30

F32 = jnp.float32
BF16 = jnp.bfloat16

LANES = 128
SUBLANES = 8
TM = 256
HALO = SUBLANES
CONV_CHUNK = 256
PROJ_CHUNK = 256
FLASH_TQ = 256
FLASH_CK = 512
FLASH_UNROLL = 8
WIN_TQ = 256
BF16_ROWS = 16
VT_ROWS = AX_HD + BF16_ROWS
V7X_VMEM_BYTES = 64 * 1024 * 1024
VMEM_LIMIT = V7X_VMEM_BYTES * 7 // 8


def _cparams(sem):
    return pltpu.CompilerParams(dimension_semantics=sem, vmem_limit_bytes=VMEM_LIMIT)


def _resident(shape):
    nd = len(shape)
    return pl.BlockSpec(shape, lambda *_: (0,) * nd, pipeline_mode=pl.Buffered(1))


def _dot(a, b):
    return jnp.dot(a, b, preferred_element_type=F32)


def _dot_nt(a, b):
    return lax.dot_general(a, b, (((1,), (1,)), ((), ())), preferred_element_type=F32)


def _dot_tn(a, b):
    return lax.dot_general(a, b, (((0,), (0,)), ((), ())), preferred_element_type=F32)


def _sigmoid(x):
    return 1.0 / (1.0 + jnp.exp(-x))


def _rms_mod(x, g, shift, scale):
    y = x * lax.rsqrt(jnp.mean(x * x, axis=-1, keepdims=True) + EPS) * g
    return y * (1.0 + scale) + shift


def _ada_kernel(c_ref, w_ref, b_ref, o_ref):
    c = c_ref[...]
    o_ref[0] = _dot(c * _sigmoid(c), w_ref[0]) + b_ref[0]


def _ada_call(cvec, ada_w, ada_b):
    depth, d, n = ada_w.shape
    nb = n // 4
    return pl.pallas_call(
        _ada_kernel,
        grid=(depth, n // nb),
        in_specs=[pl.BlockSpec(cvec.shape, lambda l, j: (0, 0)),
                  pl.BlockSpec((1, d, nb), lambda l, j: (l, 0, j)),
                  pl.BlockSpec((1, 1, nb), lambda l, j: (l, 0, j))],
        out_specs=pl.BlockSpec((1, cvec.shape[0], nb), lambda l, j: (l, 0, j)),
        out_shape=jax.ShapeDtypeStruct((depth, cvec.shape[0], n), F32),
        compiler_params=_cparams(("arbitrary", "arbitrary")),
        name="ada",
    )(cvec, ada_w, ada_b.reshape(depth, 1, n))


def _tile_rows(n_lat_tiles, h_ref, ctx_ref):
    if ctx_ref is None:
        return h_ref[0]
    return jnp.where(pl.program_id(1) < n_lat_tiles, h_ref[0], ctx_ref[0])


def _inproj_kernel(kind, n_lat_tiles, split, h_ref, *refs):
    ctx_ref = refs[0] if split else None
    mod_ref, g_ref, w_ref, *rest = refs[1:] if split else refs
    m = mod_ref[0, 0]
    a = _rms_mod(_tile_rows(n_lat_tiles, h_ref, ctx_ref), g_ref[...], m[0:1], m[1:2]).astype(BF16)
    tm = a.shape[0]
    lane = lax.broadcasted_iota(jnp.int32, (tm, LANES), 1)

    if kind in ("win", "ax"):
        gain_ref, cos_ref, sin_ref, o_ref, vt_ref = rest
        cos = cos_ref[...]
        sin = sin_ref[...]
        n_rot = gain_ref.shape[1] // LANES
        if kind == "win":
            head_sum = (lax.broadcasted_iota(jnp.int32, (LANES, LANES), 0) // WIN_HD
                        == lax.broadcasted_iota(jnp.int32, (LANES, LANES), 1) // WIN_HD).astype(BF16)

        def emit(t, xt):
            sl = slice(t * LANES, (t + 1) * LANES)
            if t >= n_rot:
                if kind == "win":
                    vt_ref[0, t - n_rot] = xt.T.astype(BF16)
                else:
                    vt_ref[0, t - n_rot, 0, :AX_HD, :] = xt.T.astype(BF16)
                    vt_ref[0, t - n_rot, 0, AX_HD:, :] = jnp.ones((VT_ROWS - AX_HD, tm), BF16)
                return
            x2 = xt * xt
            if kind == "win":
                hi = x2.astype(BF16)
                lo = (x2 - hi.astype(F32)).astype(BF16)
                ms = (_dot(hi, head_sum) + _dot(lo, head_sum)) * (1.0 / WIN_HD)
                xn = xt * lax.rsqrt(ms + EPS) * gain_ref[:, sl]
                first = (lane % WIN_HD) < (WIN_HD // 2)
                sw = jnp.where(first, pltpu.roll(xn, LANES - WIN_HD // 2, 1), pltpu.roll(xn, WIN_HD // 2, 1))
            else:
                ms = jnp.mean(x2, axis=-1, keepdims=True)
                xn = xt * lax.rsqrt(ms + EPS) * gain_ref[:, sl]
                sw = pltpu.roll(xn, AX_HD // 2, 1)
            o_ref[0, :, sl] = (xn * cos + sw * sin).astype(BF16)
    else:
        bg_ref, qkv_ref, og_ref, gate_ref = rest
        nq = ML_HEADS * ML_DK // LANES
        nqkv = nq * 2 + ML_HEADS * ML_DV // LANES
        nog = nqkv + ML_HEADS * ML_DV // LANES

        def emit(t, xt):
            sl = slice(t * LANES, (t + 1) * LANES)
            if t < nq:
                qkv_ref[0, :, sl] = (xt * ML_DK ** -0.5).astype(BF16)
            elif t < nqkv:
                qkv_ref[0, :, sl] = xt.astype(BF16)
            elif t < nog:
                og_ref[0, :, (t - nqkv) * LANES:(t - nqkv + 1) * LANES] = xt
            else:
                g = xt + bg_ref[...]
                is_forget = (lane % (2 * ML_HEADS)) >= ML_HEADS
                log_sig = jnp.minimum(g, 0.0) - jnp.log1p(jnp.exp(-jnp.abs(g)))
                gate_ref[0] = jnp.where(is_forget, log_sig, g)

    n_out = w_ref.shape[1]
    starts = list(range(0, n_out, PROJ_CHUNK))

    def proj(c):
        return _dot(a, w_ref[:, starts[c]:min(starts[c] + PROJ_CHUNK, n_out)])

    y_next = proj(0)
    for c in range(len(starts)):
        y = y_next
        if c + 1 < len(starts):
            y_next = proj(c + 1)
        for i in range(y.shape[1] // LANES):
            emit(starts[c] // LANES + i, y[:, i * LANES:(i + 1) * LANES])


def _mod_spec(n_lat_tiles, ctx_row, layer):
    return pl.BlockSpec((1, 1, 6, D_MODEL), lambda b, j: (layer, jnp.where(j < n_lat_tiles, b, ctx_row), 0, 0))


def _stream_specs(h, ctx, n_lat_tiles):
    d = h.shape[2]
    if ctx is None:
        return h.shape[1], (h,), [pl.BlockSpec((1, TM, d), lambda bi, j: (bi, j, 0))]
    return (h.shape[1] + ctx.shape[1], (h, ctx),
            [pl.BlockSpec((1, TM, d), lambda bi, j: (bi, jnp.minimum(j, n_lat_tiles - 1), 0)),
             pl.BlockSpec((1, TM, d), lambda bi, j: (bi, 0, 0))])


def _inproj_call(kind, h, ctx, mods, layer, gain, w, extras, n_lat_tiles):
    b, _, d = h.shape
    n = w.shape[1]
    ctx_row = b
    t, streams, stream_specs = _stream_specs(h, ctx, n_lat_tiles)
    in_specs = stream_specs + [
                _mod_spec(n_lat_tiles, ctx_row, layer),
                _resident((1, d)),
                _resident(w.shape)]
    if kind in ("win", "ax"):
        hgain, cos, sin = extras
        in_specs += [_resident(hgain.shape),
                     pl.BlockSpec((TM, LANES), lambda bi, j: (j, 0)),
                     pl.BlockSpec((TM, LANES), lambda bi, j: (j, 0))]
        if kind == "win":
            n_v = WIN_KV * WIN_HD // LANES
            n_qk = n - n_v * LANES
            out_shape = (jax.ShapeDtypeStruct((b, t, n_qk), BF16),
                         jax.ShapeDtypeStruct((b, n_v, LANES, t), BF16))
            out_specs = (pl.BlockSpec((1, TM, n_qk), lambda bi, j: (bi, j, 0)),
                         pl.BlockSpec((1, n_v, LANES, TM), lambda bi, j: (bi, 0, 0, j)))
        else:
            n_qk = n - AX_KV * AX_HD
            out_shape = (jax.ShapeDtypeStruct((b, t, n_qk), BF16),
                         jax.ShapeDtypeStruct((b, AX_KV, t // TM, VT_ROWS, TM), BF16))
            out_specs = (pl.BlockSpec((1, TM, n_qk), lambda bi, j: (bi, j, 0)),
                         pl.BlockSpec((1, AX_KV, 1, VT_ROWS, TM), lambda bi, j: (bi, 0, j, 0, 0)))
        args = (hgain, cos, sin)
    else:
        (bg,) = extras
        nqkv = 2 * ML_HEADS * ML_DK + ML_HEADS * ML_DV
        nv = ML_HEADS * ML_DV
        in_specs += [_resident(bg.shape)]
        out_shape = (jax.ShapeDtypeStruct((b, t, nqkv), BF16),
                     jax.ShapeDtypeStruct((b, t, nv), F32),
                     jax.ShapeDtypeStruct((b, t, LANES), F32))
        out_specs = (pl.BlockSpec((1, TM, nqkv), lambda bi, j: (bi, j, 0)),
                     pl.BlockSpec((1, TM, nv), lambda bi, j: (bi, j, 0)),
                     pl.BlockSpec((1, TM, LANES), lambda bi, j: (bi, j, 0)))
        args = (bg,)
    return pl.pallas_call(
        functools.partial(_inproj_kernel, kind, n_lat_tiles, ctx is not None),
        grid=(b, t // TM),
        in_specs=in_specs, out_specs=out_specs, out_shape=out_shape,
        compiler_params=_cparams(("parallel", "arbitrary")),
        name="inproj_" + kind,
    )(*streams, mods, gain, w, *args)


def _win_heads(q_ref, k, vt, bias, sink_ref, o_ref):
    tq = q_ref.shape[1]
    lane = lax.broadcasted_iota(jnp.int32, (tq, LANES), 1)
    lo = lane < WIN_HD
    group = WIN_HEADS // WIN_KV
    if bias is not None:
        bias = jnp.concatenate([bias] * group, axis=1)

    def scores(kv):
        kt, khalf = kv // 2, kv % 2
        qs = []
        for g in range(group):
            head = kv * group + g
            tile, half = head // 2, head % 2
            qt = q_ref[0, :, tile * LANES:(tile + 1) * LANES].astype(F32)
            qh = jnp.where(lo, qt, 0.0) if half == 0 else jnp.where(lo, 0.0, qt)
            if half != khalf:
                qh = pltpu.roll(qh, WIN_HD, 1)
            qs.append(qh.astype(BF16))
        return _dot_nt(k[:, kt * LANES:(kt + 1) * LANES], jnp.concatenate(qs, axis=0))

    s_next = scores(0)
    for kv in range(WIN_KV):
        kt, khalf = kv // 2, kv % 2
        s = s_next
        if kv + 1 < WIN_KV:
            s_next = scores(kv + 1)
        if bias is not None:
            s = s + bias
        sink = jnp.concatenate([sink_ref[kv * group + g:kv * group + g + 1, :tq] for g in range(group)], axis=1)
        mx = jnp.maximum(jnp.max(s, axis=0, keepdims=True), sink)
        e = jnp.exp(s - mx)
        den = jnp.sum(e, axis=0, keepdims=True) + jnp.exp(sink - mx)
        o = _dot(vt[kt], e.astype(BF16)) / den
        rows = slice(khalf * WIN_HD, (khalf + 1) * WIN_HD)
        for pair in range(group // 2):
            both = jnp.concatenate([o[rows, (2 * pair + half) * tq:(2 * pair + half + 1) * tq] for half in range(2)],
                                   axis=0)
            tile = (kv * group) // 2 + pair
            o_ref[0, :, tile * LANES:(tile + 1) * LANES] = both.T.astype(BF16)


def _win_kernel(n_lat, q_ref, kp, kc, kn, kx, vp, vc, vn, vx, sink_ref, o_ref):
    i = pl.program_id(1)
    tq = q_ref.shape[1]
    n_ctx = kx.shape[1]

    @pl.when(i < n_lat)
    def _():
        w = WINDOW
        nk = tq + 2 * w + n_ctx
        col = lax.broadcasted_iota(jnp.int32, (nk, 1), 0)
        t = lax.broadcasted_iota(jnp.int32, (nk, tq), 0) - lax.broadcasted_iota(jnp.int32, (nk, tq), 1)
        big = 4 * nk
        t_min = jnp.where(col < w, jnp.where(i > 0, 0, big), jnp.where(col < tq + 2 * w, 0, -big))
        t_max = jnp.where(col >= tq + 2 * w, big, jnp.where(col >= tq + w, jnp.where(i < n_lat - 1, 2 * w, -big), 2 * w))
        bias = jnp.where(t < t_min, NEG, 0.0) + jnp.where(t > t_max, NEG, 0.0)
        k = jnp.concatenate([kp[0], kc[0], kn[0], kx[0]], axis=0)
        vt = [jnp.concatenate([vp[0, kt], vc[0, kt], vn[0, kt], vx[0, kt]], axis=1) for kt in range(vx.shape[1])]
        _win_heads(q_ref, k, vt, bias, sink_ref, o_ref)

    @pl.when(i >= n_lat)
    def _():
        _win_heads(q_ref, kx[0], [vx[0, kt] for kt in range(vx.shape[1])], None, sink_ref, o_ref)


def _win_call(qk, vt, sink_b, s_lat, n_ctx):
    b, t, _ = qk.shape
    tq = WIN_TQ
    n_lat = s_lat // tq
    nq = t // tq
    dq = WIN_HEADS * WIN_HD
    kw = WIN_KV * WIN_HD
    kcol = dq // kw
    n_vt = vt.shape[1]
    per = tq // WINDOW
    last_w = s_lat // WINDOW - 1
    xrow = s_lat // n_ctx
    prev = lambda i: jnp.clip(i * per - 1, 0, last_w)
    own = lambda i: jnp.minimum(i, n_lat - 1)
    nxt = lambda i: jnp.minimum((i + 1) * per, last_w)
    k_specs = [pl.BlockSpec((1, WINDOW, kw), lambda bi, i: (bi, prev(i), kcol)),
               pl.BlockSpec((1, tq, kw), lambda bi, i: (bi, own(i), kcol)),
               pl.BlockSpec((1, WINDOW, kw), lambda bi, i: (bi, nxt(i), kcol)),
               pl.BlockSpec((1, n_ctx, kw), lambda bi, i: (bi, xrow, kcol))]
    v_specs = [pl.BlockSpec((1, n_vt, LANES, WINDOW), lambda bi, i: (bi, 0, 0, prev(i))),
               pl.BlockSpec((1, n_vt, LANES, tq), lambda bi, i: (bi, 0, 0, own(i))),
               pl.BlockSpec((1, n_vt, LANES, WINDOW), lambda bi, i: (bi, 0, 0, nxt(i))),
               pl.BlockSpec((1, n_vt, LANES, n_ctx), lambda bi, i: (bi, 0, 0, xrow))]
    return pl.pallas_call(
        functools.partial(_win_kernel, n_lat),
        grid=(b, nq),
        in_specs=[pl.BlockSpec((1, tq, dq), lambda bi, i: (bi, i, 0))] + k_specs + v_specs + [_resident(sink_b.shape)],
        out_specs=pl.BlockSpec((1, tq, dq), lambda bi, i: (bi, i, 0)),
        out_shape=jax.ShapeDtypeStruct((b, t, dq), BF16),
        compiler_params=_cparams(("parallel", "arbitrary")),
        name="win_attn",
    )(qk, *([qk] * 4), *([vt] * 4), sink_b)


def _flash_kernel(n_lat_q, n_full, s_lat, n_ctx, q_ref, k_ref, vt_ref, o_ref, m_sc, acc_sc, s_sc):
    qi = pl.program_id(2)
    tq = q_ref.shape[1]
    group = AX_HEADS // AX_KV
    per = FLASH_CK // TM
    q = jnp.concatenate([q_ref[0, :, g * AX_HD:(g + 1) * AX_HD] for g in range(group)], axis=0)

    m_sc[...] = jnp.full(m_sc.shape, NEG, F32)
    acc_sc[...] = jnp.zeros(acc_sc.shape, F32)

    def scores(slot, start, size):
        s_sc[slot, :size, :] = _dot_nt(k_ref[0, pl.ds(start, size), :], q)

    def update(slot, block, size):
        s = s_sc[slot, :size, :]
        m_prev = m_sc[...]
        m_next = jnp.maximum(m_prev, jnp.max(s, axis=0, keepdims=True))
        p = jnp.exp2(s - m_next).astype(BF16)
        alpha = jnp.exp2(m_prev - m_next)
        pv = None
        for i in range(size // TM):
            part = _dot(vt_ref[0, 0, block + i], p[i * TM:(i + 1) * TM])
            pv = part if pv is None else pv + part
        acc_sc[...] = acc_sc[...] * alpha + pv
        m_sc[...] = m_next

    @pl.when(qi < n_lat_q)
    def _():
        def at(c):
            return pl.multiple_of(c * FLASH_CK, FLASH_CK)

        scores(0, 0, FLASH_CK)

        def body(i, carry):
            c = FLASH_UNROLL * i
            for u in range(FLASH_UNROLL):
                scores((u + 1) % 2, at(c + u + 1), FLASH_CK)
                update(u % 2, (c + u) * per, FLASH_CK)
            return carry

        n_iter = (n_full - 1) // FLASH_UNROLL
        lax.fori_loop(0, n_iter, body, 0)
        tail = [(c * FLASH_CK, c * per, FLASH_CK) for c in range(FLASH_UNROLL * n_iter, n_full)]
        tail.append((s_lat, s_lat // TM, n_ctx))
        for idx, (start, block, size) in enumerate(tail):
            if idx + 1 < len(tail):
                scores((idx + 1) % 2, tail[idx + 1][0], tail[idx + 1][2])
            update(idx % 2, block, size)

    @pl.when(qi >= n_lat_q)
    def _():
        scores(0, s_lat, n_ctx)
        update(0, s_lat // TM, n_ctx)

    acc = acc_sc[...]
    o = acc[:AX_HD] / acc[AX_HD:AX_HD + 1]
    for g in range(group):
        o_ref[0, :, g * AX_HD:(g + 1) * AX_HD] = o[:, g * tq:(g + 1) * tq].T.astype(BF16)


def _flash_call(qk, vt, s_lat, n_ctx):
    b, t, _ = qk.shape
    tq = FLASH_TQ
    group = AX_HEADS // AX_KV
    gw = group * AX_HD
    dq = AX_HEADS * AX_HD
    kcol = dq // AX_HD
    m_cols = group * tq
    return pl.pallas_call(
        functools.partial(_flash_kernel, s_lat // tq, s_lat // FLASH_CK, s_lat, n_ctx),
        grid=(b, AX_KV, t // tq),
        in_specs=[pl.BlockSpec((1, tq, gw), lambda bi, j, i: (bi, i, j)),
                  pl.BlockSpec((1, t, AX_HD), lambda bi, j, i: (bi, 0, kcol + j)),
                  pl.BlockSpec((1, 1) + vt.shape[2:], lambda bi, j, i: (bi, j, 0, 0, 0))],
        out_specs=pl.BlockSpec((1, tq, gw), lambda bi, j, i: (bi, i, j)),
        out_shape=jax.ShapeDtypeStruct((b, t, dq), BF16),
        scratch_shapes=[pltpu.VMEM((1, m_cols), F32), pltpu.VMEM((VT_ROWS, m_cols), F32),
                        pltpu.VMEM((2, FLASH_CK, m_cols), F32)],
        compiler_params=_cparams(("parallel", "parallel", "arbitrary")),
        name="flash_attn",
    )(qk, qk, vt)


def _mlstm_kernel(qf_ref, gf_ref, qb_ref, gb_ref, hf_ref, hb_ref, c_sc, n_sc, m_sc):
    step = pl.program_id(1)
    L = ML_CHUNK

    @pl.when(step == 0)
    def _():
        c_sc[...] = jnp.zeros(c_sc.shape, F32)
        n_sc[...] = jnp.zeros(n_sc.shape, F32)
        m_sc[...] = jnp.full(m_sc.shape, NEG, F32)

    r = lax.broadcasted_iota(jnp.int32, (L, L), 0)
    c = lax.broadcasted_iota(jnp.int32, (L, L), 1)
    lo = c <= r
    up = c >= r
    lo_f = lo.astype(F32)
    up_f = up.astype(F32)
    hi = lax.Precision.HIGHEST
    nqk = ML_HEADS * ML_DK

    dirs = ((qf_ref, gf_ref, hf_ref), (qb_ref, gb_ref, hb_ref))
    n_streams = 2 * qf_ref.shape[0]
    units = [(s, h) for s in range(n_streams) for h in range(ML_HEADS)]

    def qkv(s, h):
        x_ref, bb = dirs[s % 2][0], s // 2
        return (x_ref[bb, :, h * ML_DK:(h + 1) * ML_DK],
                x_ref[bb, :, nqk + h * ML_DK:nqk + (h + 1) * ML_DK],
                x_ref[bb, :, 2 * nqk + h * ML_DV:2 * nqk + (h + 1) * ML_DV])

    stats = []
    for s in range(n_streams):
        d = s % 2
        gates = dirs[d][1][s // 2]
        gates_t = gates.T
        tri_col, tri_row = (lo_f, up_f) if d == 0 else (up_f, lo_f)
        cum_col = jnp.dot(tri_col, gates, precision=hi, preferred_element_type=F32)
        cum_row = jnp.dot(gates_t, tri_row, precision=hi, preferred_element_type=F32)
        total = jnp.sum(gates, axis=0, keepdims=True)
        stats.append((gates, gates_t, cum_col, cum_row, total))

    s_mat, q_c = {}, {}
    for s, h in units:
        q, k, _ = qkv(s, h)
        s_mat[s, h] = _dot_nt(q, k)
        q_c[s, h] = _dot_nt(q, c_sc[s, h].astype(BF16))

    c_bar, n_bar, m_bar = {}, {}, {}
    for s, h in units:
        gates, _, cum_col, _, total = stats[s]
        ci = 2 * ML_HEADS * (s % 2) + h
        cf = ci + ML_HEADS
        _, k, v = qkv(s, h)
        a_col = total[:, cf:cf + 1] - cum_col[:, cf:cf + 1] + gates[:, ci:ci + 1]
        m_bar[s, h] = jnp.max(a_col, axis=0, keepdims=True)
        w_col = jnp.exp(a_col - m_bar[s, h])
        c_bar[s, h] = _dot_tn((v.astype(F32) * w_col).astype(BF16), k)
        n_bar[s, h] = jnp.sum(k.astype(F32) * w_col, axis=0, keepdims=True)

    num, den, m_ts = {}, {}, {}
    for s, h in units:
        _, gates_t, cum_col, cum_row, _ = stats[s]
        mask = lo if s % 2 == 0 else up
        ci = 2 * ML_HEADS * (s % 2) + h
        cf = ci + ML_HEADS
        q, _, v = qkv(s, h)
        n_prev = n_sc[s, h][0:1, :]
        m_prev = m_sc[s, h][0:1, 0:1]
        f_col = cum_col[:, cf:cf + 1]
        dmat = jnp.where(mask, f_col - cum_row[cf:cf + 1, :] + gates_t[ci:ci + 1, :], NEG)
        inter = f_col + m_prev
        m_t = jnp.maximum(inter, jnp.max(dmat, axis=-1, keepdims=True))
        w_inter = jnp.exp(inter - m_t)
        qk = s_mat[s, h] * jnp.exp(dmat - m_t)
        qn = jnp.sum(q.astype(F32) * n_prev, axis=-1, keepdims=True)
        num[s, h] = _dot(qk.astype(BF16), v) + w_inter * q_c[s, h]
        den[s, h] = jnp.sum(qk, axis=-1, keepdims=True) + w_inter * qn
        m_ts[s, h] = m_t

    for s, h in units:
        dirs[s % 2][2][s // 2, :, h * ML_DV:(h + 1) * ML_DV] = (
            num[s, h] / jnp.maximum(jnp.abs(den[s, h]), jnp.exp(-m_ts[s, h])))

    for s, h in units:
        total = stats[s][4]
        cf = 2 * ML_HEADS * (s % 2) + h + ML_HEADS
        f_tot = total[:, cf:cf + 1]
        n_prev = n_sc[s, h][0:1, :]
        m_prev = m_sc[s, h][0:1, 0:1]
        m_new = jnp.maximum(f_tot + m_prev, m_bar[s, h])
        decay = jnp.exp(f_tot + m_prev - m_new)
        inj = jnp.exp(m_bar[s, h] - m_new)
        c_sc[s, h] = decay * c_sc[s, h] + inj * c_bar[s, h]
        n_sc[s, h] = jnp.broadcast_to(decay * n_prev + inj * n_bar[s, h], (SUBLANES, ML_DK))
        m_sc[s, h] = jnp.broadcast_to(m_new, (SUBLANES, LANES))


def _mlstm_call(qkv, gates, s_lat):
    b, t, nx = qkv.shape
    nc = t // ML_CHUNK
    nlc = s_lat // ML_CHUNK
    nv = ML_HEADS * ML_DV

    def fwd(bi, s):
        return (bi, (s + nlc) % nc, 0)

    def bwd(bi, s):
        return (bi, nc - 1 - s, 0)

    nb = 1
    return pl.pallas_call(
        _mlstm_kernel,
        grid=(b // nb, nc),
        in_specs=[pl.BlockSpec((nb, ML_CHUNK, nx), fwd), pl.BlockSpec((nb, ML_CHUNK, LANES), fwd),
                  pl.BlockSpec((nb, ML_CHUNK, nx), bwd), pl.BlockSpec((nb, ML_CHUNK, LANES), bwd)],
        out_specs=(pl.BlockSpec((nb, ML_CHUNK, nv), fwd), pl.BlockSpec((nb, ML_CHUNK, nv), bwd)),
        out_shape=(jax.ShapeDtypeStruct((b, t, nv), F32), jax.ShapeDtypeStruct((b, t, nv), F32)),
        scratch_shapes=[pltpu.VMEM((2 * nb, ML_HEADS, ML_DV, ML_DK), F32),
                        pltpu.VMEM((2 * nb, ML_HEADS, SUBLANES, ML_DK), F32),
                        pltpu.VMEM((2 * nb, ML_HEADS, SUBLANES, LANES), F32)],
        compiler_params=_cparams(("parallel", "arbitrary")),
        name="mlstm",
    )(qkv, gates, qkv, gates)


def _ml_readout(hs, og, hn_ref):
    parts = []
    for h in range(ML_HEADS):
        sl = slice(h * ML_DV, (h + 1) * ML_DV)
        x = hs[:, sl]
        xn = x * lax.rsqrt(jnp.mean(x * x, axis=-1, keepdims=True) + EPS) * hn_ref[:, sl]
        parts.append((_sigmoid(og[:, sl]) * xn).astype(BF16))
    return jnp.concatenate(parts, axis=1)


def _convmlp_kernel(kind, pre, split, n_lat_tiles, n_all_tiles, *refs):
    n_pre = {None: 0, "attn": 4, "ml": 11}[pre]
    pre_refs = refs[:n_pre]
    hp_ref, h_ref, hn_ref = refs[n_pre:n_pre + 3]
    ctx_ref = refs[n_pre + 3] if split else None
    mod_ref, g_ref, w1_ref, cw_ref, w2_ref, o_ref, perm_sc = refs[n_pre + 3 + split:]
    j = pl.program_id(1)
    m = mod_ref[0, 0]
    sh, sc, gt = (m[0:1], m[1:2], m[2:3]) if kind == "sc" else (m[3:4], m[4:5], m[5:6])
    x = jnp.concatenate([hp_ref[0], _tile_rows(n_lat_tiles, h_ref, ctx_ref), hn_ref[0]], axis=0)
    rows = x.shape[0]
    if pre == "attn":
        yp_ref, y_ref, yn_ref, wo_ref = pre_refs
        y = jnp.concatenate([yp_ref[0], y_ref[0], yn_ref[0]], axis=0)
        skip = BF16_ROWS - HALO
        x = x + m[2:3] * _dot(y, wo_ref[...])[skip:skip + rows]
    elif pre == "ml":
        ext = [jnp.concatenate([pre_refs[3 * i][0], pre_refs[3 * i + 1][0], pre_refs[3 * i + 2][0]], axis=0)
               for i in range(3)]
        gain_ref, wo_ref = pre_refs[9:]
        x = x + m[2:3] * _dot(_ml_readout(ext[0] + ext[1], ext[2], gain_ref), wo_ref[...])
    a = _rms_mod(x, g_ref[...], sh, sc)
    seg_first = (j == 0) | (j == n_lat_tiles)
    seg_last = (j == n_lat_tiles - 1) | (j == n_all_tiles - 1)
    r = lax.broadcasted_iota(jnp.int32, (rows, 1), 0)
    dead = ((r < HALO) & seg_first) | ((r >= rows - HALO) & seg_last)
    a = jnp.where(dead, 0.0, a)

    pitch = rows // SUBLANES
    n_slab = a.shape[1] // LANES
    for k in range(n_slab):
        perm_sc[k] = a[:, k * LANES:(k + 1) * LANES]
    a = jnp.concatenate(
        [jnp.concatenate([perm_sc[k, pl.ds(r, SUBLANES, stride=pitch), :] for k in range(n_slab)], axis=1)
         for r in range(pitch)], axis=0).astype(BF16)

    def conv3(z, col):
        w = cw_ref[:, col:col + CONV_CHUNK]
        prev = jnp.concatenate([pltpu.roll(z[rows - SUBLANES:], 1, 0), z[:rows - SUBLANES]], axis=0)
        nxt = jnp.concatenate([z[SUBLANES:], pltpu.roll(z[:SUBLANES], SUBLANES - 1, 0)], axis=0)
        return prev * w[0:1] + z * w[1:2] + nxt * w[2:3]

    hidden = w2_ref.shape[0]
    parts = w1_ref.shape[1] // hidden

    def up(c):
        return [_dot(a, w1_ref[:, p * hidden + c * CONV_CHUNK:p * hidden + (c + 1) * CONV_CHUNK]) for p in range(parts)]

    n_chunks = hidden // CONV_CHUNK
    acc = None
    u_next = up(0)
    for c in range(n_chunks):
        u = u_next
        if c + 1 < n_chunks:
            u_next = up(c + 1)
        if kind == "sc":
            hid = u[0] * conv3(u[1] * u[2], c * CONV_CHUNK)
        else:
            gg = conv3(u[0], c * CONV_CHUNK)
            uu = conv3(u[1], hidden + c * CONV_CHUNK)
            hid = gg * _sigmoid(gg) * uu
        y = _dot(hid.astype(BF16), w2_ref[c * CONV_CHUNK:(c + 1) * CONV_CHUNK, :])
        acc = y if acc is None else acc + y

    y = gt * acc
    for r in range(pitch):
        for k in range(n_slab):
            perm_sc[k, pl.ds(r, SUBLANES, stride=pitch), :] = y[r * SUBLANES:(r + 1) * SUBLANES, k * LANES:(k + 1) * LANES]
    o_ref[0] = x[HALO:rows - HALO] + jnp.concatenate([perm_sc[k, HALO:rows - HALO, :] for k in range(n_slab)], axis=1)


def _halo_specs(t, width, halo):
    per = TM // halo
    last = t // halo - 1
    last_tile = t // TM - 1
    return [pl.BlockSpec((1, halo, width), lambda bi, j: (bi, jnp.maximum(j * per - 1, 0), 0)),
            pl.BlockSpec((1, TM, width), lambda bi, j: (bi, jnp.minimum(j, last_tile), 0)),
            pl.BlockSpec((1, halo, width), lambda bi, j: (bi, jnp.minimum((j + 1) * per, last), 0))]


def _convmlp_call(kind, h, ctx, mods, layer, gain, w1, cw, w2, n_lat_tiles, n_all_tiles, n_tiles, pre=None, pre_ins=()):
    b, t_h, d = h.shape
    t = t_h + (0 if ctx is None else ctx.shape[1])
    ctx_row = b
    ctx_specs = [] if ctx is None else [pl.BlockSpec((1, TM, d), lambda bi, j: (bi, 0, 0))]
    ctx_args = () if ctx is None else (ctx,)
    if pre == "attn":
        y, w_o = pre_ins
        pre_specs = _halo_specs(t, y.shape[2], BF16_ROWS) + [_resident(w_o.shape)]
        pre_args = (y, y, y, w_o)
    elif pre == "ml":
        hf, hb, og, hn, w_o = pre_ins
        pre_specs = _halo_specs(t, hf.shape[2], HALO) * 3 + [_resident(hn.shape), _resident(w_o.shape)]
        pre_args = (hf, hf, hf, hb, hb, hb, og, og, og, hn, w_o)
    else:
        pre_specs, pre_args = [], ()
    return pl.pallas_call(
        functools.partial(_convmlp_kernel, kind, pre, ctx is not None, n_lat_tiles, n_all_tiles),
        grid=(b, n_tiles),
        in_specs=pre_specs + _halo_specs(t_h, d, HALO) + ctx_specs + [
            _mod_spec(n_lat_tiles, ctx_row, layer),
            _resident((1, d)), _resident(w1.shape), _resident(cw.shape), _resident(w2.shape)],
        out_specs=pl.BlockSpec((1, TM, d), lambda bi, j: (bi, j, 0)),
        out_shape=jax.ShapeDtypeStruct((b, n_tiles * TM, d), F32),
        scratch_shapes=[pltpu.VMEM((d // LANES, TM + 2 * HALO, LANES), F32)],
        compiler_params=_cparams(("parallel", "arbitrary")),
        name="convmlp_" + kind + ("_" + pre if pre else ""),
    )(*pre_args, h, h, h, *ctx_args, mods, gain, w1, cw, w2)


def _rope_tables(s_lat, n_ctx, hd, reps):
    rows = s_lat // GRID_W
    row = np.repeat(np.arange(rows, dtype=np.float32), GRID_W)
    col = np.tile(np.arange(GRID_W, dtype=np.float32), rows)
    n_freq = hd // 4
    inv = jnp.power(ROPE_THETA, -jnp.arange(n_freq, dtype=F32) / n_freq)
    ang = jnp.concatenate([jnp.asarray(row)[:, None] * inv, jnp.asarray(col)[:, None] * inv], axis=-1)
    cos, sin = jnp.cos(ang), jnp.sin(ang)
    cos = jnp.tile(jnp.concatenate([cos, cos], axis=1), (1, reps))
    sin = jnp.tile(jnp.concatenate([-sin, sin], axis=1), (1, reps))
    cos = jnp.concatenate([cos, jnp.ones((n_ctx, LANES), F32)], axis=0)
    sin = jnp.concatenate([sin, jnp.zeros((n_ctx, LANES), F32)], axis=0)
    return cos, sin


def kernel(x, c, ctx, c_ctx, ada_w, ada_b, norm_mix, norm_ffn, ffn_w_up, ffn_conv, ffn_w_down, win_w_qkv, win_q_norm, win_k_norm, win_sink, win_w_o, sc_w_in, sc_conv, sc_w_out, ax_w_qkv, ax_q_norm, ax_k_norm, ax_w_o, ml_w_in, ml_b_gate, ml_h_norm, ml_w_out):
    b, s_lat, d = x.shape
    n_ctx = ctx.shape[1]
    assert d == D_MODEL and n_ctx == TM and s_lat % FLASH_CK == 0 and s_lat % GRID_W == 0
    assert FLASH_CK % TM == 0 and n_ctx % FLASH_TQ == 0 and n_ctx % ML_CHUNK == 0
    assert n_ctx % WIN_TQ == 0 and WIN_TQ % WINDOW == 0
    n_lat_tiles = s_lat // TM
    n_all_tiles = n_lat_tiles + n_ctx // TM
    depth = ada_w.shape[0]

    h, hc = x, ctx
    pad_rows = -(b + 1) % SUBLANES
    cvec = jnp.concatenate([c, c_ctx[None], jnp.zeros((pad_rows, d), F32)], axis=0)
    mods = _ada_call(cvec, ada_w, ada_b).reshape(depth, b + 1 + pad_rows, 6, d)

    for i in range(depth):
        kind, j = i % N_MIXERS, i // N_MIXERS
        last = i == depth - 1
        n_tiles = n_lat_tiles if last else n_all_tiles
        g_mix = norm_mix[i][None]
        pre, pre_ins = None, ()
        if kind == 0:
            gain = jnp.concatenate([jnp.tile(win_q_norm[j], WIN_HEADS) * WIN_HD ** -0.5,
                                    jnp.tile(win_k_norm[j], WIN_KV)])[None]
            cos, sin = _rope_tables(s_lat, n_ctx, WIN_HD, 2)
            qk, vt = _inproj_call("win", h, hc, mods, i, g_mix, win_w_qkv[j].astype(BF16), (gain, cos, sin), n_lat_tiles)
            sink_b = jnp.broadcast_to(win_sink[j][:, None], (WIN_HEADS, WIN_TQ))
            pre, pre_ins = "attn", (_win_call(qk, vt, sink_b, s_lat, n_ctx), win_w_o[j].astype(BF16))
        elif kind == 1:
            h = _convmlp_call("sc", h, hc, mods, i, g_mix, sc_w_in[j].astype(BF16), sc_conv[j], sc_w_out[j].astype(BF16),
                              n_lat_tiles, n_all_tiles, n_tiles)
            hc = None
        elif kind == 2:
            gain = jnp.concatenate([jnp.tile(ax_q_norm[j], AX_HEADS) * (AX_HD ** -0.5 * np.log2(np.e)),
                                    jnp.tile(ax_k_norm[j], AX_KV)])[None]
            cos, sin = _rope_tables(s_lat, n_ctx, AX_HD, 1)
            qk, vt = _inproj_call("ax", h, hc, mods, i, g_mix, ax_w_qkv[j].astype(BF16), (gain, cos, sin), n_lat_tiles)
            pre, pre_ins = "attn", (_flash_call(qk, vt, s_lat, n_ctx), ax_w_o[j].astype(BF16))
        else:
            w = jnp.concatenate([ml_w_in[j], jnp.zeros((d, LANES - 4 * ML_HEADS), F32)], axis=1).astype(BF16)
            bg = jnp.concatenate([ml_b_gate[j], jnp.zeros((LANES - 4 * ML_HEADS,), F32)])[None]
            qkv, og, gates = _inproj_call("ml", h, hc, mods, i, g_mix, w, (bg,), n_lat_tiles)
            hf, hb = _mlstm_call(qkv, gates, s_lat)
            hn = jnp.tile(ml_h_norm[j], ML_HEADS)[None]
            pre, pre_ins = "ml", (hf, hb, og, hn, ml_w_out[j].astype(BF16))
        h = _convmlp_call("ffn", h, hc, mods, i, norm_ffn[i][None], ffn_w_up[i].astype(BF16), ffn_conv[i], ffn_w_down[i].astype(BF16),
                          n_lat_tiles, n_all_tiles, n_tiles, pre, pre_ins)
        hc = None
    return h[:, :s_lat] if h.shape[1] != s_lat else h
```

```python
import functools

import numpy as np
import jax
import jax.numpy as jnp
from jax import lax
from jax.experimental import pallas as pl
from jax.experimental.pallas import tpu as pltpu

D_MODEL = 1024
GRID_W = 64
N_MIXERS = 4
WINDOW = 128
WIN_HEADS = 16
WIN_KV = 4
WIN_HD = 64
AX_HEADS = 8
AX_KV = 2
AX_HD = 128
ML_HEADS = 4
ML_DK = 128
ML_DV = 256
ML_CHUNK = 256
ROPE_THETA = 10000.0
EPS = 1e-6
NEG = -1e30

F32 = jnp.float32
BF16 = jnp.bfloat16

LANES = 128
SUBLANES = 8
TM = 256
HALO = SUBLANES
CONV_CHUNK = 256
PROJ_CHUNK = 256
FLASH_TQ = 256
FLASH_CK = 512
FLASH_UNROLL = 8
WIN_TQ = 256
BF16_ROWS = 16
VT_ROWS = AX_HD + BF16_ROWS
V7X_VMEM_BYTES = 64 * 1024 * 1024
VMEM_LIMIT = V7X_VMEM_BYTES * 7 // 8


def _cparams(sem):
    return pltpu.CompilerParams(dimension_semantics=sem, vmem_limit_bytes=VMEM_LIMIT)


def _resident(shape):
    nd = len(shape)
    return pl.BlockSpec(shape, lambda *_: (0,) * nd, pipeline_mode=pl.Buffered(1))


def _dot(a, b):
    return jnp.dot(a, b, preferred_element_type=F32)


def _dot_nt(a, b):
    return lax.dot_general(a, b, (((1,), (1,)), ((), ())), preferred_element_type=F32)


def _dot_tn(a, b):
    return lax.dot_general(a, b, (((0,), (0,)), ((), ())), preferred_element_type=F32)


def _sigmoid(x):
    return 1.0 / (1.0 + jnp.exp(-x))


def _rms_mod(x, g, shift, scale):
    y = x * lax.rsqrt(jnp.mean(x * x, axis=-1, keepdims=True) + EPS) * g
    return y * (1.0 + scale) + shift


def _ada_kernel(c_ref, w_ref, b_ref, o_ref):
    c = c_ref[...]
    o_ref[0] = _dot(c * _sigmoid(c), w_ref[0]) + b_ref[0]


def _ada_call(cvec, ada_w, ada_b):
    depth, d, n = ada_w.shape
    nb = n // 4
    return pl.pallas_call(
        _ada_kernel,
        grid=(depth, n // nb),
        in_specs=[pl.BlockSpec(cvec.shape, lambda l, j: (0, 0)),
                  pl.BlockSpec((1, d, nb), lambda l, j: (l, 0, j)),
                  pl.BlockSpec((1, 1, nb), lambda l, j: (l, 0, j))],
        out_specs=pl.BlockSpec((1, cvec.shape[0], nb), lambda l, j: (l, 0, j)),
        out_shape=jax.ShapeDtypeStruct((depth, cvec.shape[0], n), F32),
        compiler_params=_cparams(("arbitrary", "arbitrary")),
        name="ada",
    )(cvec, ada_w, ada_b.reshape(depth, 1, n))


def _tile_rows(n_lat_tiles, h_ref, ctx_ref):
    if ctx_ref is None:
        return h_ref[0]
    return jnp.where(pl.program_id(1) < n_lat_tiles, h_ref[0], ctx_ref[0])


def _inproj_kernel(kind, n_lat_tiles, split, h_ref, *refs):
    ctx_ref = refs[0] if split else None
    mod_ref, g_ref, w_ref, *rest = refs[1:] if split else refs
    m = mod_ref[0, 0]
    a = _rms_mod(_tile_rows(n_lat_tiles, h_ref, ctx_ref), g_ref[...], m[0:1], m[1:2]).astype(BF16)
    tm = a.shape[0]
    lane = lax.broadcasted_iota(jnp.int32, (tm, LANES), 1)

    if kind in ("win", "ax"):
        gain_ref, cos_ref, sin_ref, o_ref, vt_ref = rest
        cos = cos_ref[...]
        sin = sin_ref[...]
        n_rot = gain_ref.shape[1] // LANES
        if kind == "win":
            head_sum = (lax.broadcasted_iota(jnp.int32, (LANES, LANES), 0) // WIN_HD
                        == lax.broadcasted_iota(jnp.int32, (LANES, LANES), 1) // WIN_HD).astype(BF16)

        def emit(t, xt):
            sl = slice(t * LANES, (t + 1) * LANES)
            if t >= n_rot:
                if kind == "win":
                    vt_ref[0, t - n_rot] = xt.T.astype(BF16)
                else:
                    vt_ref[0, t - n_rot, 0, :AX_HD, :] = xt.T.astype(BF16)
                    vt_ref[0, t - n_rot, 0, AX_HD:, :] = jnp.ones((VT_ROWS - AX_HD, tm), BF16)
                return
            x2 = xt * xt
            if kind == "win":
                hi = x2.astype(BF16)
                lo = (x2 - hi.astype(F32)).astype(BF16)
                ms = (_dot(hi, head_sum) + _dot(lo, head_sum)) * (1.0 / WIN_HD)
                xn = xt * lax.rsqrt(ms + EPS) * gain_ref[:, sl]
                first = (lane % WIN_HD) < (WIN_HD // 2)
                sw = jnp.where(first, pltpu.roll(xn, LANES - WIN_HD // 2, 1), pltpu.roll(xn, WIN_HD // 2, 1))
            else:
                ms = jnp.mean(x2, axis=-1, keepdims=True)
                xn = xt * lax.rsqrt(ms + EPS) * gain_ref[:, sl]
                sw = pltpu.roll(xn, AX_HD // 2, 1)
            o_ref[0, :, sl] = (xn * cos + sw * sin).astype(BF16)
    else:
        bg_ref, qkv_ref, og_ref, gate_ref = rest
        nq = ML_HEADS * ML_DK // LANES
        nqkv = nq * 2 + ML_HEADS * ML_DV // LANES
        nog = nqkv + ML_HEADS * ML_DV // LANES

        def emit(t, xt):
            sl = slice(t * LANES, (t + 1) * LANES)
            if t < nq:
                qkv_ref[0, :, sl] = (xt * ML_DK ** -0.5).astype(BF16)
            elif t < nqkv:
                qkv_ref[0, :, sl] = xt.astype(BF16)
            elif t < nog:
                og_ref[0, :, (t - nqkv) * LANES:(t - nqkv + 1) * LANES] = xt
            else:
                g = xt + bg_ref[...]
                is_forget = (lane % (2 * ML_HEADS)) >= ML_HEADS
                log_sig = jnp.minimum(g, 0.0) - jnp.log1p(jnp.exp(-jnp.abs(g)))
                gate_ref[0] = jnp.where(is_forget, log_sig, g)

    n_out = w_ref.shape[1]
    starts = list(range(0, n_out, PROJ_CHUNK))

    def proj(c):
        return _dot(a, w_ref[:, starts[c]:min(starts[c] + PROJ_CHUNK, n_out)])

    y_next = proj(0)
    for c in range(len(starts)):
        y = y_next
        if c + 1 < len(starts):
            y_next = proj(c + 1)
        for i in range(y.shape[1] // LANES):
            emit(starts[c] // LANES + i, y[:, i * LANES:(i + 1) * LANES])


def _mod_spec(n_lat_tiles, ctx_row, layer):
    return pl.BlockSpec((1, 1, 6, D_MODEL), lambda b, j: (layer, jnp.where(j < n_lat_tiles, b, ctx_row), 0, 0))


def _stream_specs(h, ctx, n_lat_tiles):
    d = h.shape[2]
    if ctx is None:
        return h.shape[1], (h,), [pl.BlockSpec((1, TM, d), lambda bi, j: (bi, j, 0))]
    return (h.shape[1] + ctx.shape[1], (h, ctx),
            [pl.BlockSpec((1, TM, d), lambda bi, j: (bi, jnp.minimum(j, n_lat_tiles - 1), 0)),
             pl.BlockSpec((1, TM, d), lambda bi, j: (bi, 0, 0))])


def _inproj_call(kind, h, ctx, mods, layer, gain, w, extras, n_lat_tiles):
    b, _, d = h.shape
    n = w.shape[1]
    ctx_row = b
    t, streams, stream_specs = _stream_specs(h, ctx, n_lat_tiles)
    in_specs = stream_specs + [
                _mod_spec(n_lat_tiles, ctx_row, layer),
                _resident((1, d)),
                _resident(w.shape)]
    if kind in ("win", "ax"):
        hgain, cos, sin = extras
        in_specs += [_resident(hgain.shape),
                     pl.BlockSpec((TM, LANES), lambda bi, j: (j, 0)),
                     pl.BlockSpec((TM, LANES), lambda bi, j: (j, 0))]
        if kind == "win":
            n_v = WIN_KV * WIN_HD // LANES
            n_qk = n - n_v * LANES
            out_shape = (jax.ShapeDtypeStruct((b, t, n_qk), BF16),
                         jax.ShapeDtypeStruct((b, n_v, LANES, t), BF16))
            out_specs = (pl.BlockSpec((1, TM, n_qk), lambda bi, j: (bi, j, 0)),
                         pl.BlockSpec((1, n_v, LANES, TM), lambda bi, j: (bi, 0, 0, j)))
        else:
            n_qk = n - AX_KV * AX_HD
            out_shape = (jax.ShapeDtypeStruct((b, t, n_qk), BF16),
                         jax.ShapeDtypeStruct((b, AX_KV, t // TM, VT_ROWS, TM), BF16))
            out_specs = (pl.BlockSpec((1, TM, n_qk), lambda bi, j: (bi, j, 0)),
                         pl.BlockSpec((1, AX_KV, 1, VT_ROWS, TM), lambda bi, j: (bi, 0, j, 0, 0)))
        args = (hgain, cos, sin)
    else:
        (bg,) = extras
        nqkv = 2 * ML_HEADS * ML_DK + ML_HEADS * ML_DV
        nv = ML_HEADS * ML_DV
        in_specs += [_resident(bg.shape)]
        out_shape = (jax.ShapeDtypeStruct((b, t, nqkv), BF16),
                     jax.ShapeDtypeStruct((b, t, nv), F32),
                     jax.ShapeDtypeStruct((b, t, LANES), F32))
        out_specs = (pl.BlockSpec((1, TM, nqkv), lambda bi, j: (bi, j, 0)),
                     pl.BlockSpec((1, TM, nv), lambda bi, j: (bi, j, 0)),
                     pl.BlockSpec((1, TM, LANES), lambda bi, j: (bi, j, 0)))
        args = (bg,)
    return pl.pallas_call(
        functools.partial(_inproj_kernel, kind, n_lat_tiles, ctx is not None),
        grid=(b, t // TM),
        in_specs=in_specs, out_specs=out_specs, out_shape=out_shape,
        compiler_params=_cparams(("parallel", "arbitrary")),
        name="inproj_" + kind,
    )(*streams, mods, gain, w, *args)


def _win_heads(q_ref, k, vt, bias, sink_ref, o_ref):
    tq = q_ref.shape[1]
    lane = lax.broadcasted_iota(jnp.int32, (tq, LANES), 1)
    lo = lane < WIN_HD
    group = WIN_HEADS // WIN_KV
    if bias is not None:
        bias = jnp.concatenate([bias] * group, axis=1)

    def scores(kv):
        kt, khalf = kv // 2, kv % 2
        qs = []
        for g in range(group):
            head = kv * group + g
            tile, half = head // 2, head % 2
            qt = q_ref[0, :, tile * LANES:(tile + 1) * LANES].astype(F32)
            qh = jnp.where(lo, qt, 0.0) if half == 0 else jnp.where(lo, 0.0, qt)
            if half != khalf:
                qh = pltpu.roll(qh, WIN_HD, 1)
            qs.append(qh.astype(BF16))
        return _dot_nt(k[:, kt * LANES:(kt + 1) * LANES], jnp.concatenate(qs, axis=0))

    s_next = scores(0)
    for kv in range(WIN_KV):
        kt, khalf = kv // 2, kv % 2
        s = s_next
        if kv + 1 < WIN_KV:
            s_next = scores(kv + 1)
        if bias is not None:
            s = s + bias
        sink = jnp.concatenate([sink_ref[kv * group + g:kv * group + g + 1, :tq] for g in range(group)], axis=1)
        mx = jnp.maximum(jnp.max(s, axis=0, keepdims=True), sink)
        e = jnp.exp2(s - mx)
        rows = slice(khalf * WIN_HD, (khalf + 1) * WIN_HD)
        other = (1 - khalf) * WIN_HD
        own = (lax.broadcasted_iota(jnp.int32, vt[kt].shape, 0) // WIN_HD) == khalf
        o = _dot(jnp.where(own, vt[kt], jnp.ones_like(vt[kt])), e.astype(BF16))
        o = o / (o[other:other + 1] + jnp.exp2(sink - mx))
        for pair in range(group // 2):
            both = jnp.concatenate([o[rows, (2 * pair + half) * tq:(2 * pair + half + 1) * tq] for half in range(2)],
                                   axis=0)
            tile = (kv * group) // 2 + pair
            o_ref[0, :, tile * LANES:(tile + 1) * LANES] = both.T.astype(BF16)


def _win_kernel(n_lat, q_ref, kp, kc, kn, kx, vp, vc, vn, vx, sink_ref, o_ref):
    i = pl.program_id(1)
    tq = q_ref.shape[1]
    n_ctx = kx.shape[1]

    @pl.when(i < n_lat)
    def _():
        w = WINDOW
        nk = tq + 2 * w + n_ctx
        col = lax.broadcasted_iota(jnp.int32, (nk, 1), 0)
        t = lax.broadcasted_iota(jnp.int32, (nk, tq), 0) - lax.broadcasted_iota(jnp.int32, (nk, tq), 1)
        big = 4 * nk
        t_min = jnp.where(col < w, jnp.where(i > 0, 0, big), jnp.where(col < tq + 2 * w, 0, -big))
        t_max = jnp.where(col >= tq + 2 * w, big, jnp.where(col >= tq + w, jnp.where(i < n_lat - 1, 2 * w, -big), 2 * w))
        bias = jnp.where(t < t_min, NEG, 0.0) + jnp.where(t > t_max, NEG, 0.0)
        k = jnp.concatenate([kp[0], kc[0], kn[0], kx[0]], axis=0)
        vt = [jnp.concatenate([vp[0, kt], vc[0, kt], vn[0, kt], vx[0, kt]], axis=1) for kt in range(vx.shape[1])]
        _win_heads(q_ref, k, vt, bias, sink_ref, o_ref)

    @pl.when(i >= n_lat)
    def _():
        _win_heads(q_ref, kx[0], [vx[0, kt] for kt in range(vx.shape[1])], None, sink_ref, o_ref)


def _win_call(qk, vt, sink_b, s_lat, n_ctx):
    b, t, _ = qk.shape
    tq = WIN_TQ
    n_lat = s_lat // tq
    nq = t // tq
    dq = WIN_HEADS * WIN_HD
    kw = WIN_KV * WIN_HD
    kcol = dq // kw
    n_vt = vt.shape[1]
    per = tq // WINDOW
    last_w = s_lat // WINDOW - 1
    xrow = s_lat // n_ctx
    prev = lambda i: jnp.clip(i * per - 1, 0, last_w)
    own = lambda i: jnp.minimum(i, n_lat - 1)
    nxt = lambda i: jnp.minimum((i + 1) * per, last_w)
    k_specs = [pl.BlockSpec((1, WINDOW, kw), lambda bi, i: (bi, prev(i), kcol)),
               pl.BlockSpec((1, tq, kw), lambda bi, i: (bi, own(i), kcol)),
               pl.BlockSpec((1, WINDOW, kw), lambda bi, i: (bi, nxt(i), kcol)),
               pl.BlockSpec((1, n_ctx, kw), lambda bi, i: (bi, xrow, kcol))]
    v_specs = [pl.BlockSpec((1, n_vt, LANES, WINDOW), lambda bi, i: (bi, 0, 0, prev(i))),
               pl.BlockSpec((1, n_vt, LANES, tq), lambda bi, i: (bi, 0, 0, own(i))),
               pl.BlockSpec((1, n_vt, LANES, WINDOW), lambda bi, i: (bi, 0, 0, nxt(i))),
               pl.BlockSpec((1, n_vt, LANES, n_ctx), lambda bi, i: (bi, 0, 0, xrow))]
    return pl.pallas_call(
        functools.partial(_win_kernel, n_lat),
        grid=(b, nq),
        in_specs=[pl.BlockSpec((1, tq, dq), lambda bi, i: (bi, i, 0))] + k_specs + v_specs + [_resident(sink_b.shape)],
        out_specs=pl.BlockSpec((1, tq, dq), lambda bi, i: (bi, i, 0)),
        out_shape=jax.ShapeDtypeStruct((b, t, dq), BF16),
        compiler_params=_cparams(("parallel", "arbitrary")),
        name="win_attn",
    )(qk, *([qk] * 4), *([vt] * 4), sink_b)


def _flash_kernel(n_lat_q, n_full, s_lat, n_ctx, q_ref, k_ref, vt_ref, o_ref, m_sc, acc_sc, s_sc):
    qi = pl.program_id(2)
    tq = q_ref.shape[1]
    group = AX_HEADS // AX_KV
    per = FLASH_CK // TM
    q = jnp.concatenate([q_ref[0, :, g * AX_HD:(g + 1) * AX_HD] for g in range(group)], axis=0)

    m_sc[...] = jnp.full(m_sc.shape, NEG, F32)
    acc_sc[...] = jnp.zeros(acc_sc.shape, F32)

    def scores(slot, start, size):
        s_sc[slot, :size, :] = _dot_nt(k_ref[0, pl.ds(start, size), :], q)

    def update(slot, block, size):
        s = s_sc[slot, :size, :]
        m_prev = m_sc[...]
        m_next = jnp.maximum(m_prev, jnp.max(s, axis=0, keepdims=True))
        p = jnp.exp2(s - m_next).astype(BF16)
        alpha = jnp.exp2(m_prev - m_next)
        pv = None
        for i in range(size // TM):
            part = _dot(vt_ref[0, 0, block + i], p[i * TM:(i + 1) * TM])
            pv = part if pv is None else pv + part
        acc_sc[...] = acc_sc[...] * alpha + pv
        m_sc[...] = m_next

    @pl.when(qi < n_lat_q)
    def _():
        def at(c):
            return pl.multiple_of(c * FLASH_CK, FLASH_CK)

        scores(0, 0, FLASH_CK)

        def body(i, carry):
            c = FLASH_UNROLL * i
            for u in range(FLASH_UNROLL):
                scores((u + 1) % 2, at(c + u + 1), FLASH_CK)
                update(u % 2, (c + u) * per, FLASH_CK)
            return carry

        n_iter = (n_full - 1) // FLASH_UNROLL
        lax.fori_loop(0, n_iter, body, 0)
        tail = [(c * FLASH_CK, c * per, FLASH_CK) for c in range(FLASH_UNROLL * n_iter, n_full)]
        tail.append((s_lat, s_lat // TM, n_ctx))
        for idx, (start, block, size) in enumerate(tail):
            if idx + 1 < len(tail):
                scores((idx + 1) % 2, tail[idx + 1][0], tail[idx + 1][2])
            update(idx % 2, block, size)

    @pl.when(qi >= n_lat_q)
    def _():
        scores(0, s_lat, n_ctx)
        update(0, s_lat // TM, n_ctx)

    acc = acc_sc[...]
    o = acc[:AX_HD] / acc[AX_HD:AX_HD + 1]
    for g in range(group):
        o_ref[0, :, g * AX_HD:(g + 1) * AX_HD] = o[:, g * tq:(g + 1) * tq].T.astype(BF16)


def _flash_call(qk, vt, s_lat, n_ctx):
    b, t, _ = qk.shape
    tq = FLASH_TQ
    group = AX_HEADS // AX_KV
    gw = group * AX_HD
    dq = AX_HEADS * AX_HD
    kcol = dq // AX_HD
    m_cols = group * tq
    return pl.pallas_call(
        functools.partial(_flash_kernel, s_lat // tq, s_lat // FLASH_CK, s_lat, n_ctx),
        grid=(b, AX_KV, t // tq),
        in_specs=[pl.BlockSpec((1, tq, gw), lambda bi, j, i: (bi, i, j)),
                  pl.BlockSpec((1, t, AX_HD), lambda bi, j, i: (bi, 0, kcol + j)),
                  pl.BlockSpec((1, 1) + vt.shape[2:], lambda bi, j, i: (bi, j, 0, 0, 0))],
        out_specs=pl.BlockSpec((1, tq, gw), lambda bi, j, i: (bi, i, j)),
        out_shape=jax.ShapeDtypeStruct((b, t, dq), BF16),
        scratch_shapes=[pltpu.VMEM((1, m_cols), F32), pltpu.VMEM((VT_ROWS, m_cols), F32),
                        pltpu.VMEM((2, FLASH_CK, m_cols), F32)],
        compiler_params=_cparams(("parallel", "parallel", "arbitrary")),
        name="flash_attn",
    )(qk, qk, vt)


def _mlstm_kernel(qf_ref, gf_ref, qb_ref, gb_ref, hf_ref, hb_ref, c_sc, n_sc, m_sc):
    step = pl.program_id(1)
    L = ML_CHUNK

    @pl.when(step == 0)
    def _():
        c_sc[...] = jnp.zeros(c_sc.shape, F32)
        n_sc[...] = jnp.zeros(n_sc.shape, F32)
        m_sc[...] = jnp.full(m_sc.shape, NEG, F32)

    r = lax.broadcasted_iota(jnp.int32, (L, L), 0)
    c = lax.broadcasted_iota(jnp.int32, (L, L), 1)
    lo = c <= r
    up = c >= r
    lo_f = lo.astype(F32)
    up_f = up.astype(F32)
    hi = lax.Precision.HIGHEST
    nqk = ML_HEADS * ML_DK

    dirs = ((qf_ref, gf_ref, hf_ref), (qb_ref, gb_ref, hb_ref))
    n_streams = 2 * qf_ref.shape[0]
    units = [(s, h) for s in range(n_streams) for h in range(ML_HEADS)]

    def qkv(s, h):
        x_ref, bb = dirs[s % 2][0], s // 2
        return (x_ref[bb, :, h * ML_DK:(h + 1) * ML_DK],
                x_ref[bb, :, nqk + h * ML_DK:nqk + (h + 1) * ML_DK],
                x_ref[bb, :, 2 * nqk + h * ML_DV:2 * nqk + (h + 1) * ML_DV])

    stats = []
    for s in range(n_streams):
        d = s % 2
        gates = dirs[d][1][s // 2]
        gates_t = gates.T
        tri_col, tri_row = (lo_f, up_f) if d == 0 else (up_f, lo_f)
        cum_col = jnp.dot(tri_col, gates, precision=hi, preferred_element_type=F32)
        cum_row = jnp.dot(gates_t, tri_row, precision=hi, preferred_element_type=F32)
        total = jnp.sum(gates, axis=0, keepdims=True)
        stats.append((gates, gates_t, cum_col, cum_row, total))

    s_mat, q_c = {}, {}
    for s, h in units:
        q, k, _ = qkv(s, h)
        s_mat[s, h] = _dot_nt(q, k)
        q_c[s, h] = _dot_nt(q, c_sc[s, h].astype(BF16))

    c_bar, n_bar, m_bar = {}, {}, {}
    for s, h in units:
        gates, _, cum_col, _, total = stats[s]
        ci = 2 * ML_HEADS * (s % 2) + h
        cf = ci + ML_HEADS
        _, k, v = qkv(s, h)
        a_col = total[:, cf:cf + 1] - cum_col[:, cf:cf + 1] + gates[:, ci:ci + 1]
        m_bar[s, h] = jnp.max(a_col, axis=0, keepdims=True)
        w_col = jnp.exp(a_col - m_bar[s, h])
        c_bar[s, h] = _dot_tn((v.astype(F32) * w_col).astype(BF16), k)
        n_bar[s, h] = jnp.sum(k.astype(F32) * w_col, axis=0, keepdims=True)

    num, den, m_ts = {}, {}, {}
    for s, h in units:
        _, gates_t, cum_col, cum_row, _ = stats[s]
        mask = lo if s % 2 == 0 else up
        ci = 2 * ML_HEADS * (s % 2) + h
        cf = ci + ML_HEADS
        q, _, v = qkv(s, h)
        n_prev = n_sc[s, h][0:1, :]
        m_prev = m_sc[s, h][0:1, 0:1]
        f_col = cum_col[:, cf:cf + 1]
        dmat = jnp.where(mask, f_col - cum_row[cf:cf + 1, :] + gates_t[ci:ci + 1, :], NEG)
        inter = f_col + m_prev
        m_t = jnp.maximum(inter, jnp.max(dmat, axis=-1, keepdims=True))
        w_inter = jnp.exp(inter - m_t)
        qk = s_mat[s, h] * jnp.exp(dmat - m_t)
        qn = jnp.sum(q.astype(F32) * n_prev, axis=-1, keepdims=True)
        num[s, h] = _dot(qk.astype(BF16), v) + w_inter * q_c[s, h]
        den[s, h] = jnp.sum(qk, axis=-1, keepdims=True) + w_inter * qn
        m_ts[s, h] = m_t

    for s, h in units:
        dirs[s % 2][2][s // 2, :, h * ML_DV:(h + 1) * ML_DV] = (
            num[s, h] / jnp.maximum(jnp.abs(den[s, h]), jnp.exp(-m_ts[s, h])))

    for s, h in units:
        total = stats[s][4]
        cf = 2 * ML_HEADS * (s % 2) + h + ML_HEADS
        f_tot = total[:, cf:cf + 1]
        n_prev = n_sc[s, h][0:1, :]
        m_prev = m_sc[s, h][0:1, 0:1]
        m_new = jnp.maximum(f_tot + m_prev, m_bar[s, h])
        decay = jnp.exp(f_tot + m_prev - m_new)
        inj = jnp.exp(m_bar[s, h] - m_new)
        c_sc[s, h] = decay * c_sc[s, h] + inj * c_bar[s, h]
        n_sc[s, h] = jnp.broadcast_to(decay * n_prev + inj * n_bar[s, h], (SUBLANES, ML_DK))
        m_sc[s, h] = jnp.broadcast_to(m_new, (SUBLANES, LANES))


def _mlstm_call(qkv, gates, s_lat):
    b, t, nx = qkv.shape
    nc = t // ML_CHUNK
    nlc = s_lat // ML_CHUNK
    nv = ML_HEADS * ML_DV

    def fwd(bi, s):
        return (bi, (s + nlc) % nc, 0)

    def bwd(bi, s):
        return (bi, nc - 1 - s, 0)

    nb = 1
    return pl.pallas_call(
        _mlstm_kernel,
        grid=(b // nb, nc),
        in_specs=[pl.BlockSpec((nb, ML_CHUNK, nx), fwd), pl.BlockSpec((nb, ML_CHUNK, LANES), fwd),
                  pl.BlockSpec((nb, ML_CHUNK, nx), bwd), pl.BlockSpec((nb, ML_CHUNK, LANES), bwd)],
        out_specs=(pl.BlockSpec((nb, ML_CHUNK, nv), fwd), pl.BlockSpec((nb, ML_CHUNK, nv), bwd)),
        out_shape=(jax.ShapeDtypeStruct((b, t, nv), F32), jax.ShapeDtypeStruct((b, t, nv), F32)),
        scratch_shapes=[pltpu.VMEM((2 * nb, ML_HEADS, ML_DV, ML_DK), F32),
                        pltpu.VMEM((2 * nb, ML_HEADS, SUBLANES, ML_DK), F32),
                        pltpu.VMEM((2 * nb, ML_HEADS, SUBLANES, LANES), F32)],
        compiler_params=_cparams(("parallel", "arbitrary")),
        name="mlstm",
    )(qkv, gates, qkv, gates)


def _ml_readout(hs, og, hn_ref):
    parts = []
    for h in range(ML_HEADS):
        sl = slice(h * ML_DV, (h + 1) * ML_DV)
        x = hs[:, sl]
        xn = x * lax.rsqrt(jnp.mean(x * x, axis=-1, keepdims=True) + EPS) * hn_ref[:, sl]
        parts.append((_sigmoid(og[:, sl]) * xn).astype(BF16))
    return jnp.concatenate(parts, axis=1)


def _convmlp_kernel(kind, pre, split, n_lat_tiles, n_all_tiles, *refs):
    n_pre = {None: 0, "attn": 4, "ml": 11}[pre]
    pre_refs = refs[:n_pre]
    hp_ref, h_ref, hn_ref = refs[n_pre:n_pre + 3]
    ctx_ref = refs[n_pre + 3] if split else None
    mod_ref, g_ref, w1_ref, cw_ref, w2_ref, o_ref, perm_sc = refs[n_pre + 3 + split:]
    j = pl.program_id(1)
    m = mod_ref[0, 0]
    sh, sc, gt = (m[0:1], m[1:2], m[2:3]) if kind == "sc" else (m[3:4], m[4:5], m[5:6])
    x = jnp.concatenate([hp_ref[0], _tile_rows(n_lat_tiles, h_ref, ctx_ref), hn_ref[0]], axis=0)
    rows = x.shape[0]
    if pre == "attn":
        yp_ref, y_ref, yn_ref, wo_ref = pre_refs
        y = jnp.concatenate([yp_ref[0], y_ref[0], yn_ref[0]], axis=0)
        skip = BF16_ROWS - HALO
        x = x + m[2:3] * _dot(y, wo_ref[...])[skip:skip + rows]
    elif pre == "ml":
        ext = [jnp.concatenate([pre_refs[3 * i][0], pre_refs[3 * i + 1][0], pre_refs[3 * i + 2][0]], axis=0)
               for i in range(3)]
        gain_ref, wo_ref = pre_refs[9:]
        x = x + m[2:3] * _dot(_ml_readout(ext[0] + ext[1], ext[2], gain_ref), wo_ref[...])
    a = _rms_mod(x, g_ref[...], sh, sc)
    seg_first = (j == 0) | (j == n_lat_tiles)
    seg_last = (j == n_lat_tiles - 1) | (j == n_all_tiles - 1)
    r = lax.broadcasted_iota(jnp.int32, (rows, 1), 0)
    dead = ((r < HALO) & seg_first) | ((r >= rows - HALO) & seg_last)
    a = jnp.where(dead, 0.0, a)

    pitch = rows // SUBLANES
    n_slab = a.shape[1] // LANES
    for k in range(n_slab):
        perm_sc[k] = a[:, k * LANES:(k + 1) * LANES]
    a = jnp.concatenate(
        [jnp.concatenate([perm_sc[k, pl.ds(r, SUBLANES, stride=pitch), :] for k in range(n_slab)], axis=1)
         for r in range(pitch)], axis=0).astype(BF16)

    def conv3(z, col):
        w = cw_ref[:, col:col + CONV_CHUNK]
        prev = jnp.concatenate([pltpu.roll(z[rows - SUBLANES:], 1, 0), z[:rows - SUBLANES]], axis=0)
        nxt = jnp.concatenate([z[SUBLANES:], pltpu.roll(z[:SUBLANES], SUBLANES - 1, 0)], axis=0)
        return prev * w[0:1] + z * w[1:2] + nxt * w[2:3]

    hidden = w2_ref.shape[0]
    parts = w1_ref.shape[1] // hidden

    def up(c):
        return [_dot(a, w1_ref[:, p * hidden + c * CONV_CHUNK:p * hidden + (c + 1) * CONV_CHUNK]) for p in range(parts)]

    n_chunks = hidden // CONV_CHUNK
    acc = None
    u_next = up(0)
    for c in range(n_chunks):
        u = u_next
        if c + 1 < n_chunks:
            u_next = up(c + 1)
        if kind == "sc":
            hid = u[0] * conv3(u[1] * u[2], c * CONV_CHUNK)
        else:
            gg = conv3(u[0], c * CONV_CHUNK)
            uu = conv3(u[1], hidden + c * CONV_CHUNK)
            hid = gg * _sigmoid(gg) * uu
        y = _dot(hid.astype(BF16), w2_ref[c * CONV_CHUNK:(c + 1) * CONV_CHUNK, :])
        acc = y if acc is None else acc + y

    y = gt * acc
    for r in range(pitch):
        for k in range(n_slab):
            perm_sc[k, pl.ds(r, SUBLANES, stride=pitch), :] = y[r * SUBLANES:(r + 1) * SUBLANES, k * LANES:(k + 1) * LANES]
    o_ref[0] = x[HALO:rows - HALO] + jnp.concatenate([perm_sc[k, HALO:rows - HALO, :] for k in range(n_slab)], axis=1)


def _halo_specs(t, width, halo):
    per = TM // halo
    last = t // halo - 1
    last_tile = t // TM - 1
    return [pl.BlockSpec((1, halo, width), lambda bi, j: (bi, jnp.maximum(j * per - 1, 0), 0)),
            pl.BlockSpec((1, TM, width), lambda bi, j: (bi, jnp.minimum(j, last_tile), 0)),
            pl.BlockSpec((1, halo, width), lambda bi, j: (bi, jnp.minimum((j + 1) * per, last), 0))]


def _convmlp_call(kind, h, ctx, mods, layer, gain, w1, cw, w2, n_lat_tiles, n_all_tiles, n_tiles, pre=None, pre_ins=()):
    b, t_h, d = h.shape
    t = t_h + (0 if ctx is None else ctx.shape[1])
    ctx_row = b
    ctx_specs = [] if ctx is None else [pl.BlockSpec((1, TM, d), lambda bi, j: (bi, 0, 0))]
    ctx_args = () if ctx is None else (ctx,)
    if pre == "attn":
        y, w_o = pre_ins
        pre_specs = _halo_specs(t, y.shape[2], BF16_ROWS) + [_resident(w_o.shape)]
        pre_args = (y, y, y, w_o)
    elif pre == "ml":
        hf, hb, og, hn, w_o = pre_ins
        pre_specs = _halo_specs(t, hf.shape[2], HALO) * 3 + [_resident(hn.shape), _resident(w_o.shape)]
        pre_args = (hf, hf, hf, hb, hb, hb, og, og, og, hn, w_o)
    else:
        pre_specs, pre_args = [], ()
    return pl.pallas_call(
        functools.partial(_convmlp_kernel, kind, pre, ctx is not None, n_lat_tiles, n_all_tiles),
        grid=(b, n_tiles),
        in_specs=pre_specs + _halo_specs(t_h, d, HALO) + ctx_specs + [
            _mod_spec(n_lat_tiles, ctx_row, layer),
            _resident((1, d)), _resident(w1.shape), _resident(cw.shape), _resident(w2.shape)],
        out_specs=pl.BlockSpec((1, TM, d), lambda bi, j: (bi, j, 0)),
        out_shape=jax.ShapeDtypeStruct((b, n_tiles * TM, d), F32),
        scratch_shapes=[pltpu.VMEM((d // LANES, TM + 2 * HALO, LANES), F32)],
        compiler_params=_cparams(("parallel", "arbitrary")),
        name="convmlp_" + kind + ("_" + pre if pre else ""),
    )(*pre_args, h, h, h, *ctx_args, mods, gain, w1, cw, w2)


def _rope_tables(s_lat, n_ctx, hd, reps):
    rows = s_lat // GRID_W
    row = np.repeat(np.arange(rows, dtype=np.float32), GRID_W)
    col = np.tile(np.arange(GRID_W, dtype=np.float32), rows)
    n_freq = hd // 4
    inv = jnp.power(ROPE_THETA, -jnp.arange(n_freq, dtype=F32) / n_freq)
    ang = jnp.concatenate([jnp.asarray(row)[:, None] * inv, jnp.asarray(col)[:, None] * inv], axis=-1)
    cos, sin = jnp.cos(ang), jnp.sin(ang)
    cos = jnp.tile(jnp.concatenate([cos, cos], axis=1), (1, reps))
    sin = jnp.tile(jnp.concatenate([-sin, sin], axis=1), (1, reps))
    cos = jnp.concatenate([cos, jnp.ones((n_ctx, LANES), F32)], axis=0)
    sin = jnp.concatenate([sin, jnp.zeros((n_ctx, LANES), F32)], axis=0)
    return cos, sin


def kernel(x, c, ctx, c_ctx, ada_w, ada_b, norm_mix, norm_ffn, ffn_w_up, ffn_conv, ffn_w_down, win_w_qkv, win_q_norm, win_k_norm, win_sink, win_w_o, sc_w_in, sc_conv, sc_w_out, ax_w_qkv, ax_q_norm, ax_k_norm, ax_w_o, ml_w_in, ml_b_gate, ml_h_norm, ml_w_out):
    b, s_lat, d = x.shape
    n_ctx = ctx.shape[1]
    assert d == D_MODEL and n_ctx == TM and s_lat % FLASH_CK == 0 and s_lat % GRID_W == 0
    assert FLASH_CK % TM == 0 and n_ctx % FLASH_TQ == 0 and n_ctx % ML_CHUNK == 0
    assert n_ctx % WIN_TQ == 0 and WIN_TQ % WINDOW == 0
    n_lat_tiles = s_lat // TM
    n_all_tiles = n_lat_tiles + n_ctx // TM
    depth = ada_w.shape[0]

    h, hc = x, ctx
    pad_rows = -(b + 1) % SUBLANES
    cvec = jnp.concatenate([c, c_ctx[None], jnp.zeros((pad_rows, d), F32)], axis=0)
    mods = _ada_call(cvec, ada_w, ada_b).reshape(depth, b + 1 + pad_rows, 6, d)

    for i in range(depth):
        kind, j = i % N_MIXERS, i // N_MIXERS
        last = i == depth - 1
        n_tiles = n_lat_tiles if last else n_all_tiles
        g_mix = norm_mix[i][None]
        pre, pre_ins = None, ()
        if kind == 0:
            log2e = np.log2(np.e)
            gain = jnp.concatenate([jnp.tile(win_q_norm[j], WIN_HEADS) * (WIN_HD ** -0.5 * log2e),
                                    jnp.tile(win_k_norm[j], WIN_KV)])[None]
            cos, sin = _rope_tables(s_lat, n_ctx, WIN_HD, 2)
            qk, vt = _inproj_call("win", h, hc, mods, i, g_mix, win_w_qkv[j].astype(BF16), (gain, cos, sin), n_lat_tiles)
            sink_b = jnp.broadcast_to((win_sink[j] * log2e)[:, None], (WIN_HEADS, WIN_TQ))
            pre, pre_ins = "attn", (_win_call(qk, vt, sink_b, s_lat, n_ctx), win_w_o[j].astype(BF16))
        elif kind == 1:
            h = _convmlp_call("sc", h, hc, mods, i, g_mix, sc_w_in[j].astype(BF16), sc_conv[j], sc_w_out[j].astype(BF16),
                              n_lat_tiles, n_all_tiles, n_tiles)
            hc = None
        elif kind == 2:
            gain = jnp.concatenate([jnp.tile(ax_q_norm[j], AX_HEADS) * (AX_HD ** -0.5 * np.log2(np.e)),
                                    jnp.tile(ax_k_norm[j], AX_KV)])[None]
            cos, sin = _rope_tables(s_lat, n_ctx, AX_HD, 1)
            qk, vt = _inproj_call("ax", h, hc, mods, i, g_mix, ax_w_qkv[j].astype(BF16), (gain, cos, sin), n_lat_tiles)
            pre, pre_ins = "attn", (_flash_call(qk, vt, s_lat, n_ctx), ax_w_o[j].astype(BF16))
        else:
            w = jnp.concatenate([ml_w_in[j], jnp.zeros((d, LANES - 4 * ML_HEADS), F32)], axis=1).astype(BF16)
            bg = jnp.concatenate([ml_b_gate[j], jnp.zeros((LANES - 4 * ML_HEADS,), F32)])[None]
            qkv, og, gates = _inproj_call("ml", h, hc, mods, i, g_mix, w, (bg,), n_lat_tiles)
            hf, hb = _mlstm_call(qkv, gates, s_lat)
            hn = jnp.tile(ml_h_norm[j], ML_HEADS)[None]
            pre, pre_ins = "ml", (hf, hb, og, hn, ml_w_out[j].astype(BF16))
        h = _convmlp_call("ffn", h, hc, mods, i, norm_ffn[i][None], ffn_w_up[i].astype(BF16), ffn_conv[i], ffn_w_down[i].astype(BF16),
                          n_lat_tiles, n_all_tiles, n_tiles, pre, pre_ins)
        hc = None
    return h[:, :s_lat] if h.shape[1] != s_lat else h
```

```python
import functools

import numpy as np
import jax
import jax.numpy as jnp
from jax import lax
from jax.experimental import pallas as pl
from jax.experimental.pallas import tpu as pltpu

D_MODEL = 1024
GRID_W = 64
N_MIXERS = 4
WINDOW = 128
WIN_HEADS = 16
WIN_KV = 4
WIN_HD = 64
AX_HEADS = 8
AX_KV = 2
AX_HD = 128
ML_HEADS = 4
ML_DK = 128
ML_DV = 256
ML_CHUNK = 256
ROPE_THETA = 10000.0
EPS = 1e-6
NEG = -1e30

F32 = jnp.float32
BF16 = jnp.bfloat16

LANES = 128
SUBLANES = 8
TM = 256
HALO = SUBLANES
CONV_CHUNK = 256
PROJ_CHUNK = 256
FLASH_TQ = 256
FLASH_CK = 512
FLASH_UNROLL = 8
WIN_TQ = 256
BF16_ROWS = 16
VT_ROWS = AX_HD + BF16_ROWS
V7X_VMEM_BYTES = 64 * 1024 * 1024
VMEM_LIMIT = V7X_VMEM_BYTES * 7 // 8


def _cparams(sem):
    return pltpu.CompilerParams(dimension_semantics=sem, vmem_limit_bytes=VMEM_LIMIT)


def _resident(shape):
    nd = len(shape)
    return pl.BlockSpec(shape, lambda *_: (0,) * nd, pipeline_mode=pl.Buffered(1))


def _dot(a, b):
    return jnp.dot(a, b, preferred_element_type=F32)


def _dot_nt(a, b):
    return lax.dot_general(a, b, (((1,), (1,)), ((), ())), preferred_element_type=F32)


def _dot_tn(a, b):
    return lax.dot_general(a, b, (((0,), (0,)), ((), ())), preferred_element_type=F32)


def _sigmoid(x):
    return 1.0 / (1.0 + jnp.exp(-x))


def _rms_mod(x, g, shift, scale):
    y = x * lax.rsqrt(jnp.mean(x * x, axis=-1, keepdims=True) + EPS) * g
    return y * (1.0 + scale) + shift


def _ada_kernel(c_ref, w_ref, b_ref, o_ref):
    c = c_ref[...]
    o_ref[0] = _dot(c * _sigmoid(c), w_ref[0]) + b_ref[0]


def _ada_call(cvec, ada_w, ada_b):
    depth, d, n = ada_w.shape
    nb = n // 4
    return pl.pallas_call(
        _ada_kernel,
        grid=(depth, n // nb),
        in_specs=[pl.BlockSpec(cvec.shape, lambda l, j: (0, 0)),
                  pl.BlockSpec((1, d, nb), lambda l, j: (l, 0, j)),
                  pl.BlockSpec((1, 1, nb), lambda l, j: (l, 0, j))],
        out_specs=pl.BlockSpec((1, cvec.shape[0], nb), lambda l, j: (l, 0, j)),
        out_shape=jax.ShapeDtypeStruct((depth, cvec.shape[0], n), F32),
        compiler_params=_cparams(("arbitrary", "arbitrary")),
        name="ada",
    )(cvec, ada_w, ada_b.reshape(depth, 1, n))


def _tile_rows(n_lat_tiles, h_ref, ctx_ref):
    if ctx_ref is None:
        return h_ref[0]
    return jnp.where(pl.program_id(1) < n_lat_tiles, h_ref[0], ctx_ref[0])


def _inproj_kernel(kind, n_lat_tiles, split, h_ref, *refs):
    ctx_ref = refs[0] if split else None
    mod_ref, g_ref, w_ref, *rest = refs[1:] if split else refs
    m = mod_ref[0, 0]
    a = _rms_mod(_tile_rows(n_lat_tiles, h_ref, ctx_ref), g_ref[...], m[0:1], m[1:2]).astype(BF16)
    tm = a.shape[0]
    lane = lax.broadcasted_iota(jnp.int32, (tm, LANES), 1)

    if kind in ("win", "ax"):
        gain_ref, cos_ref, sin_ref, o_ref, vt_ref = rest
        cos = cos_ref[...]
        sin = sin_ref[...]
        n_rot = gain_ref.shape[1] // LANES
        if kind == "win":
            head_sum = (lax.broadcasted_iota(jnp.int32, (LANES, LANES), 0) // WIN_HD
                        == lax.broadcasted_iota(jnp.int32, (LANES, LANES), 1) // WIN_HD).astype(BF16)

        def emit(t, xt):
            sl = slice(t * LANES, (t + 1) * LANES)
            if t >= n_rot:
                if kind == "win":
                    vt_ref[0, t - n_rot] = xt.T.astype(BF16)
                else:
                    vt_ref[0, t - n_rot, 0, :AX_HD, :] = xt.T.astype(BF16)
                    vt_ref[0, t - n_rot, 0, AX_HD:, :] = jnp.ones((VT_ROWS - AX_HD, tm), BF16)
                return
            x2 = xt * xt
            if kind == "win":
                hi = x2.astype(BF16)
                lo = (x2 - hi.astype(F32)).astype(BF16)
                ms = (_dot(hi, head_sum) + _dot(lo, head_sum)) * (1.0 / WIN_HD)
                xn = xt * lax.rsqrt(ms + EPS) * gain_ref[:, sl]
                first = (lane % WIN_HD) < (WIN_HD // 2)
                sw = jnp.where(first, pltpu.roll(xn, LANES - WIN_HD // 2, 1), pltpu.roll(xn, WIN_HD // 2, 1))
            else:
                ms = jnp.mean(x2, axis=-1, keepdims=True)
                xn = xt * lax.rsqrt(ms + EPS) * gain_ref[:, sl]
                sw = pltpu.roll(xn, AX_HD // 2, 1)
            o_ref[0, :, sl] = (xn * cos + sw * sin).astype(BF16)
    else:
        bg_ref, qkv_ref, og_ref, gate_ref = rest
        nq = ML_HEADS * ML_DK // LANES
        nqkv = nq * 2 + ML_HEADS * ML_DV // LANES
        nog = nqkv + ML_HEADS * ML_DV // LANES

        def emit(t, xt):
            sl = slice(t * LANES, (t + 1) * LANES)
            if t < nq:
                qkv_ref[0, :, sl] = (xt * ML_DK ** -0.5).astype(BF16)
            elif t < nqkv:
                qkv_ref[0, :, sl] = xt.astype(BF16)
            elif t < nog:
                og_ref[0, :, (t - nqkv) * LANES:(t - nqkv + 1) * LANES] = xt
            else:
                g = xt + bg_ref[...]
                is_forget = (lane % (2 * ML_HEADS)) >= ML_HEADS
                log_sig = jnp.minimum(g, 0.0) - jnp.log1p(jnp.exp(-jnp.abs(g)))
                gate_ref[0] = jnp.where(is_forget, log_sig, g)

    n_out = w_ref.shape[1]
    starts = list(range(0, n_out, PROJ_CHUNK))

    def proj(c):
        return _dot(a, w_ref[:, starts[c]:min(starts[c] + PROJ_CHUNK, n_out)])

    y_next = proj(0)
    for c in range(len(starts)):
        y = y_next
        if c + 1 < len(starts):
            y_next = proj(c + 1)
        for i in range(y.shape[1] // LANES):
            emit(starts[c] // LANES + i, y[:, i * LANES:(i + 1) * LANES])


def _mod_spec(n_lat_tiles, ctx_row, layer):
    return pl.BlockSpec((1, 1, 6, D_MODEL), lambda b, j: (layer, jnp.where(j < n_lat_tiles, b, ctx_row), 0, 0))


def _stream_specs(h, ctx, n_lat_tiles):
    d = h.shape[2]
    if ctx is None:
        return h.shape[1], (h,), [pl.BlockSpec((1, TM, d), lambda bi, j: (bi, j, 0))]
    return (h.shape[1] + ctx.shape[1], (h, ctx),
            [pl.BlockSpec((1, TM, d), lambda bi, j: (bi, jnp.minimum(j, n_lat_tiles - 1), 0)),
             pl.BlockSpec((1, TM, d), lambda bi, j: (bi, 0, 0))])


def _inproj_call(kind, h, ctx, mods, layer, gain, w, extras, n_lat_tiles):
    b, _, d = h.shape
    n = w.shape[1]
    ctx_row = b
    t, streams, stream_specs = _stream_specs(h, ctx, n_lat_tiles)
    in_specs = stream_specs + [
                _mod_spec(n_lat_tiles, ctx_row, layer),
                _resident((1, d)),
                _resident(w.shape)]
    if kind in ("win", "ax"):
        hgain, cos, sin = extras
        in_specs += [_resident(hgain.shape),
                     pl.BlockSpec((TM, LANES), lambda bi, j: (j, 0)),
                     pl.BlockSpec((TM, LANES), lambda bi, j: (j, 0))]
        if kind == "win":
            n_v = WIN_KV * WIN_HD // LANES
            n_qk = n - n_v * LANES
            out_shape = (jax.ShapeDtypeStruct((b, t, n_qk), BF16),
                         jax.ShapeDtypeStruct((b, n_v, LANES, t), BF16))
            out_specs = (pl.BlockSpec((1, TM, n_qk), lambda bi, j: (bi, j, 0)),
                         pl.BlockSpec((1, n_v, LANES, TM), lambda bi, j: (bi, 0, 0, j)))
        else:
            n_qk = n - AX_KV * AX_HD
            out_shape = (jax.ShapeDtypeStruct((b, t, n_qk), BF16),
                         jax.ShapeDtypeStruct((b, AX_KV, t // TM, VT_ROWS, TM), BF16))
            out_specs = (pl.BlockSpec((1, TM, n_qk), lambda bi, j: (bi, j, 0)),
                         pl.BlockSpec((1, AX_KV, 1, VT_ROWS, TM), lambda bi, j: (bi, 0, j, 0, 0)))
        args = (hgain, cos, sin)
    else:
        (bg,) = extras
        nqkv = 2 * ML_HEADS * ML_DK + ML_HEADS * ML_DV
        nv = ML_HEADS * ML_DV
        in_specs += [_resident(bg.shape)]
        out_shape = (jax.ShapeDtypeStruct((b, t, nqkv), BF16),
                     jax.ShapeDtypeStruct((b, t, nv), F32),
                     jax.ShapeDtypeStruct((b, t, LANES), F32))
        out_specs = (pl.BlockSpec((1, TM, nqkv), lambda bi, j: (bi, j, 0)),
                     pl.BlockSpec((1, TM, nv), lambda bi, j: (bi, j, 0)),
                     pl.BlockSpec((1, TM, LANES), lambda bi, j: (bi, j, 0)))
        args = (bg,)
    return pl.pallas_call(
        functools.partial(_inproj_kernel, kind, n_lat_tiles, ctx is not None),
        grid=(b, t // TM),
        in_specs=in_specs, out_specs=out_specs, out_shape=out_shape,
        compiler_params=_cparams(("parallel", "arbitrary")),
        name="inproj_" + kind,
    )(*streams, mods, gain, w, *args)


def _win_heads(q_ref, k, vt, bias, sink_ref, o_ref):
    tq = q_ref.shape[1]
    lane = lax.broadcasted_iota(jnp.int32, (tq, LANES), 1)
    lo = lane < WIN_HD
    group = WIN_HEADS // WIN_KV
    if bias is not None:
        bias = jnp.concatenate([bias] * group, axis=1)

    def scores(kv):
        kt, khalf = kv // 2, kv % 2
        qs = []
        for g in range(group):
            head = kv * group + g
            tile, half = head // 2, head % 2
            qt = q_ref[0, :, tile * LANES:(tile + 1) * LANES].astype(F32)
            qh = jnp.where(lo, qt, 0.0) if half == 0 else jnp.where(lo, 0.0, qt)
            if half != khalf:
                qh = pltpu.roll(qh, WIN_HD, 1)
            qs.append(qh.astype(BF16))
        return _dot_nt(k[:, kt * LANES:(kt + 1) * LANES], jnp.concatenate(qs, axis=0))

    s_next = scores(0)
    for kv in range(WIN_KV):
        kt, khalf = kv // 2, kv % 2
        s = s_next
        if kv + 1 < WIN_KV:
            s_next = scores(kv + 1)
        if bias is not None:
            s = s + bias
        sink = jnp.concatenate([sink_ref[kv * group + g:kv * group + g + 1, :tq] for g in range(group)], axis=1)
        mx = jnp.maximum(jnp.max(s, axis=0, keepdims=True), sink)
        e = jnp.exp2(s - mx)
        rows = slice(khalf * WIN_HD, (khalf + 1) * WIN_HD)
        other = (1 - khalf) * WIN_HD
        own = (lax.broadcasted_iota(jnp.int32, vt[kt].shape, 0) // WIN_HD) == khalf
        o = _dot(jnp.where(own, vt[kt], jnp.ones_like(vt[kt])), e.astype(BF16))
        o = o / (o[other:other + 1] + jnp.exp2(sink - mx))
        for pair in range(group // 2):
            both = jnp.concatenate([o[rows, (2 * pair + half) * tq:(2 * pair + half + 1) * tq] for half in range(2)],
                                   axis=0)
            tile = (kv * group) // 2 + pair
            o_ref[0, :, tile * LANES:(tile + 1) * LANES] = both.T.astype(BF16)


def _win_kernel(n_lat, q_ref, kp, kc, kn, kx, vp, vc, vn, vx, sink_ref, o_ref):
    i = pl.program_id(1)
    tq = q_ref.shape[1]
    n_ctx = kx.shape[1]

    @pl.when(i < n_lat)
    def _():
        w = WINDOW
        nk = tq + 2 * w + n_ctx
        col = lax.broadcasted_iota(jnp.int32, (nk, 1), 0)
        t = lax.broadcasted_iota(jnp.int32, (nk, tq), 0) - lax.broadcasted_iota(jnp.int32, (nk, tq), 1)
        big = 4 * nk
        t_min = jnp.where(col < w, jnp.where(i > 0, 0, big), jnp.where(col < tq + 2 * w, 0, -big))
        t_max = jnp.where(col >= tq + 2 * w, big, jnp.where(col >= tq + w, jnp.where(i < n_lat - 1, 2 * w, -big), 2 * w))
        bias = jnp.where(t < t_min, NEG, 0.0) + jnp.where(t > t_max, NEG, 0.0)
        k = jnp.concatenate([kp[0], kc[0], kn[0], kx[0]], axis=0)
        vt = [jnp.concatenate([vp[0, kt], vc[0, kt], vn[0, kt], vx[0, kt]], axis=1) for kt in range(vx.shape[1])]
        _win_heads(q_ref, k, vt, bias, sink_ref, o_ref)

    @pl.when(i >= n_lat)
    def _():
        _win_heads(q_ref, kx[0], [vx[0, kt] for kt in range(vx.shape[1])], None, sink_ref, o_ref)


def _win_call(qk, vt, sink_b, s_lat, n_ctx):
    b, t, _ = qk.shape
    tq = WIN_TQ
    n_lat = s_lat // tq
    nq = t // tq
    dq = WIN_HEADS * WIN_HD
    kw = WIN_KV * WIN_HD
    kcol = dq // kw
    n_vt = vt.shape[1]
    per = tq // WINDOW
    last_w = s_lat // WINDOW - 1
    xrow = s_lat // n_ctx
    prev = lambda i: jnp.clip(i * per - 1, 0, last_w)
    own = lambda i: jnp.minimum(i, n_lat - 1)
    nxt = lambda i: jnp.minimum((i + 1) * per, last_w)
    k_specs = [pl.BlockSpec((1, WINDOW, kw), lambda bi, i: (bi, prev(i), kcol)),
               pl.BlockSpec((1, tq, kw), lambda bi, i: (bi, own(i), kcol)),
               pl.BlockSpec((1, WINDOW, kw), lambda bi, i: (bi, nxt(i), kcol)),
               pl.BlockSpec((1, n_ctx, kw), lambda bi, i: (bi, xrow, kcol))]
    v_specs = [pl.BlockSpec((1, n_vt, LANES, WINDOW), lambda bi, i: (bi, 0, 0, prev(i))),
               pl.BlockSpec((1, n_vt, LANES, tq), lambda bi, i: (bi, 0, 0, own(i))),
               pl.BlockSpec((1, n_vt, LANES, WINDOW), lambda bi, i: (bi, 0, 0, nxt(i))),
               pl.BlockSpec((1, n_vt, LANES, n_ctx), lambda bi, i: (bi, 0, 0, xrow))]
    return pl.pallas_call(
        functools.partial(_win_kernel, n_lat),
        grid=(b, nq),
        in_specs=[pl.BlockSpec((1, tq, dq), lambda bi, i: (bi, i, 0))] + k_specs + v_specs + [_resident(sink_b.shape)],
        out_specs=pl.BlockSpec((1, tq, dq), lambda bi, i: (bi, i, 0)),
        out_shape=jax.ShapeDtypeStruct((b, t, dq), BF16),
        compiler_params=_cparams(("parallel", "arbitrary")),
        name="win_attn",
    )(qk, *([qk] * 4), *([vt] * 4), sink_b)


def _flash_kernel(n_lat_q, n_full, s_lat, n_ctx, q_ref, k_ref, vt_ref, o_ref, m_sc, acc_sc, s_sc):
    qi = pl.program_id(2)
    tq = q_ref.shape[1]
    group = AX_HEADS // AX_KV
    per = FLASH_CK // TM
    q = jnp.concatenate([q_ref[0, :, g * AX_HD:(g + 1) * AX_HD] for g in range(group)], axis=0)

    m_sc[...] = jnp.full(m_sc.shape, NEG, F32)
    acc_sc[...] = jnp.zeros(acc_sc.shape, F32)

    def scores(slot, start, size):
        s_sc[slot, :size, :] = _dot_nt(k_ref[0, pl.ds(start, size), :], q)

    def update(slot, block, size):
        s = s_sc[slot, :size, :]
        m_prev = m_sc[...]
        m_next = jnp.maximum(m_prev, jnp.max(s, axis=0, keepdims=True))
        p = jnp.exp2(s - m_next).astype(BF16)
        alpha = jnp.exp2(m_prev - m_next)
        pv = None
        for i in range(size // TM):
            part = _dot(vt_ref[0, 0, block + i], p[i * TM:(i + 1) * TM])
            pv = part if pv is None else pv + part
        acc_sc[...] = acc_sc[...] * alpha + pv
        m_sc[...] = m_next

    @pl.when(qi < n_lat_q)
    def _():
        def at(c):
            return pl.multiple_of(c * FLASH_CK, FLASH_CK)

        scores(0, 0, FLASH_CK)

        def body(i, carry):
            c = FLASH_UNROLL * i
            for u in range(FLASH_UNROLL):
                scores((u + 1) % 2, at(c + u + 1), FLASH_CK)
                update(u % 2, (c + u) * per, FLASH_CK)
            return carry

        n_iter = (n_full - 1) // FLASH_UNROLL
        lax.fori_loop(0, n_iter, body, 0)
        tail = [(c * FLASH_CK, c * per, FLASH_CK) for c in range(FLASH_UNROLL * n_iter, n_full)]
        tail.append((s_lat, s_lat // TM, n_ctx))
        for idx, (start, block, size) in enumerate(tail):
            if idx + 1 < len(tail):
                scores((idx + 1) % 2, tail[idx + 1][0], tail[idx + 1][2])
            update(idx % 2, block, size)

    @pl.when(qi >= n_lat_q)
    def _():
        scores(0, s_lat, n_ctx)
        update(0, s_lat // TM, n_ctx)

    acc = acc_sc[...]
    o = acc[:AX_HD] / acc[AX_HD:AX_HD + 1]
    for g in range(group):
        o_ref[0, :, g * AX_HD:(g + 1) * AX_HD] = o[:, g * tq:(g + 1) * tq].T.astype(BF16)


def _flash_call(qk, vt, s_lat, n_ctx):
    b, t, _ = qk.shape
    tq = FLASH_TQ
    group = AX_HEADS // AX_KV
    gw = group * AX_HD
    dq = AX_HEADS * AX_HD
    kcol = dq // AX_HD
    m_cols = group * tq
    return pl.pallas_call(
        functools.partial(_flash_kernel, s_lat // tq, s_lat // FLASH_CK, s_lat, n_ctx),
        grid=(b, AX_KV, t // tq),
        in_specs=[pl.BlockSpec((1, tq, gw), lambda bi, j, i: (bi, i, j)),
                  pl.BlockSpec((1, t, AX_HD), lambda bi, j, i: (bi, 0, kcol + j)),
                  pl.BlockSpec((1, 1) + vt.shape[2:], lambda bi, j, i: (bi, j, 0, 0, 0))],
        out_specs=pl.BlockSpec((1, tq, gw), lambda bi, j, i: (bi, i, j)),
        out_shape=jax.ShapeDtypeStruct((b, t, dq), BF16),
        scratch_shapes=[pltpu.VMEM((1, m_cols), F32), pltpu.VMEM((VT_ROWS, m_cols), F32),
                        pltpu.VMEM((2, FLASH_CK, m_cols), F32)],
        compiler_params=_cparams(("parallel", "parallel", "arbitrary")),
        name="flash_attn",
    )(qk, qk, vt)


def _mlstm_kernel(qf_ref, gf_ref, qb_ref, gb_ref, hf_ref, hb_ref, c_sc, n_sc, m_sc):
    step = pl.program_id(1)
    L = ML_CHUNK

    @pl.when(step == 0)
    def _():
        c_sc[...] = jnp.zeros(c_sc.shape, F32)
        n_sc[...] = jnp.zeros(n_sc.shape, F32)
        m_sc[...] = jnp.full(m_sc.shape, NEG, F32)

    r = lax.broadcasted_iota(jnp.int32, (L, L), 0)
    c = lax.broadcasted_iota(jnp.int32, (L, L), 1)
    lo = c <= r
    up = c >= r
    lo_f = lo.astype(F32)
    up_f = up.astype(F32)
    hi = lax.Precision.HIGHEST
    nqk = ML_HEADS * ML_DK

    dirs = ((qf_ref, gf_ref, hf_ref), (qb_ref, gb_ref, hb_ref))
    n_streams = 2 * qf_ref.shape[0]
    units = [(s, h) for s in range(n_streams) for h in range(ML_HEADS)]

    def qkv(s, h):
        x_ref, bb = dirs[s % 2][0], s // 2
        return (x_ref[bb, :, h * ML_DK:(h + 1) * ML_DK],
                x_ref[bb, :, nqk + h * ML_DK:nqk + (h + 1) * ML_DK],
                x_ref[bb, :, 2 * nqk + h * ML_DV:2 * nqk + (h + 1) * ML_DV])

    stats = []
    for s in range(n_streams):
        d = s % 2
        gates = dirs[d][1][s // 2]
        gates_t = gates.T
        tri_col, tri_row = (lo_f, up_f) if d == 0 else (up_f, lo_f)
        cum_col = jnp.dot(tri_col, gates, precision=hi, preferred_element_type=F32)
        cum_row = jnp.dot(gates_t, tri_row, precision=hi, preferred_element_type=F32)
        total = jnp.sum(gates, axis=0, keepdims=True)
        stats.append((gates, gates_t, cum_col, cum_row, total))

    s_mat, q_c = {}, {}
    for s, h in units:
        q, k, _ = qkv(s, h)
        s_mat[s, h] = _dot_nt(q, k)
        q_c[s, h] = _dot_nt(q, c_sc[s, h].astype(BF16))

    c_bar, n_bar, m_bar = {}, {}, {}
    for s, h in units:
        gates, _, cum_col, _, total = stats[s]
        ci = 2 * ML_HEADS * (s % 2) + h
        cf = ci + ML_HEADS
        _, k, v = qkv(s, h)
        a_col = total[:, cf:cf + 1] - cum_col[:, cf:cf + 1] + gates[:, ci:ci + 1]
        m_bar[s, h] = jnp.max(a_col, axis=0, keepdims=True)
        w_col = jnp.exp(a_col - m_bar[s, h])
        c_bar[s, h] = _dot_tn((v.astype(F32) * w_col).astype(BF16), k)
        n_bar[s, h] = jnp.sum(k.astype(F32) * w_col, axis=0, keepdims=True)

    num, den, m_ts = {}, {}, {}
    for s, h in units:
        _, gates_t, cum_col, cum_row, _ = stats[s]
        mask = lo if s % 2 == 0 else up
        ci = 2 * ML_HEADS * (s % 2) + h
        cf = ci + ML_HEADS
        q, _, v = qkv(s, h)
        n_prev = n_sc[s, h][0:1, :]
        m_prev = m_sc[s, h][0:1, 0:1]
        f_col = cum_col[:, cf:cf + 1]
        dmat = jnp.where(mask, f_col - cum_row[cf:cf + 1, :] + gates_t[ci:ci + 1, :], NEG)
        inter = f_col + m_prev
        m_t = jnp.maximum(inter, jnp.max(dmat, axis=-1, keepdims=True))
        w_inter = jnp.exp(inter - m_t)
        qk = s_mat[s, h] * jnp.exp(dmat - m_t)
        qn = jnp.sum(q.astype(F32) * n_prev, axis=-1, keepdims=True)
        num[s, h] = _dot(qk.astype(BF16), v) + w_inter * q_c[s, h]
        den[s, h] = jnp.sum(qk, axis=-1, keepdims=True) + w_inter * qn
        m_ts[s, h] = m_t

    for s, h in units:
        dirs[s % 2][2][s // 2, :, h * ML_DV:(h + 1) * ML_DV] = (
            num[s, h] / jnp.maximum(jnp.abs(den[s, h]), jnp.exp(-m_ts[s, h])))

    for s, h in units:
        total = stats[s][4]
        cf = 2 * ML_HEADS * (s % 2) + h + ML_HEADS
        f_tot = total[:, cf:cf + 1]
        n_prev = n_sc[s, h][0:1, :]
        m_prev = m_sc[s, h][0:1, 0:1]
        m_new = jnp.maximum(f_tot + m_prev, m_bar[s, h])
        decay = jnp.exp(f_tot + m_prev - m_new)
        inj = jnp.exp(m_bar[s, h] - m_new)
        c_sc[s, h] = decay * c_sc[s, h] + inj * c_bar[s, h]
        n_sc[s, h] = jnp.broadcast_to(decay * n_prev + inj * n_bar[s, h], (SUBLANES, ML_DK))
        m_sc[s, h] = jnp.broadcast_to(m_new, (SUBLANES, LANES))


def _mlstm_call(qkv, gates, s_lat):
    b, t, nx = qkv.shape
    nc = t // ML_CHUNK
    nlc = s_lat // ML_CHUNK
    nv = ML_HEADS * ML_DV

    def fwd(bi, s):
        return (bi, (s + nlc) % nc, 0)

    def bwd(bi, s):
        return (bi, nc - 1 - s, 0)

    nb = 1
    return pl.pallas_call(
        _mlstm_kernel,
        grid=(b // nb, nc),
        in_specs=[pl.BlockSpec((nb, ML_CHUNK, nx), fwd), pl.BlockSpec((nb, ML_CHUNK, LANES), fwd),
                  pl.BlockSpec((nb, ML_CHUNK, nx), bwd), pl.BlockSpec((nb, ML_CHUNK, LANES), bwd)],
        out_specs=(pl.BlockSpec((nb, ML_CHUNK, nv), fwd), pl.BlockSpec((nb, ML_CHUNK, nv), bwd)),
        out_shape=(jax.ShapeDtypeStruct((b, t, nv), F32), jax.ShapeDtypeStruct((b, t, nv), F32)),
        scratch_shapes=[pltpu.VMEM((2 * nb, ML_HEADS, ML_DV, ML_DK), F32),
                        pltpu.VMEM((2 * nb, ML_HEADS, SUBLANES, ML_DK), F32),
                        pltpu.VMEM((2 * nb, ML_HEADS, SUBLANES, LANES), F32)],
        compiler_params=_cparams(("parallel", "arbitrary")),
        name="mlstm",
    )(qkv, gates, qkv, gates)


def _ml_readout(hs, og, hn_ref):
    parts = []
    for h in range(ML_HEADS):
        sl = slice(h * ML_DV, (h + 1) * ML_DV)
        x = hs[:, sl]
        xn = x * lax.rsqrt(jnp.mean(x * x, axis=-1, keepdims=True) + EPS) * hn_ref[:, sl]
        parts.append((_sigmoid(og[:, sl]) * xn).astype(BF16))
    return jnp.concatenate(parts, axis=1)


def _convmlp_kernel(kind, pre, split, n_lat_tiles, n_all_tiles, *refs):
    n_pre = {None: 0, "attn": 4, "ml": 11}[pre]
    pre_refs = refs[:n_pre]
    hp_ref, h_ref, hn_ref = refs[n_pre:n_pre + 3]
    ctx_ref = refs[n_pre + 3] if split else None
    mod_ref, g_ref, w1_ref, cw_ref, w2_ref, o_ref, perm_sc = refs[n_pre + 3 + split:]
    j = pl.program_id(1)
    m = mod_ref[0, 0]
    sh, sc, gt = (m[0:1], m[1:2], m[2:3]) if kind == "sc" else (m[3:4], m[4:5], m[5:6])
    x = jnp.concatenate([hp_ref[0], _tile_rows(n_lat_tiles, h_ref, ctx_ref), hn_ref[0]], axis=0)
    rows = x.shape[0]
    if pre == "attn":
        yp_ref, y_ref, yn_ref, wo_ref = pre_refs
        y = jnp.concatenate([yp_ref[0], y_ref[0], yn_ref[0]], axis=0)
        skip = BF16_ROWS - HALO
        x = x + m[2:3] * _dot(y, wo_ref[...])[skip:skip + rows]
    elif pre == "ml":
        ext = [jnp.concatenate([pre_refs[3 * i][0], pre_refs[3 * i + 1][0], pre_refs[3 * i + 2][0]], axis=0)
               for i in range(3)]
        gain_ref, wo_ref = pre_refs[9:]
        x = x + m[2:3] * _dot(_ml_readout(ext[0] + ext[1], ext[2], gain_ref), wo_ref[...])
    a = _rms_mod(x, g_ref[...], sh, sc)
    seg_first = (j == 0) | (j == n_lat_tiles)
    seg_last = (j == n_lat_tiles - 1) | (j == n_all_tiles - 1)
    r = lax.broadcasted_iota(jnp.int32, (rows, 1), 0)
    dead = ((r < HALO) & seg_first) | ((r >= rows - HALO) & seg_last)
    a = jnp.where(dead, 0.0, a)

    pitch = rows // SUBLANES
    n_slab = a.shape[1] // LANES
    for k in range(n_slab):
        perm_sc[k] = a[:, k * LANES:(k + 1) * LANES]
    a = jnp.concatenate(
        [jnp.concatenate([perm_sc[k, pl.ds(r, SUBLANES, stride=pitch), :] for k in range(n_slab)], axis=1)
         for r in range(pitch)], axis=0).astype(BF16)

    def conv3(z, col):
        w = cw_ref[:, col:col + CONV_CHUNK]
        prev = jnp.concatenate([pltpu.roll(z[rows - SUBLANES:], 1, 0), z[:rows - SUBLANES]], axis=0)
        nxt = jnp.concatenate([z[SUBLANES:], pltpu.roll(z[:SUBLANES], SUBLANES - 1, 0)], axis=0)
        return prev * w[0:1] + z * w[1:2] + nxt * w[2:3]

    hidden = w2_ref.shape[0]
    parts = w1_ref.shape[1] // hidden

    def up(c):
        return [_dot(a, w1_ref[:, p * hidden + c * CONV_CHUNK:p * hidden + (c + 1) * CONV_CHUNK]) for p in range(parts)]

    n_chunks = hidden // CONV_CHUNK
    acc = None
    u_next = up(0)
    for c in range(n_chunks):
        u = u_next
        if c + 1 < n_chunks:
            u_next = up(c + 1)
        if kind == "sc":
            hid = u[0] * conv3(u[1] * u[2], c * CONV_CHUNK)
        else:
            gg = conv3(u[0], c * CONV_CHUNK)
            uu = conv3(u[1], hidden + c * CONV_CHUNK)
            hid = gg * _sigmoid(gg) * uu
        y = _dot(hid.astype(BF16), w2_ref[c * CONV_CHUNK:(c + 1) * CONV_CHUNK, :])
        acc = y if acc is None else acc + y

    y = gt * acc
    for r in range(pitch):
        for k in range(n_slab):
            perm_sc[k, pl.ds(r, SUBLANES, stride=pitch), :] = y[r * SUBLANES:(r + 1) * SUBLANES, k * LANES:(k + 1) * LANES]
    o_ref[0] = x[HALO:rows - HALO] + jnp.concatenate([perm_sc[k, HALO:rows - HALO, :] for k in range(n_slab)], axis=1)


def _halo_specs(t, width, halo):
    per = TM // halo
    last = t // halo - 1
    last_tile = t // TM - 1
    return [pl.BlockSpec((1, halo, width), lambda bi, j: (bi, jnp.maximum(j * per - 1, 0), 0)),
            pl.BlockSpec((1, TM, width), lambda bi, j: (bi, jnp.minimum(j, last_tile), 0)),
            pl.BlockSpec((1, halo, width), lambda bi, j: (bi, jnp.minimum((j + 1) * per, last), 0))]


def _convmlp_call(kind, h, ctx, mods, layer, gain, w1, cw, w2, n_lat_tiles, n_all_tiles, n_tiles, pre=None, pre_ins=()):
    b, t_h, d = h.shape
    t = t_h + (0 if ctx is None else ctx.shape[1])
    ctx_row = b
    ctx_specs = [] if ctx is None else [pl.BlockSpec((1, TM, d), lambda bi, j: (bi, 0, 0))]
    ctx_args = () if ctx is None else (ctx,)
    if pre == "attn":
        y, w_o = pre_ins
        pre_specs = _halo_specs(t, y.shape[2], BF16_ROWS) + [_resident(w_o.shape)]
        pre_args = (y, y, y, w_o)
    elif pre == "ml":
        hf, hb, og, hn, w_o = pre_ins
        pre_specs = _halo_specs(t, hf.shape[2], HALO) * 3 + [_resident(hn.shape), _resident(w_o.shape)]
        pre_args = (hf, hf, hf, hb, hb, hb, og, og, og, hn, w_o)
    else:
        pre_specs, pre_args = [], ()
    return pl.pallas_call(
        functools.partial(_convmlp_kernel, kind, pre, ctx is not None, n_lat_tiles, n_all_tiles),
        grid=(b, n_tiles),
        in_specs=pre_specs + _halo_specs(t_h, d, HALO) + ctx_specs + [
            _mod_spec(n_lat_tiles, ctx_row, layer),
            _resident((1, d)), _resident(w1.shape), _resident(cw.shape), _resident(w2.shape)],
        out_specs=pl.BlockSpec((1, TM, d), lambda bi, j: (bi, j, 0)),
        out_shape=jax.ShapeDtypeStruct((b, n_tiles * TM, d), F32),
        scratch_shapes=[pltpu.VMEM((d // LANES, TM + 2 * HALO, LANES), F32)],
        compiler_params=_cparams(("parallel", "arbitrary")),
        name="convmlp_" + kind + ("_" + pre if pre else ""),
    )(*pre_args, h, h, h, *ctx_args, mods, gain, w1, cw, w2)


def _rope_tables(s_lat, n_ctx, hd, reps):
    rows = s_lat // GRID_W
    row = np.repeat(np.arange(rows, dtype=np.float64), GRID_W)
    col = np.tile(np.arange(GRID_W, dtype=np.float64), rows)
    n_freq = hd // 4
    inv = np.power(ROPE_THETA, -np.arange(n_freq, dtype=np.float64) / n_freq)
    ang = np.concatenate([row[:, None] * inv, col[:, None] * inv], axis=-1)
    cos, sin = np.cos(ang), np.sin(ang)
    cos = np.tile(np.concatenate([cos, cos], axis=1), (1, reps))
    sin = np.tile(np.concatenate([-sin, sin], axis=1), (1, reps))
    cos = np.concatenate([cos, np.ones((n_ctx, LANES))], axis=0)
    sin = np.concatenate([sin, np.zeros((n_ctx, LANES))], axis=0)
    return jnp.asarray(cos, F32), jnp.asarray(sin, F32)


def kernel(x, c, ctx, c_ctx, ada_w, ada_b, norm_mix, norm_ffn, ffn_w_up, ffn_conv, ffn_w_down, win_w_qkv, win_q_norm, win_k_norm, win_sink, win_w_o, sc_w_in, sc_conv, sc_w_out, ax_w_qkv, ax_q_norm, ax_k_norm, ax_w_o, ml_w_in, ml_b_gate, ml_h_norm, ml_w_out):
    b, s_lat, d = x.shape
    n_ctx = ctx.shape[1]
    assert d == D_MODEL and n_ctx == TM and s_lat % FLASH_CK == 0 and s_lat % GRID_W == 0
    assert FLASH_CK % TM == 0 and n_ctx % FLASH_TQ == 0 and n_ctx % ML_CHUNK == 0
    assert n_ctx % WIN_TQ == 0 and WIN_TQ % WINDOW == 0
    n_lat_tiles = s_lat // TM
    n_all_tiles = n_lat_tiles + n_ctx // TM
    depth = ada_w.shape[0]

    h, hc = x, ctx
    pad_rows = -(b + 1) % SUBLANES
    cvec = jnp.concatenate([c, c_ctx[None], jnp.zeros((pad_rows, d), F32)], axis=0)
    mods = _ada_call(cvec, ada_w, ada_b).reshape(depth, b + 1 + pad_rows, 6, d)

    for i in range(depth):
        kind, j = i % N_MIXERS, i // N_MIXERS
        last = i == depth - 1
        n_tiles = n_lat_tiles if last else n_all_tiles
        g_mix = norm_mix[i][None]
        pre, pre_ins = None, ()
        if kind == 0:
            log2e = np.log2(np.e)
            gain = jnp.concatenate([jnp.tile(win_q_norm[j], WIN_HEADS) * (WIN_HD ** -0.5 * log2e),
                                    jnp.tile(win_k_norm[j], WIN_KV)])[None]
            cos, sin = _rope_tables(s_lat, n_ctx, WIN_HD, 2)
            qk, vt = _inproj_call("win", h, hc, mods, i, g_mix, win_w_qkv[j].astype(BF16), (gain, cos, sin), n_lat_tiles)
            sink_b = jnp.broadcast_to((win_sink[j] * log2e)[:, None], (WIN_HEADS, WIN_TQ))
            pre, pre_ins = "attn", (_win_call(qk, vt, sink_b, s_lat, n_ctx), win_w_o[j].astype(BF16))
        elif kind == 1:
            h = _convmlp_call("sc", h, hc, mods, i, g_mix, sc_w_in[j].astype(BF16), sc_conv[j], sc_w_out[j].astype(BF16),
                              n_lat_tiles, n_all_tiles, n_tiles)
            hc = None
        elif kind == 2:
            gain = jnp.concatenate([jnp.tile(ax_q_norm[j], AX_HEADS) * (AX_HD ** -0.5 * np.log2(np.e)),
                                    jnp.tile(ax_k_norm[j], AX_KV)])[None]
            cos, sin = _rope_tables(s_lat, n_ctx, AX_HD, 1)
            qk, vt = _inproj_call("ax", h, hc, mods, i, g_mix, ax_w_qkv[j].astype(BF16), (gain, cos, sin), n_lat_tiles)
            pre, pre_ins = "attn", (_flash_call(qk, vt, s_lat, n_ctx), ax_w_o[j].astype(BF16))
        else:
            w = jnp.concatenate([ml_w_in[j], jnp.zeros((d, LANES - 4 * ML_HEADS), F32)], axis=1).astype(BF16)
            bg = jnp.concatenate([ml_b_gate[j], jnp.zeros((LANES - 4 * ML_HEADS,), F32)])[None]
            qkv, og, gates = _inproj_call("ml", h, hc, mods, i, g_mix, w, (bg,), n_lat_tiles)
            hf, hb = _mlstm_call(qkv, gates, s_lat)
            hn = jnp.tile(ml_h_norm[j], ML_HEADS)[None]
            pre, pre_ins = "ml", (hf, hb, og, hn, ml_w_out[j].astype(BF16))
        h = _convmlp_call("ffn", h, hc, mods, i, norm_ffn[i][None], ffn_w_up[i].astype(BF16), ffn_conv[i], ffn_w_down[i].astype(BF16),
                          n_lat_tiles, n_all_tiles, n_tiles, pre, pre_ins)
        hc = None
    return h[:, :s_lat] if h.shape[1] != s_lat else h
```
